```python
import math
import jax
import jax.numpy as jnp
from jax import lax
import numpy as np

D_MODEL = 2048
BATCH = 32
SEQ = 256
DEPTH = 4
DEC_BATCH = 4
DEC_SEQ = 1024
PAST_LEN = 512

GRID_W = 64
N_MIXERS = 2
N_GLA = (DEPTH + 1) // 2
N_SGU = DEPTH // 2
GLA_HEADS = 4
GLA_DK = D_MODEL // 2 // GLA_HEADS
GLA_DV = D_MODEL // GLA_HEADS
GLA_HK = GLA_HEADS * GLA_DK
GLA_HV = GLA_HEADS * GLA_DV
GLA_RANK = 16
GLA_NORMALIZER = 16.0
GLA_CHUNK = 64
SGU_FFN = 6 * D_MODEL
SGU_HALF = SGU_FFN // 2
SGU_GROUPS = 4
SGU_CHUNK = 128
N_EXPERTS = 64
TOP_K = 8
N_GROUPS = 8
TOPK_GROUPS = 4
EXPERT_FF = 512
ROUTE_SCALE = 2.5
MOE_BLOCK = 128
DN_ALPHA = (2 * DEPTH) ** 0.25
DN_BETA = (8 * DEPTH) ** -0.25
LN_EPS = 1e-5
RMS_EPS = 1e-6

kernel_name = "hybrid_gla_sgu_moe_diffusion_step"


def layer_norm(x, g, b):
    xf = x.astype(jnp.float32)
    mu = jnp.mean(xf, axis=-1, keepdims=True)
    var = jnp.mean(jnp.square(xf - mu), axis=-1, keepdims=True)
    return ((xf - mu) * lax.rsqrt(var + LN_EPS)).astype(x.dtype) * g + b


def rms_norm(x, g):
    xf = x.astype(jnp.float32)
    return (xf * lax.rsqrt(jnp.mean(xf * xf, axis=-1, keepdims=True) + RMS_EPS)).astype(g.dtype) * g


def modulate(x, shift, scale):
    return x * (1 + scale) + shift


def post_norm(x, out, gate, g, b):
    return layer_norm(DN_ALPHA * x + gate * out, g, b)


def grid_pos_embed(rows, dtype):
    t = jnp.arange(rows * GRID_W)
    r = (t // GRID_W).astype(jnp.float32)
    col = (t % GRID_W).astype(jnp.float32)
    quarter = D_MODEL // 4
    omega = 1.0 / (10000.0 ** (jnp.arange(quarter, dtype=jnp.float32) / quarter))
    def emb(p):
        a = p[:, None] * omega[None, :]
        return jnp.concatenate([jnp.sin(a), jnp.cos(a)], axis=-1)
    return jnp.concatenate([emb(r), emb(col)], axis=-1).astype(dtype)


def gla_scan(q, k, v, gk, s0):
    b, t = q.shape[:2]
    n = t // GLA_CHUNK
    def chunks(a):
        return a.astype(jnp.float32).reshape(b, n, GLA_CHUNK, *a.shape[2:])
    q, k, v, gk = chunks(q), chunks(k), chunks(v), chunks(gk)
    g_cum = jnp.cumsum(gk, axis=2)
    g_ref = g_cum[:, :, GLA_CHUNK // 2 - 1:GLA_CHUNK // 2]
    g_last = g_cum[:, :, -1:]
    q_in = q * jnp.exp(g_cum - g_ref)
    k_in = k * jnp.exp(g_ref - g_cum)
    lower = jnp.tril(jnp.ones((GLA_CHUNK, GLA_CHUNK), dtype=bool))
    a = jnp.einsum('bnihd,bnjhd->bnhij', q_in, k_in)
    a = jnp.where(lower, a, 0.0)
    o_intra = jnp.einsum('bnhij,bnjhe->bnihe', a, v)
    q_dec = q * jnp.exp(g_cum)
    k_dec = k * jnp.exp(g_last - g_cum)
    u = jnp.einsum('bnchd,bnche->nbhde', k_dec, v)
    chunk_decay = jnp.exp(g_last[:, :, 0]).transpose(1, 0, 2, 3)
    def step(s, inp):
        dec, uc = inp
        return dec[..., None] * s + uc, s
    s_final, s_enter = lax.scan(step, s0.astype(jnp.float32), (chunk_decay, u))
    o_inter = jnp.einsum('bnchd,nbhde->bnche', q_dec, s_enter)
    return (o_intra + o_inter).reshape(b, t, GLA_HEADS, GLA_DV), s_final


def gla_mixer(h, s0, w_in, w_a1, w_a2, b_a, norm_g, w_out):
    b, t, _ = h.shape
    proj = h @ w_in
    q, k, v, g = jnp.split(proj, [GLA_HK, 2 * GLA_HK, 2 * GLA_HK + GLA_HV], axis=-1)
    q = q.reshape(b, t, GLA_HEADS, GLA_DK) * (GLA_DK ** -0.5)
    k = k.reshape(b, t, GLA_HEADS, GLA_DK)
    v = v.reshape(b, t, GLA_HEADS, GLA_DV)
    low = jnp.einsum('btd,zdr->zbtr', h, w_a1)
    pre = jnp.einsum('zbtr,zrk->zbtk', low, w_a2) + b_a[:, None, None, :]
    gk = (jax.nn.log_sigmoid(pre.astype(jnp.float32)) / GLA_NORMALIZER).reshape(2, b, t, GLA_HEADS, GLA_DK)
    o_f, s_f = gla_scan(q, k, v, gk[0], s0[:, 0])
    o_b, s_b = gla_scan(q[:, ::-1], k[:, ::-1], v[:, ::-1], gk[1][:, ::-1], s0[:, 1])
    o = rms_norm(o_f + o_b[:, ::-1], norm_g).astype(h.dtype)
    o = o * jax.nn.silu(g.reshape(b, t, GLA_HEADS, GLA_DV))
    out = o.reshape(b, t, GLA_HV) @ w_out
    return out, jnp.stack([s_f, s_b], axis=1).astype(h.dtype)


def sgu_mixer(h, w_in, ln_g, ln_b, w_s, b_s, w_out):
    b, t, _ = h.shape
    z = jax.nn.gelu(h @ w_in, approximate=False)
    u, v = jnp.split(z, 2, axis=-1)
    v = layer_norm(v, ln_g, ln_b)
    n = t // SGU_CHUNK
    v = v.reshape(b, n, SGU_CHUNK, SGU_GROUPS, SGU_HALF // SGU_GROUPS)
    s = jnp.einsum('gij,bnjgc->bnigc', w_s, v) + b_s.T[None, None, :, :, None]
    return (u * s.reshape(b, t, SGU_HALF)) @ w_out


def route(x, w_r, b_r):
    t = x.shape[0]
    scores = jax.nn.sigmoid((x @ w_r).astype(jnp.float32))
    sel = scores + b_r.astype(jnp.float32)
    grp = sel.reshape(t, N_GROUPS, N_EXPERTS // N_GROUPS)
    grp_score = jnp.sum(lax.top_k(grp, 2)[0], axis=-1)
    _, gidx = lax.top_k(grp_score, TOPK_GROUPS)
    gmask = jnp.any(gidx[..., None] == jnp.arange(N_GROUPS), axis=-2)
    emask = jnp.repeat(gmask, N_EXPERTS // N_GROUPS, axis=-1)
    _, eidx = lax.top_k(jnp.where(emask, sel, -jnp.inf), TOP_K)
    w = jnp.take_along_axis(scores, eidx, axis=-1)
    w = w / jnp.sum(w, axis=-1, keepdims=True) * ROUTE_SCALE
    return eidx, w


def routed_experts(x, eidx, ew, w_gu, w_down):
    t = x.shape[0]
    n_assign = t * TOP_K
    n_blocks = -(-n_assign // MOE_BLOCK) + N_EXPERTS
    cap = n_blocks * MOE_BLOCK
    flat_e = eidx.reshape(-1)
    order = jnp.argsort(flat_e)
    sorted_e = flat_e[order]
    counts = jnp.zeros((N_EXPERTS,), jnp.int32).at[flat_e].add(1)
    padded = (counts + MOE_BLOCK - 1) // MOE_BLOCK * MOE_BLOCK
    start = jnp.cumsum(counts) - counts
    pend = jnp.cumsum(padded)
    pstart = pend - padded
    dest = pstart[sorted_e] + jnp.arange(n_assign) - start[sorted_e]
    tok = jnp.zeros((cap,), jnp.int32).at[dest].set((order // TOP_K).astype(jnp.int32))
    wts = jnp.zeros((cap,), x.dtype).at[dest].set(ew.reshape(-1)[order])
    block_e = jnp.minimum(jnp.searchsorted(pend, jnp.arange(n_blocks) * MOE_BLOCK, side='right'), N_EXPERTS - 1)
    def expert_block(args):
        e, t_blk, w_blk = args
        xb = x[t_blk]
        gate, up = jnp.split(xb @ w_gu[e], 2, axis=-1)
        return ((jax.nn.silu(gate) * up) @ w_down[e]) * w_blk[:, None]
    yb = lax.map(expert_block, (block_e, tok.reshape(n_blocks, MOE_BLOCK), wts.reshape(n_blocks, MOE_BLOCK)))
    return jax.ops.segment_sum(yb.reshape(cap, -1), tok, num_segments=t)


def moe(h, w_r, b_r, w_gu, w_down, ws_gu, ws_down):
    b, t, d = h.shape
    x = h.reshape(b * t, d)
    eidx, ew = route(x, w_r, b_r)
    y = routed_experts(x, eidx, ew.astype(x.dtype), w_gu, w_down)
    g, u = jnp.split(x @ ws_gu, 2, axis=-1)
    y = y + (jax.nn.silu(g) * u) @ ws_down
    return y.reshape(b, t, d)


def setup_inputs(seed: int = 0) -> dict:
    key = jax.random.key(seed)
    ks = iter(jax.random.split(key, 32))
    f32 = jnp.float32
    def nrm(shape, scale):
        return jax.random.normal(next(ks), shape, f32) * scale
    d = D_MODEL
    return {
        'x_prompt': nrm((BATCH, SEQ, d), 1.0),
        'x_sample': nrm((DEC_BATCH, DEC_SEQ, d), 1.0),
        'state_gla': nrm((DEC_BATCH, N_GLA, 2, GLA_HEADS, GLA_DK, GLA_DV), 1.0),
        'c': nrm((DEC_BATCH, d), 1.0),
        'c_ctx': nrm((d,), 1.0),
        'w_ada': nrm((DEPTH, d, 6 * d), 0.5 * d ** -0.5),
        'b_ada': nrm((DEPTH, 6 * d), 0.01),
        'ln_g': 1.0 + nrm((DEPTH, 2, d), 0.02),
        'ln_b': nrm((DEPTH, 2, d), 0.02),
        'gla_w_in': nrm((N_GLA, d, 2 * GLA_HK + 2 * GLA_HV), d ** -0.5),
        'gla_w_a1': nrm((N_GLA, 2, d, GLA_RANK), d ** -0.5),
        'gla_w_a2': nrm((N_GLA, 2, GLA_RANK, GLA_HK), GLA_RANK ** -0.5),
        'gla_b_a': nrm((N_GLA, 2, GLA_HK), 0.1),
        'gla_norm_g': 1.0 + nrm((N_GLA, GLA_DV), 0.02),
        'gla_w_out': nrm((N_GLA, GLA_HV, d), DN_BETA * GLA_HV ** -0.5),
        'sgu_w_in': nrm((N_SGU, d, SGU_FFN), d ** -0.5),
        'sgu_ln_g': 1.0 + nrm((N_SGU, SGU_HALF), 0.02),
        'sgu_ln_b': nrm((N_SGU, SGU_HALF), 0.02),
        'sgu_w_s': nrm((N_SGU, SGU_GROUPS, SGU_CHUNK, SGU_CHUNK), SGU_CHUNK ** -0.5),
        'sgu_b_s': 1.0 + nrm((N_SGU, SGU_GROUPS, SGU_CHUNK), 0.01),
        'sgu_w_out': nrm((N_SGU, SGU_HALF, d), DN_BETA * SGU_HALF ** -0.5),
        'moe_w_router': nrm((DEPTH, d, N_EXPERTS), d ** -0.5),
        'moe_b_router': nrm((DEPTH, N_EXPERTS), 0.01),
        'moe_w_gu': nrm((DEPTH, N_EXPERTS, d, 2 * EXPERT_FF), d ** -0.5),
        'moe_w_down': nrm((DEPTH, N_EXPERTS, EXPERT_FF, d), DN_BETA * EXPERT_FF ** -0.5),
        'moe_ws_gu': nrm((DEPTH, d, 2 * EXPERT_FF), d ** -0.5),
        'moe_ws_down': nrm((DEPTH, EXPERT_FF, d), DN_BETA * EXPERT_FF ** -0.5),
    }


def reference(x_prompt, x_sample, state_gla, c, c_ctx, w_ada, b_ada, ln_g, ln_b,
              gla_w_in, gla_w_a1, gla_w_a2, gla_b_a, gla_norm_g, gla_w_out,
              sgu_w_in, sgu_ln_g, sgu_ln_b, sgu_w_s, sgu_b_s, sgu_w_out,
              moe_w_router, moe_b_router, moe_w_gu, moe_w_down, moe_ws_gu, moe_ws_down):
    rows = x_sample.shape[1] // GRID_W
    xp = x_prompt
    xs = x_sample + grid_pos_embed(rows, x_sample.dtype)[None]
    silu_ctx = jax.nn.silu(c_ctx)[None]
    silu_c = jax.nn.silu(c)
    ctx_state0 = jnp.zeros((xp.shape[0], 2, GLA_HEADS, GLA_DK, GLA_DV), xp.dtype)
    new_states = []
    for l in range(DEPTH):
        mod_p = jnp.split((silu_ctx @ w_ada[l] + b_ada[l])[:, None, :], 6, axis=-1)
        mod_s = jnp.split((silu_c @ w_ada[l] + b_ada[l])[:, None, :], 6, axis=-1)
        hp = modulate(xp, mod_p[0], mod_p[1])
        hs = modulate(xs, mod_s[0], mod_s[1])
        j = l // N_MIXERS
        if l % N_MIXERS == 0:
            gla_p = (gla_w_in[j], gla_w_a1[j], gla_w_a2[j], gla_b_a[j], gla_norm_g[j], gla_w_out[j])
            out_p, st = gla_mixer(hp, ctx_state0, *gla_p)
            out_s, _ = gla_mixer(hs, state_gla[:, j], *gla_p)
            new_states.append(st)
        else:
            sgu_p = (sgu_w_in[j], sgu_ln_g[j], sgu_ln_b[j], sgu_w_s[j], sgu_b_s[j], sgu_w_out[j])
            out_p = sgu_mixer(hp, *sgu_p)
            out_s = sgu_mixer(hs, *sgu_p)
        xp = post_norm(xp, out_p, mod_p[2], ln_g[l, 0], ln_b[l, 0])
        xs = post_norm(xs, out_s, mod_s[2], ln_g[l, 0], ln_b[l, 0])
        moe_p = (moe_w_router[l], moe_b_router[l], moe_w_gu[l], moe_w_down[l], moe_ws_gu[l], moe_ws_down[l])
        xp = post_norm(xp, moe(modulate(xp, mod_p[3], mod_p[4]), *moe_p), mod_p[5], ln_g[l, 1], ln_b[l, 1])
        xs = post_norm(xs, moe(modulate(xs, mod_s[3], mod_s[4]), *moe_p), mod_s[5], ln_g[l, 1], ln_b[l, 1])
    new_state_gla = jnp.stack(new_states, axis=1)
    return (xp, xs, new_state_gla)
```

```python
import functools

import jax
import jax.numpy as jnp
from jax import lax
from jax.experimental import pallas as pl
from jax.experimental.pallas import tpu as pltpu

F32 = jnp.float32
BF16 = jnp.bfloat16

V7X_VMEM_LIMIT_BYTES = 56 * 1024 * 1024
LANES = 128
SUBLANES = 8

GRID_W = 64
GLA_HEADS = 4
GLA_CHUNK = 64
GLA_NORMALIZER = 16.0
SGU_GROUPS = 4
SGU_CHUNK = 128
TOP_K = 8
N_GROUPS = 8
TOPK_GROUPS = 4
ROUTE_SCALE = 2.5
LN_EPS = 1e-5
RMS_EPS = 1e-6
MOE_BLOCK = 256
COND_ROWS = 8


def _cparams(n_axes):
    return pltpu.CompilerParams(
        dimension_semantics=("arbitrary",) * n_axes,
        vmem_limit_bytes=V7X_VMEM_LIMIT_BYTES,
    )


def _silu(x):
    return x * jax.nn.sigmoid(x)


def _layer_norm(y, g, b):
    mu = jnp.mean(y, axis=-1, keepdims=True)
    d = y - mu
    var = jnp.mean(d * d, axis=-1, keepdims=True)
    return d * lax.rsqrt(var + LN_EPS) * g + b


def _row_tile(limit, *sizes):
    tm = limit
    while any(s % tm for s in sizes):
        tm //= 2
    return tm


def _cond_index(tm, tp, seq_s):
    def index(i):
        return jnp.where(i * tm < tp, 0, 1 + (i * tm - tp) // seq_s)

    return index


def _ada_kernel(c_ref, w_ref, b_ref, o_ref):
    s = _silu(c_ref[...]).astype(BF16)
    o_ref[...] = jnp.dot(s, w_ref[...].astype(BF16), preferred_element_type=F32) + b_ref[...]


def _ada_mods(cond, w_ada, b_ada):
    depth, d, n = w_ada.shape
    tn = _row_tile(1024, n)
    return pl.pallas_call(
        _ada_kernel,
        grid=(depth, n // tn),
        in_specs=[
            pl.BlockSpec((COND_ROWS, d), lambda l, j: (0, 0)),
            pl.BlockSpec((None, d, tn), lambda l, j: (l, 0, j)),
            pl.BlockSpec((None, 1, tn), lambda l, j: (l, 0, j)),
        ],
        out_specs=pl.BlockSpec((None, COND_ROWS, tn), lambda l, j: (l, 0, j)),
        out_shape=jax.ShapeDtypeStruct((depth, COND_ROWS, n), F32),
        compiler_params=_cparams(2),
        name="ada_mods",
    )(cond, w_ada, b_ada.reshape(depth, 1, n))


def _prep_kernel(xp_ref, xs_ref, pos_ref, sh_ref, sc_ref, x_ref, h_ref, *, n_p):
    x = jnp.where(pl.program_id(0) < n_p, xp_ref[...], xs_ref[...] + pos_ref[...])
    x_ref[...] = x
    h_ref[...] = (x * (1.0 + sc_ref[...]) + sh_ref[...]).astype(h_ref.dtype)


def _prep(xp, xs, pos, shift, scale, seq_s):
    tp, d = xp.shape
    ts = xs.shape[0]
    tm = _row_tile(512, tp, seq_s)
    n_p = tp // tm
    n_pos = seq_s // tm
    cid = _cond_index(tm, tp, seq_s)
    mod_spec = pl.BlockSpec((None, 1, d), lambda i: (cid(i), 0, 0))
    return pl.pallas_call(
        functools.partial(_prep_kernel, n_p=n_p),
        grid=((tp + ts) // tm,),
        in_specs=[
            pl.BlockSpec((tm, d), lambda i: (jnp.minimum(i, n_p - 1), 0)),
            pl.BlockSpec((tm, d), lambda i: (jnp.maximum(i - n_p, 0), 0)),
            pl.BlockSpec((tm, d), lambda i: (jnp.maximum(i - n_p, 0) % n_pos, 0)),
            mod_spec,
            mod_spec,
        ],
        out_specs=[pl.BlockSpec((tm, d), lambda i: (i, 0))] * 2,
        out_shape=[
            jax.ShapeDtypeStruct((tp + ts, d), F32),
            jax.ShapeDtypeStruct((tp + ts, d), BF16),
        ],
        compiler_params=_cparams(1),
        name="prep",
    )(xp, xs, pos, shift, scale)


def _mm_kernel(x_ref, w_ref, o_ref, wbf_ref, *, act):
    @pl.when(pl.program_id(1) == 0)
    def _():
        wbf_ref[...] = w_ref[...].astype(BF16)

    acc = jnp.dot(x_ref[...].astype(BF16), wbf_ref[...], preferred_element_type=F32)
    if act == "gelu":
        acc = 0.5 * acc * (1.0 + lax.erf(acc * 0.7071067811865476))
    o_ref[...] = acc.astype(o_ref.dtype)


def _matmul(x, w, out_dtype, act=None):
    m, k = x.shape
    n = w.shape[1]
    tm = _row_tile(512, m)
    tn_limit = 1024
    while k * tn_limit * 10 > V7X_VMEM_LIMIT_BYTES * 4 // 7 and tn_limit > LANES:
        tn_limit //= 2
    tn = _row_tile(tn_limit, n)
    return pl.pallas_call(
        functools.partial(_mm_kernel, act=act),
        grid=(n // tn, m // tm),
        in_specs=[
            pl.BlockSpec((tm, k), lambda j, i: (i, 0)),
            pl.BlockSpec((k, tn), lambda j, i: (0, j)),
        ],
        out_specs=pl.BlockSpec((tm, tn), lambda j, i: (i, j)),
        out_shape=jax.ShapeDtypeStruct((m, n), out_dtype),
        scratch_shapes=[pltpu.VMEM((k, tn), BF16)],
        compiler_params=_cparams(2),
        name="matmul_" + (act or "lin"),
    )(x, w)


def _post_norm_kernel(x_ref, o_ref, gate_ref, g_ref, b_ref, *rest, alpha, with_h):
    y = alpha * x_ref[...] + gate_ref[...] * o_ref[...].astype(F32)
    xn = _layer_norm(y, g_ref[...], b_ref[...])
    if with_h:
        sh_ref, sc_ref, xo_ref, ho_ref = rest
        ho_ref[...] = (xn * (1.0 + sc_ref[...]) + sh_ref[...]).astype(ho_ref.dtype)
    else:
        (xo_ref,) = rest
    xo_ref[...] = xn


def _post_norm(x, out, gate, ln_g, ln_b, shift, scale, h_dtype, alpha, tp, seq_s):
    t, d = x.shape
    tm = _row_tile(512, tp, seq_s)
    cid = _cond_index(tm, tp, seq_s)
    row = pl.BlockSpec((tm, d), lambda i: (i, 0))
    mod_spec = pl.BlockSpec((None, 1, d), lambda i: (cid(i), 0, 0))
    vec = pl.BlockSpec((1, d), lambda i: (0, 0))
    return pl.pallas_call(
        functools.partial(_post_norm_kernel, alpha=alpha, with_h=True),
        grid=(t // tm,),
        in_specs=[row, row, mod_spec, vec, vec, mod_spec, mod_spec],
        out_specs=[row, row],
        out_shape=[jax.ShapeDtypeStruct((t, d), F32), jax.ShapeDtypeStruct((t, d), h_dtype)],
        compiler_params=_cparams(1),
        name="post_norm",
    )(x, out, gate, ln_g.reshape(1, d), ln_b.reshape(1, d), shift, scale)


def _prefix_sum_mm(tri, x):
    hi = x.astype(BF16)
    r1 = x - hi.astype(F32)
    mid = r1.astype(BF16)
    lo = (r1 - mid.astype(F32)).astype(BF16)
    out = jnp.dot(tri, hi, preferred_element_type=F32)
    out = out + jnp.dot(tri, mid, preferred_element_type=F32)
    return out + jnp.dot(tri, lo, preferred_element_type=F32)


def _gla_kernel(*refs, seq_len, has_s0, out_state):
    q_ref, k_ref, v_ref, g_ref, low_ref, wa2_ref, ba_ref, ng_ref = refs[:8]
    rest = list(refs[8:])
    s0_ref = rest.pop(0) if has_s0 else None
    o_ref = rest.pop(0)
    st_ref = rest.pop(0) if out_state else None
    gk_scr, oacc, s_t = rest

    c = GLA_CHUNK
    n = seq_len // c
    dk = q_ref.shape[1]
    dv = v_ref.shape[1]
    scale = dk ** -0.5

    for z in range(2):
        pre = jnp.dot(low_ref[z].astype(BF16), wa2_ref[z].astype(BF16),
                      preferred_element_type=F32) + ba_ref[z]
        log_sig = jnp.minimum(pre, 0.0) - jnp.log1p(jnp.exp(-jnp.abs(pre)))
        gk_scr[z] = log_sig * (1.0 / GLA_NORMALIZER)

    row = lax.broadcasted_iota(jnp.int32, (c, c), 0)
    col = lax.broadcasted_iota(jnp.int32, (c, c), 1)
    masks = (col <= row, col >= row)
    tris = tuple(jnp.where(m, 1.0, 0.0).astype(BF16) for m in masks)
    nt_dims = (((1,), (1,)), ((), ()))
    tn_dims = (((0,), (0,)), ((), ()))

    def chunk_step(ci, z):
        rows = pl.ds(pl.multiple_of(ci * c, c), c)
        qc = q_ref[rows, :].astype(F32) * scale
        kc = k_ref[rows, :].astype(F32)
        vc = v_ref[rows, :].astype(BF16)
        gcum = _prefix_sum_mm(tris[z], gk_scr[z, rows, :])
        if z == 0:
            g_mid = gcum[c // 2 - 1:c // 2]
            g_last = gcum[c - 1:c]
        else:
            g_mid = gcum[c // 2:c // 2 + 1]
            g_last = gcum[0:1]
        q_in = (qc * jnp.exp(gcum - g_mid)).astype(BF16)
        k_in = (kc * jnp.exp(g_mid - gcum)).astype(BF16)
        a = lax.dot_general(q_in, k_in, nt_dims, preferred_element_type=F32)
        a = jnp.where(masks[z], a, 0.0).astype(BF16)
        o = jnp.dot(a, vc, preferred_element_type=F32)
        s_prev = s_t[...]
        q_dec = (qc * jnp.exp(gcum)).astype(BF16)
        o = o + lax.dot_general(q_dec, s_prev.astype(BF16), nt_dims, preferred_element_type=F32)
        k_dec = (kc * jnp.exp(g_last - gcum)).astype(BF16)
        u_t = lax.dot_general(vc, k_dec, tn_dims, preferred_element_type=F32)
        s_t[...] = s_prev * jnp.exp(g_last) + u_t
        if z == 0:
            oacc[rows, :] = o
        else:
            oacc[rows, :] += o

    for z in range(2):
        if has_s0:
            s_t[...] = s0_ref[z].T
        else:
            s_t[...] = jnp.zeros((dv, dk), F32)

        def body(i, carry, z=z):
            chunk_step(i if z == 0 else n - 1 - i, z)
            return carry

        lax.fori_loop(0, n, body, 0)
        if out_state:
            st_ref[z] = s_t[...].T

    o = oacc[...]
    on = o * lax.rsqrt(jnp.mean(o * o, axis=-1, keepdims=True) + RMS_EPS) * ng_ref[...]
    o_ref[...] = (on * _silu(g_ref[...].astype(F32))).astype(o_ref.dtype)


def _gla_scan(proj, low, w_a2, b_a, norm_g, state, layer_j, *, batch, seq_len, row0, out_state):
    h = GLA_HEADS
    hk = w_a2.shape[2]
    dk = hk // h
    hv = (proj.shape[1] - 2 * hk) // 2
    dv = hv // h
    rank = w_a2.shape[1]
    off = row0 // seq_len
    has_s0 = state is not None
    in_specs = [
        pl.BlockSpec((seq_len, dk), lambda b, i: (b + off, i)),
        pl.BlockSpec((seq_len, dk), lambda b, i: (b + off, h + i)),
        pl.BlockSpec((seq_len, dv), lambda b, i: (b + off, 2 * hk // dv + i)),
        pl.BlockSpec((seq_len, dv), lambda b, i: (b + off, (2 * hk + hv) // dv + i)),
        pl.BlockSpec((2, seq_len, rank), lambda b, i: (0, b + off, 0)),
        pl.BlockSpec((2, rank, dk), lambda b, i: (0, 0, i)),
        pl.BlockSpec((2, 1, dk), lambda b, i: (0, 0, i)),
        pl.BlockSpec((1, dv), lambda b, i: (0, 0)),
    ]
    args = [proj, proj, proj, proj, low, w_a2, b_a.reshape(2, 1, hk), norm_g.reshape(1, dv)]
    if has_s0:
        in_specs.append(pl.BlockSpec((None, None, 2, None, dk, dv),
                                     lambda b, i: (b, layer_j, 0, i, 0, 0)))
        args.append(state)
    out_specs = [pl.BlockSpec((seq_len, dv), lambda b, i: (b, i))]
    out_shape = [jax.ShapeDtypeStruct((batch * seq_len, hv), BF16)]
    if out_state:
        out_specs.append(pl.BlockSpec((None, 2, None, dk, dv), lambda b, i: (b, 0, i, 0, 0)))
        out_shape.append(jax.ShapeDtypeStruct((batch, 2, h, dk, dv), F32))
    return pl.pallas_call(
        functools.partial(_gla_kernel, seq_len=seq_len, has_s0=has_s0, out_state=out_state),
        grid=(batch, h),
        in_specs=in_specs,
        out_specs=out_specs,
        out_shape=out_shape,
        scratch_shapes=[
            pltpu.VMEM((2, seq_len, dk), F32),
            pltpu.VMEM((seq_len, dv), F32),
            pltpu.VMEM((dv, dk), F32),
        ],
        compiler_params=_cparams(2),
        name="gla_scan",
    )(*args)


def _sgu_kernel(u_ref, v_ref, lg_ref, lb_ref, ws_ref, bs_ref, o_ref):
    v = v_ref[...].astype(F32)
    vn = _layer_norm(v, lg_ref[...], lb_ref[...]).astype(BF16)
    rows, half = vn.shape
    cg = half // SGU_GROUPS
    for r in range(rows // SGU_CHUNK):
        rs = slice(r * SGU_CHUNK, (r + 1) * SGU_CHUNK)
        for g in range(SGU_GROUPS):
            cs = slice(g * cg, (g + 1) * cg)
            s = jnp.dot(ws_ref[g].astype(BF16), vn[rs, cs], preferred_element_type=F32) + bs_ref[g]
            o_ref[rs, cs] = (u_ref[rs, cs].astype(F32) * s).astype(o_ref.dtype)


def _sgu_gate(z, ln_g, ln_b, w_s, b_s):
    t, ffn = z.shape
    half = ffn // 2
    rows = SGU_CHUNK
    return pl.pallas_call(
        _sgu_kernel,
        grid=(t // rows,),
        in_specs=[
            pl.BlockSpec((rows, half), lambda i: (i, 0)),
            pl.BlockSpec((rows, half), lambda i: (i, 1)),
            pl.BlockSpec((1, half), lambda i: (0, 0)),
            pl.BlockSpec((1, half), lambda i: (0, 0)),
            pl.BlockSpec((SGU_GROUPS, SGU_CHUNK, SGU_CHUNK), lambda i: (0, 0, 0)),
            pl.BlockSpec((SGU_GROUPS, SGU_CHUNK, 1), lambda i: (0, 0, 0)),
        ],
        out_specs=pl.BlockSpec((rows, half), lambda i: (i, 0)),
        out_shape=jax.ShapeDtypeStruct((t, half), BF16),
        compiler_params=_cparams(1),
        name="sgu_gate",
    )(z, z, ln_g.reshape(1, half), ln_b.reshape(1, half), w_s, b_s.reshape(SGU_GROUPS, SGU_CHUNK, 1))


def _router_kernel(x_ref, wrt_ref, br_ref, sel_ref, wd_ref, pos_ref, cnt_ref, carry):
    i = pl.program_id(0)

    @pl.when(i == 0)
    def _():
        carry[...] = jnp.zeros_like(carry)

    x = x_ref[...].astype(BF16)
    logits = lax.dot_general(wrt_ref[...].astype(BF16), x, (((1,), (1,)), ((), ())),
                             preferred_element_type=F32)
    n_exp, tm = logits.shape
    ng = N_GROUPS
    per = n_exp // ng
    scores = jax.nn.sigmoid(logits)
    selv = scores + br_ref[...]
    a = [selv[j * ng:(j + 1) * ng] for j in range(per)]
    sc = [scores[j * ng:(j + 1) * ng] for j in range(per)]

    t1 = a[0]
    t2 = jnp.full_like(t1, -jnp.inf)
    for j in range(1, per):
        t2 = jnp.maximum(t2, jnp.minimum(t1, a[j]))
        t1 = jnp.maximum(t1, a[j])
    gscore = t1 + t2

    gidx = lax.broadcasted_iota(jnp.int32, (ng, tm), 0)
    grank = jnp.zeros((ng, tm), jnp.int32)
    for gp in range(ng):
        other = gscore[gp:gp + 1, :]
        beats = (other > gscore) | ((other == gscore) & (gp < gidx))
        grank = grank + beats.astype(jnp.int32)
    gmask = grank < TOPK_GROUPS

    val = [jnp.where(gmask, a[j], -jnp.inf) for j in range(per)]
    ranks = [jnp.zeros((ng, tm), jnp.int32) for _ in range(per)]
    for gp in range(ng):
        lower_group = gp < gidx
        lower_or_same_group = gp <= gidx
        for jp in range(per):
            other = val[jp][gp:gp + 1, :]
            for j in range(per):
                tie = lower_or_same_group if jp < j else lower_group
                beats = (other > val[j]) | ((other == val[j]) & tie)
                ranks[j] = ranks[j] + beats.astype(jnp.int32)
    selm = [(ranks[j] < TOP_K) & gmask for j in range(per)]

    ssum = jnp.zeros((ng, tm), F32)
    for j in range(per):
        ssum = ssum + jnp.where(selm[j], sc[j], 0.0)
    ssum = jnp.sum(ssum, axis=0, keepdims=True)
    wd = jnp.concatenate(
        [jnp.where(selm[j], sc[j] / ssum * ROUTE_SCALE, 0.0) for j in range(per)], axis=0)
    selmat = jnp.concatenate([jnp.where(selm[j], 1.0, 0.0) for j in range(per)], axis=0)

    s_i = lax.broadcasted_iota(jnp.int32, (tm, tm), 0)
    t_i = lax.broadcasted_iota(jnp.int32, (tm, tm), 1)
    upper = jnp.where(s_i < t_i, 1.0, 0.0).astype(BF16)
    base = carry[...][:, 0:1]
    pos = jnp.dot(selmat.astype(BF16), upper, preferred_element_type=F32) + base
    total = base + jnp.sum(selmat, axis=1, keepdims=True)
    carry[...] = jnp.broadcast_to(total, carry.shape)

    sel_ref[...] = selmat
    wd_ref[...] = wd
    pos_ref[...] = pos
    cnt_ref[...] = jnp.broadcast_to(total, cnt_ref.shape)


def _router(hm, w_rt, b_r_col):
    t, d = hm.shape
    n_exp = w_rt.shape[0]
    tm = _row_tile(512, t)
    col = pl.BlockSpec((n_exp, tm), lambda i: (0, i))
    return pl.pallas_call(
        _router_kernel,
        grid=(t // tm,),
        in_specs=[
            pl.BlockSpec((tm, d), lambda i: (i, 0)),
            pl.BlockSpec((n_exp, d), lambda i: (0, 0)),
            pl.BlockSpec((n_exp, 1), lambda i: (0, 0)),
        ],
        out_specs=[col, col, col, pl.BlockSpec((n_exp, LANES), lambda i: (0, 0))],
        out_shape=[
            jax.ShapeDtypeStruct((n_exp, t), F32),
            jax.ShapeDtypeStruct((n_exp, t), F32),
            jax.ShapeDtypeStruct((n_exp, t), F32),
            jax.ShapeDtypeStruct((n_exp, LANES), F32),
        ],
        scratch_shapes=[pltpu.VMEM((n_exp, LANES), F32)],
        compiler_params=_cparams(1),
        name="router",
    )(hm, w_rt, b_r_col)


def _dispatch_kernel(dest_ref, fill_ref, x_ref, o_hbm, zrow, sem, *, tm, fill_steps, fill_per_step):
    def body(t, carry):
        for kk in range(TOP_K):
            d = dest_ref[0, t * TOP_K + kk]
            pltpu.make_async_copy(x_ref.at[pl.ds(t, 1), :], o_hbm.at[pl.ds(d, 1), :], sem).start()
        return carry

    lax.fori_loop(0, tm, body, 0)

    @pl.when(pl.program_id(0) < fill_steps)
    def _():
        zrow[...] = jnp.zeros_like(zrow)

        def fill(f, carry):
            d = fill_ref[0, f]
            pltpu.make_async_copy(zrow.at[pl.ds(0, 1), :], o_hbm.at[pl.ds(d, 1), :], sem).start()
            return carry

        lax.fori_loop(0, fill_per_step, fill, 0)
        for _ in range(fill_per_step // tm):
            pltpu.make_async_copy(x_ref, o_hbm.at[pl.ds(0, tm), :], sem).wait()

    for _ in range(TOP_K):
        pltpu.make_async_copy(x_ref, o_hbm.at[pl.ds(0, tm), :], sem).wait()


def _dispatch(hm, dest_tiles, fill, cap, tm):
    t, d = hm.shape
    steps = t // tm
    fill_steps = 1
    while fill_steps * 2 <= steps:
        fill_steps *= 2
    n_fill = fill.shape[0]
    fill_per_step = n_fill // fill_steps
    assert fill_per_step * fill_steps == n_fill and fill_per_step % tm == 0
    return pl.pallas_call(
        functools.partial(_dispatch_kernel, tm=tm, fill_steps=fill_steps,
                          fill_per_step=fill_per_step),
        grid=(steps,),
        in_specs=[
            pl.BlockSpec((None, 1, tm * TOP_K), lambda i: (i, 0, 0), memory_space=pltpu.SMEM),
            pl.BlockSpec((None, 1, fill_per_step),
                         lambda i: (jnp.minimum(i, fill_steps - 1), 0, 0),
                         memory_space=pltpu.SMEM),
            pl.BlockSpec((tm, d), lambda i: (i, 0)),
        ],
        out_specs=pl.BlockSpec(memory_space=pl.ANY),
        out_shape=jax.ShapeDtypeStruct((cap, d), hm.dtype),
        scratch_shapes=[pltpu.VMEM((SUBLANES, d), hm.dtype), pltpu.SemaphoreType.DMA],
        compiler_params=_cparams(1),
        name="moe_dispatch",
    )(dest_tiles, fill.reshape(fill_steps, 1, fill_per_step), hm)


def _swiglu(x, wgu, wd, ff):
    h = jnp.dot(x, wgu, preferred_element_type=F32)
    act = (_silu(h[:, :ff]) * h[:, ff:]).astype(BF16)
    return jnp.dot(act, wd, preferred_element_type=F32)


def _expert_kernel(be_ref, nu_ref, x_ref, wgu_ref, wd_ref, o_ref, wgu_bf, wd_bf, *, ff):
    b = pl.program_id(0)

    @pl.when(b < nu_ref[0])
    def _():
        @pl.when((b == 0) | (be_ref[b] != be_ref[jnp.maximum(b - 1, 0)]))
        def _():
            wgu_bf[...] = wgu_ref[...].astype(BF16)
            wd_bf[...] = wd_ref[...].astype(BF16)

        o_ref[...] = _swiglu(x_ref[...].astype(BF16), wgu_bf[...], wd_bf[...], ff)

    @pl.when(b >= nu_ref[0])
    def _():
        o_ref[...] = jnp.zeros_like(o_ref)


def _experts(xs, block_e, n_used, w_gu, w_down):
    cap, d = xs.shape
    ff = w_down.shape[1]
    n_blocks = cap // MOE_BLOCK

    def blk(b, be, nu):
        return jnp.minimum(b, nu[0] - 1)

    return pl.pallas_call(
        functools.partial(_expert_kernel, ff=ff),
        grid_spec=pltpu.PrefetchScalarGridSpec(
            num_scalar_prefetch=2,
            grid=(n_blocks,),
            in_specs=[
                pl.BlockSpec((MOE_BLOCK, d), lambda b, be, nu: (blk(b, be, nu), 0)),
                pl.BlockSpec((None, d, 2 * ff), lambda b, be, nu: (be[blk(b, be, nu)], 0, 0)),
                pl.BlockSpec((None, ff, d), lambda b, be, nu: (be[blk(b, be, nu)], 0, 0)),
            ],
            out_specs=pl.BlockSpec((MOE_BLOCK, d), lambda b, be, nu: (b, 0)),
            scratch_shapes=[pltpu.VMEM((d, 2 * ff), BF16), pltpu.VMEM((ff, d), BF16)],
        ),
        out_shape=jax.ShapeDtypeStruct((cap, d), F32),
        compiler_params=_cparams(1),
        name="moe_experts",
    )(block_e, n_used, xs, w_gu, w_down)


def _shared_kernel(x_ref, wgu_ref, wd_ref, o_ref, wgu_bf, wd_bf, *, ff):
    @pl.when(pl.program_id(0) == 0)
    def _():
        wgu_bf[...] = wgu_ref[...].astype(BF16)
        wd_bf[...] = wd_ref[...].astype(BF16)

    o_ref[...] = _swiglu(x_ref[...].astype(BF16), wgu_bf[...], wd_bf[...], ff)


def _shared_expert(hm, ws_gu, ws_down):
    t, d = hm.shape
    ff = ws_down.shape[0]
    tm = _row_tile(512, t)
    return pl.pallas_call(
        functools.partial(_shared_kernel, ff=ff),
        grid=(t // tm,),
        in_specs=[
            pl.BlockSpec((tm, d), lambda i: (i, 0)),
            pl.BlockSpec((d, 2 * ff), lambda i: (0, 0)),
            pl.BlockSpec((ff, d), lambda i: (0, 0)),
        ],
        out_specs=pl.BlockSpec((tm, d), lambda i: (i, 0)),
        out_shape=jax.ShapeDtypeStruct((t, d), F32),
        scratch_shapes=[pltpu.VMEM((d, 2 * ff), BF16), pltpu.VMEM((ff, d), BF16)],
        compiler_params=_cparams(1),
        name="moe_shared",
    )(hm, ws_gu, ws_down)


def _combine_kernel(dcur_ref, dnext_ref, w8_ref, ysh_ref, x_ref, gate_ref, g_ref, b_ref, *rest,
                    tm, alpha, n_steps, with_h):
    if with_h:
        sh_ref, sc_ref, yb_hbm, xo_ref, ho_ref, buf, sem = rest
    else:
        yb_hbm, xo_ref, buf, sem = rest
    i = pl.program_id(0)
    slot = i % 2

    def issue(dref, s):
        def body(t, carry):
            for kk in range(TOP_K):
                d = dref[0, t * TOP_K + kk]
                pltpu.make_async_copy(yb_hbm.at[pl.ds(d, 1), :],
                                      buf.at[s, kk, pl.ds(t, 1), :], sem.at[s]).start()
            return carry

        lax.fori_loop(0, tm, body, 0)

    @pl.when(i == 0)
    def _():
        issue(dcur_ref, 0)

    @pl.when(i + 1 < n_steps)
    def _():
        issue(dnext_ref, 1 - slot)

    for kk in range(TOP_K):
        pltpu.make_async_copy(yb_hbm.at[pl.ds(0, tm), :], buf.at[slot, kk], sem.at[slot]).wait()

    w8 = w8_ref[...]
    acc = ysh_ref[...]
    for kk in range(TOP_K):
        acc = acc + w8[:, kk:kk + 1] * buf[slot, kk]
    y = alpha * x_ref[...] + gate_ref[...] * acc
    xn = _layer_norm(y, g_ref[...], b_ref[...])
    xo_ref[...] = xn
    if with_h:
        ho_ref[...] = (xn * (1.0 + sc_ref[...]) + sh_ref[...]).astype(ho_ref.dtype)


def _combine(yb, dest_tiles, w8, y_sh, x, gate, ln_g, ln_b, shift, scale, alpha, tp, seq_s, tm):
    t, d = x.shape
    n_steps = t // tm
    with_h = shift is not None
    cid = _cond_index(tm, tp, seq_s)
    row = pl.BlockSpec((tm, d), lambda i: (i, 0))
    mod_spec = pl.BlockSpec((None, 1, d), lambda i: (cid(i), 0, 0))
    vec = pl.BlockSpec((1, d), lambda i: (0, 0))
    in_specs = [
        pl.BlockSpec((None, 1, tm * TOP_K), lambda i: (i, 0, 0), memory_space=pltpu.SMEM),
        pl.BlockSpec((None, 1, tm * TOP_K), lambda i: (jnp.minimum(i + 1, n_steps - 1), 0, 0),
                     memory_space=pltpu.SMEM),
        pl.BlockSpec((tm, TOP_K), lambda i: (i, 0)),
        row, row, mod_spec, vec, vec,
    ]
    args = [dest_tiles, dest_tiles, w8, y_sh, x, gate, ln_g.reshape(1, d), ln_b.reshape(1, d)]
    out_specs = [row]
    out_shape = [jax.ShapeDtypeStruct((t, d), F32)]
    if with_h:
        in_specs += [mod_spec, mod_spec]
        args += [shift, scale]
        out_specs.append(row)
        out_shape.append(jax.ShapeDtypeStruct((t, d), BF16))
    in_specs.append(pl.BlockSpec(memory_space=pl.ANY))
    args.append(yb)
    return pl.pallas_call(
        functools.partial(_combine_kernel, tm=tm, alpha=alpha, n_steps=n_steps, with_h=with_h),
        grid=(n_steps,),
        in_specs=in_specs,
        out_specs=out_specs,
        out_shape=out_shape,
        scratch_shapes=[pltpu.VMEM((2, TOP_K, tm, d), F32), pltpu.SemaphoreType.DMA((2,))],
        compiler_params=_cparams(1),
        name="moe_combine",
    )(*args)


def _expert_row_order(n_exp):
    per = n_exp // N_GROUPS
    r = jnp.arange(n_exp)
    return (r % N_GROUPS) * per + r // N_GROUPS


def _moe_layer(x, hm, w_r, b_r, w_gu, w_down, ws_gu, ws_down, gate, ln_g, ln_b, shift, scale,
               alpha, tp, seq_s):
    t, d = hm.shape
    n_exp = w_r.shape[1]
    order = _expert_row_order(n_exp)
    sel, wd, pos, cnt = _router(hm, w_r.T[order], b_r[order].reshape(n_exp, 1))

    n_blocks = -(-(t * TOP_K) // MOE_BLOCK) + n_exp
    cap = n_blocks * MOE_BLOCK
    counts = cnt[:, 0].astype(jnp.int32)
    padded = (counts + MOE_BLOCK - 1) // MOE_BLOCK * MOE_BLOCK
    pend = jnp.cumsum(padded)
    pstart = pend - padded
    n_used = (pend[-1:] // MOE_BLOCK).astype(jnp.int32)
    block_row = jnp.minimum(
        jnp.searchsorted(pend, jnp.arange(n_blocks) * MOE_BLOCK, side="right"), n_exp - 1)
    block_e = order[block_row].astype(jnp.int32)
    chosen = sel > 0.5
    dest = pstart[:, None] + pos.astype(jnp.int32)
    slot = jnp.cumsum(chosen.astype(jnp.int32), axis=0) - chosen.astype(jnp.int32)
    pick = chosen[None] & (slot[None] == jnp.arange(TOP_K)[:, None, None])
    dest8 = jnp.sum(jnp.where(pick, dest[None], 0), axis=1).T
    w8 = jnp.sum(jnp.where(pick, wd[None], 0.0), axis=1).T

    padlen = padded - counts
    cpad_end = jnp.cumsum(padlen)
    f = jnp.arange(cap - t * TOP_K)
    fr = jnp.minimum(jnp.searchsorted(cpad_end, f, side="right"), n_exp - 1)
    fill = jnp.where(f < cpad_end[-1],
                     pstart[fr] + counts[fr] + f - (cpad_end - padlen)[fr],
                     pend[-1] + f - cpad_end[-1]).astype(jnp.int32)

    tm_d = _row_tile(256, tp, seq_s)
    xs = _dispatch(hm, dest8.reshape(t // tm_d, 1, tm_d * TOP_K), fill, cap, tm_d)
    yb = _experts(xs, block_e, n_used, w_gu, w_down)
    y_sh = _shared_expert(hm, ws_gu, ws_down)
    tm_c = _row_tile(128, tp, seq_s)
    return _combine(yb, dest8.reshape(t // tm_c, 1, tm_c * TOP_K), w8, y_sh, x, gate, ln_g, ln_b,
                    shift, scale, alpha, tp, seq_s, tm_c)


def _gla_layer(h, state_gla, layer_j, w_in, w_a1, w_a2, b_a, norm_g, w_out, dims):
    bp, sp, bs, ss = dims
    tp = bp * sp
    d = h.shape[1]
    rank = w_a1.shape[2]
    proj = _matmul(h, w_in, BF16)
    low = _matmul(h, jnp.transpose(w_a1, (1, 0, 2)).reshape(d, 2 * rank), F32)
    low = jnp.transpose(low.reshape(-1, 2, rank), (1, 0, 2))
    og_p, st = _gla_scan(proj, low, w_a2, b_a, norm_g, None, layer_j,
                         batch=bp, seq_len=sp, row0=0, out_state=True)
    (og_s,) = _gla_scan(proj, low, w_a2, b_a, norm_g, state_gla, layer_j,
                        batch=bs, seq_len=ss, row0=tp, out_state=False)
    out = _matmul(jnp.concatenate([og_p, og_s], axis=0), w_out, BF16)
    return out, st


def _sgu_layer(h, w_in, ln_g, ln_b, w_s, b_s, w_out):
    z = _matmul(h, w_in, BF16, act="gelu")
    return _matmul(_sgu_gate(z, ln_g, ln_b, w_s, b_s), w_out, BF16)


def _grid_pos_embed(rows, d):
    t = jnp.arange(rows * GRID_W)
    r = (t // GRID_W).astype(F32)
    col = (t % GRID_W).astype(F32)
    quarter = d // 4
    omega = 1.0 / (10000.0 ** (jnp.arange(quarter, dtype=F32) / quarter))

    def emb(p):
        a = p[:, None] * omega[None, :]
        return jnp.concatenate([jnp.sin(a), jnp.cos(a)], axis=-1)

    return jnp.concatenate([emb(r), emb(col)], axis=-1)


def kernel(x_prompt, x_sample, state_gla, c, c_ctx, w_ada, b_ada, ln_g, ln_b, gla_w_in, gla_w_a1, gla_w_a2, gla_b_a, gla_norm_g, gla_w_out, sgu_w_in, sgu_ln_g, sgu_ln_b, sgu_w_s, sgu_b_s, sgu_w_out, moe_w_router, moe_b_router, moe_w_gu, moe_w_down, moe_ws_gu, moe_ws_down):
    bp, sp, d = x_prompt.shape
    bs, ss, _ = x_sample.shape
    assert 1 + bs <= COND_ROWS
    tp, ts = bp * sp, bs * ss
    depth = w_ada.shape[0]
    alpha = (2 * depth) ** 0.25
    dims = (bp, sp, bs, ss)

    cond = jnp.zeros((COND_ROWS, d), F32).at[0].set(c_ctx).at[1:1 + bs].set(c)
    mods = _ada_mods(cond, w_ada, b_ada).reshape(depth, COND_ROWS, 6, 1, d)

    def mod(l, i):
        return mods[l, :, i]

    pos = _grid_pos_embed(ss // GRID_W, d).astype(x_sample.dtype)
    x, h = _prep(x_prompt.reshape(tp, d), x_sample.reshape(ts, d), pos, mod(0, 0), mod(0, 1), ss)

    states = []
    for l in range(depth):
        j = l // 2
        if l % 2 == 0:
            out, st = _gla_layer(h, state_gla, j, gla_w_in[j], gla_w_a1[j], gla_w_a2[j],
                                 gla_b_a[j], gla_norm_g[j], gla_w_out[j], dims)
            states.append(st)
        else:
            out = _sgu_layer(h, sgu_w_in[j], sgu_ln_g[j], sgu_ln_b[j], sgu_w_s[j], sgu_b_s[j],
                             sgu_w_out[j])
        x, hm = _post_norm(x, out, mod(l, 2), ln_g[l, 0], ln_b[l, 0], mod(l, 3), mod(l, 4), F32,
                           alpha, tp, ss)
        last = l == depth - 1
        res = _moe_layer(x, hm, moe_w_router[l], moe_b_router[l], moe_w_gu[l], moe_w_down[l],
                         moe_ws_gu[l], moe_ws_down[l], mod(l, 5), ln_g[l, 1], ln_b[l, 1],
                         None if last else mod(l + 1, 0), None if last else mod(l + 1, 1),
                         alpha, tp, ss)
        if last:
            (x,) = res
        else:
            x, h = res

    new_state = jnp.stack(states, axis=1)
    return x[:tp].reshape(bp, sp, d), x[tp:].reshape(bs, ss, d), new_state
```

```python
import functools

import jax
import jax.numpy as jnp
from jax import lax
from jax.experimental import pallas as pl
from jax.experimental.pallas import tpu as pltpu

F32 = jnp.float32
BF16 = jnp.bfloat16

V7X_VMEM_LIMIT_BYTES = 56 * 1024 * 1024
LANES = 128
SUBLANES = 8

GRID_W = 64
GLA_HEADS = 4
GLA_CHUNK = 64
GLA_NORMALIZER = 16.0
SGU_GROUPS = 4
SGU_CHUNK = 128
TOP_K = 8
N_GROUPS = 8
TOPK_GROUPS = 4
ROUTE_SCALE = 2.5
LN_EPS = 1e-5
RMS_EPS = 1e-6
MOE_BLOCK = 256
COND_ROWS = 8


def _cparams(n_axes):
    return pltpu.CompilerParams(
        dimension_semantics=("arbitrary",) * n_axes,
        vmem_limit_bytes=V7X_VMEM_LIMIT_BYTES,
    )


def _silu(x):
    return x * jax.nn.sigmoid(x)


def _layer_norm(y, g, b):
    mu = jnp.mean(y, axis=-1, keepdims=True)
    d = y - mu
    var = jnp.mean(d * d, axis=-1, keepdims=True)
    return d * lax.rsqrt(var + LN_EPS) * g + b


_HI16 = 0xFFFF0000


def _pack_pairs(x):
    half = x.shape[-1] // 2
    lo = lax.bitcast_convert_type(x[:, :half].astype(BF16).astype(F32), jnp.uint32)
    hi = lax.bitcast_convert_type(x[:, half:].astype(BF16).astype(F32), jnp.uint32)
    return lax.shift_right_logical(lo, jnp.uint32(16)) | (hi & jnp.uint32(_HI16))


def _unpack_lo(w):
    return lax.bitcast_convert_type(lax.shift_left(w, jnp.uint32(16)), F32)


def _unpack_hi(w):
    return lax.bitcast_convert_type(w & jnp.uint32(_HI16), F32)


def _unpack_pairs_bf16(w):
    return jnp.concatenate([_unpack_lo(w).astype(BF16), _unpack_hi(w).astype(BF16)], axis=-1)


def _row_tile(limit, *sizes):
    tm = limit
    while any(s % tm for s in sizes):
        tm //= 2
    return tm


def _cond_index(tm, tp, seq_s):
    def index(i):
        return jnp.where(i * tm < tp, 0, 1 + (i * tm - tp) // seq_s)

    return index


def _ada_kernel(c_ref, w_ref, b_ref, o_ref):
    s = _silu(c_ref[...]).astype(BF16)
    o_ref[...] = jnp.dot(s, w_ref[...].astype(BF16), preferred_element_type=F32) + b_ref[...]


def _ada_mods(cond, w_ada, b_ada):
    depth, d, n = w_ada.shape
    tn = _row_tile(1024, n)
    return pl.pallas_call(
        _ada_kernel,
        grid=(depth, n // tn),
        in_specs=[
            pl.BlockSpec((COND_ROWS, d), lambda l, j: (0, 0)),
            pl.BlockSpec((None, d, tn), lambda l, j: (l, 0, j)),
            pl.BlockSpec((None, 1, tn), lambda l, j: (l, 0, j)),
        ],
        out_specs=pl.BlockSpec((None, COND_ROWS, tn), lambda l, j: (l, 0, j)),
        out_shape=jax.ShapeDtypeStruct((depth, COND_ROWS, n), F32),
        compiler_params=_cparams(2),
        name="ada_mods",
    )(cond, w_ada, b_ada.reshape(depth, 1, n))


def _prep_kernel(xp_ref, xs_ref, pos_ref, sh_ref, sc_ref, x_ref, h_ref, *, n_p):
    x = jnp.where(pl.program_id(0) < n_p, xp_ref[...], xs_ref[...] + pos_ref[...])
    x_ref[...] = x
    h_ref[...] = (x * (1.0 + sc_ref[...]) + sh_ref[...]).astype(h_ref.dtype)


def _prep(xp, xs, pos, shift, scale, seq_s):
    tp, d = xp.shape
    ts = xs.shape[0]
    tm = _row_tile(512, tp, seq_s)
    n_p = tp // tm
    n_pos = seq_s // tm
    cid = _cond_index(tm, tp, seq_s)
    mod_spec = pl.BlockSpec((None, 1, d), lambda i: (cid(i), 0, 0))
    return pl.pallas_call(
        functools.partial(_prep_kernel, n_p=n_p),
        grid=((tp + ts) // tm,),
        in_specs=[
            pl.BlockSpec((tm, d), lambda i: (jnp.minimum(i, n_p - 1), 0)),
            pl.BlockSpec((tm, d), lambda i: (jnp.maximum(i - n_p, 0), 0)),
            pl.BlockSpec((tm, d), lambda i: (jnp.maximum(i - n_p, 0) % n_pos, 0)),
            mod_spec,
            mod_spec,
        ],
        out_specs=[pl.BlockSpec((tm, d), lambda i: (i, 0))] * 2,
        out_shape=[
            jax.ShapeDtypeStruct((tp + ts, d), F32),
            jax.ShapeDtypeStruct((tp + ts, d), BF16),
        ],
        compiler_params=_cparams(1),
        name="prep",
    )(xp, xs, pos, shift, scale)


def _mm_kernel(x_ref, w_ref, o_ref, wbf_ref, *, act):
    @pl.when(pl.program_id(1) == 0)
    def _():
        wbf_ref[...] = w_ref[...].astype(BF16)

    acc = jnp.dot(x_ref[...].astype(BF16), wbf_ref[...], preferred_element_type=F32)
    if act == "gelu":
        acc = 0.5 * acc * (1.0 + lax.erf(acc * 0.7071067811865476))
    o_ref[...] = acc.astype(o_ref.dtype)


def _matmul(x, w, layer, out_dtype, act=None):
    m, k = x.shape
    n = w.shape[2]
    tm = _row_tile(512, m)
    tn_limit = 1024
    while k * tn_limit * 10 > V7X_VMEM_LIMIT_BYTES * 4 // 7 and tn_limit > LANES:
        tn_limit //= 2
    tn = _row_tile(tn_limit, n)
    return pl.pallas_call(
        functools.partial(_mm_kernel, act=act),
        grid=(n // tn, m // tm),
        in_specs=[
            pl.BlockSpec((tm, k), lambda j, i: (i, 0)),
            pl.BlockSpec((None, k, tn), lambda j, i: (layer, 0, j)),
        ],
        out_specs=pl.BlockSpec((tm, tn), lambda j, i: (i, j)),
        out_shape=jax.ShapeDtypeStruct((m, n), out_dtype),
        scratch_shapes=[pltpu.VMEM((k, tn), BF16)],
        compiler_params=_cparams(2),
        name="matmul_" + (act or "lin"),
    )(x, w)


def _post_norm_kernel(x_ref, o_ref, gate_ref, g_ref, b_ref, *rest, alpha, with_h):
    y = alpha * x_ref[...] + gate_ref[...] * o_ref[...].astype(F32)
    xn = _layer_norm(y, g_ref[...], b_ref[...])
    if with_h:
        sh_ref, sc_ref, xo_ref, ho_ref = rest
        _store_h(ho_ref, xn * (1.0 + sc_ref[...]) + sh_ref[...])
    else:
        (xo_ref,) = rest
    xo_ref[...] = xn


def _store_h(ho_ref, h):
    ho_ref[...] = _pack_pairs(h) if ho_ref.dtype == jnp.uint32 else h.astype(ho_ref.dtype)


def _post_norm(x, out, gate, ln_g, ln_b, shift, scale, alpha, tp, seq_s):
    t, d = x.shape
    tm = _row_tile(512, tp, seq_s)
    cid = _cond_index(tm, tp, seq_s)
    row = pl.BlockSpec((tm, d), lambda i: (i, 0))
    mod_spec = pl.BlockSpec((None, 1, d), lambda i: (cid(i), 0, 0))
    vec = pl.BlockSpec((1, d), lambda i: (0, 0))
    return pl.pallas_call(
        functools.partial(_post_norm_kernel, alpha=alpha, with_h=True),
        grid=(t // tm,),
        in_specs=[row, row, mod_spec, vec, vec, mod_spec, mod_spec],
        out_specs=[row, pl.BlockSpec((tm, d // 2), lambda i: (i, 0))],
        out_shape=[jax.ShapeDtypeStruct((t, d), F32),
                   jax.ShapeDtypeStruct((t, d // 2), jnp.uint32)],
        compiler_params=_cparams(1),
        name="post_norm",
    )(x, out, gate, ln_g.reshape(1, d), ln_b.reshape(1, d), shift, scale)


def _prefix_sum_mm(tri, x):
    hi = x.astype(BF16)
    r1 = x - hi.astype(F32)
    mid = r1.astype(BF16)
    lo = (r1 - mid.astype(F32)).astype(BF16)
    out = jnp.dot(tri, hi, preferred_element_type=F32)
    out = out + jnp.dot(tri, mid, preferred_element_type=F32)
    return out + jnp.dot(tri, lo, preferred_element_type=F32)


def _gla_kernel(*refs, seq_len, hp, has_s0, out_state):
    q_ref, k_ref, v_ref, g_ref, low_ref, wa2_ref, ba_ref, ng_ref = refs[:8]
    rest = list(refs[8:])
    s0_ref = rest.pop(0) if has_s0 else None
    o_ref = rest.pop(0)
    st_ref = rest.pop(0) if out_state else None
    gk_scr, oacc, s_t = rest

    c = GLA_CHUNK
    n = seq_len // c
    dk = q_ref.shape[1] // hp
    dv = v_ref.shape[1] // hp
    scale = dk ** -0.5

    for z in range(2):
        pre = jnp.dot(low_ref[z].astype(BF16), wa2_ref[z].astype(BF16),
                      preferred_element_type=F32) + ba_ref[z]
        log_sig = jnp.minimum(pre, 0.0) - jnp.log1p(jnp.exp(-jnp.abs(pre)))
        gk_scr[z] = log_sig * (1.0 / GLA_NORMALIZER)
        for hh in range(hp):
            s_t[z, hh] = s0_ref[z, hh].T if has_s0 else jnp.zeros((dv, dk), F32)

    row = lax.broadcasted_iota(jnp.int32, (c, c), 0)
    col = lax.broadcasted_iota(jnp.int32, (c, c), 1)
    masks = (col <= row, col >= row)
    tris = tuple(jnp.where(m, 1.0, 0.0).astype(BF16) for m in masks)
    nt_dims = (((1,), (1,)), ((), ()))
    tn_dims = (((0,), (0,)), ((), ()))

    n_stages = 5

    def chunk_stages(ci, z, hh):
        rows = pl.ds(pl.multiple_of(ci * c, c), c)
        ck = slice(hh * dk, (hh + 1) * dk)
        cv = slice(hh * dv, (hh + 1) * dv)
        gcum = _prefix_sum_mm(tris[z], gk_scr[z, rows, ck])
        yield
        qc = q_ref[rows, ck].astype(F32) * scale
        kc = k_ref[rows, ck].astype(F32)
        if z == 0:
            g_mid = gcum[c // 2 - 1:c // 2]
            g_last = gcum[c - 1:c]
        else:
            g_mid = gcum[c // 2:c // 2 + 1]
            g_last = gcum[0:1]
        q_in = (qc * jnp.exp(gcum - g_mid)).astype(BF16)
        k_in = (kc * jnp.exp(g_mid - gcum)).astype(BF16)
        a = lax.dot_general(q_in, k_in, nt_dims, preferred_element_type=F32)
        yield
        vc = v_ref[rows, cv].astype(BF16)
        a = jnp.where(masks[z], a, 0.0).astype(BF16)
        o = jnp.dot(a, vc, preferred_element_type=F32)
        yield
        s_prev = s_t[z, hh]
        q_dec = (qc * jnp.exp(gcum)).astype(BF16)
        o = o + lax.dot_general(q_dec, s_prev.astype(BF16), nt_dims, preferred_element_type=F32)
        oacc[z, rows, cv] = o
        yield
        k_dec = (kc * jnp.exp(g_last - gcum)).astype(BF16)
        u_t = lax.dot_general(vc, k_dec, tn_dims, preferred_element_type=F32)
        s_t[z, hh] = s_prev * jnp.exp(g_last) + u_t
        yield

    def body(i, carry):
        scans = [chunk_stages(i if z == 0 else n - 1 - i, z, hh)
                 for z in range(2) for hh in range(hp)]
        for _ in range(n_stages):
            for scan in scans:
                next(scan)
        return carry

    lax.fori_loop(0, n, body, 0)

    for hh in range(hp):
        cv = slice(hh * dv, (hh + 1) * dv)
        if out_state:
            for z in range(2):
                st_ref[z, hh] = s_t[z, hh].T
        o = oacc[0, :, cv] + oacc[1, :, cv]
        on = o * lax.rsqrt(jnp.mean(o * o, axis=-1, keepdims=True) + RMS_EPS) * ng_ref[...]
        o_ref[:, cv] = (on * _silu(g_ref[:, cv].astype(F32))).astype(o_ref.dtype)


def _gla_scan(proj, low, w_a2, b_a, norm_g, state, layer_j, *, batch, seq_len, row0, hp,
              out_state):
    h = GLA_HEADS
    hk = w_a2.shape[2]
    dk = hk // h
    hv = (proj.shape[1] - 2 * hk) // 2
    dv = hv // h
    rank = w_a2.shape[1]
    off = row0 // seq_len
    has_s0 = state is not None
    wk, wv = hp * dk, hp * dv
    in_specs = [
        pl.BlockSpec((seq_len, wk), lambda b, i: (b + off, i)),
        pl.BlockSpec((seq_len, wk), lambda b, i: (b + off, hk // wk + i)),
        pl.BlockSpec((seq_len, wv), lambda b, i: (b + off, 2 * hk // wv + i)),
        pl.BlockSpec((seq_len, wv), lambda b, i: (b + off, (2 * hk + hv) // wv + i)),
        pl.BlockSpec((2, seq_len, rank), lambda b, i: (0, b + off, 0)),
        pl.BlockSpec((2, rank, wk), lambda b, i: (0, 0, i)),
        pl.BlockSpec((2, 1, wk), lambda b, i: (0, 0, i)),
        pl.BlockSpec((1, dv), lambda b, i: (0, 0)),
    ]
    args = [proj, proj, proj, proj, low, w_a2, b_a.reshape(2, 1, hk), norm_g.reshape(1, dv)]
    if has_s0:
        in_specs.append(pl.BlockSpec((None, None, 2, hp, dk, dv),
                                     lambda b, i: (b, layer_j, 0, i, 0, 0)))
        args.append(state)
    out_specs = [pl.BlockSpec((seq_len, wv), lambda b, i: (b, i))]
    out_shape = [jax.ShapeDtypeStruct((batch * seq_len, hv), BF16)]
    if out_state:
        out_specs.append(pl.BlockSpec((None, 2, hp, dk, dv), lambda b, i: (b, 0, i, 0, 0)))
        out_shape.append(jax.ShapeDtypeStruct((batch, 2, h, dk, dv), F32))
    return pl.pallas_call(
        functools.partial(_gla_kernel, seq_len=seq_len, hp=hp, has_s0=has_s0,
                          out_state=out_state),
        grid=(batch, h // hp),
        in_specs=in_specs,
        out_specs=out_specs,
        out_shape=out_shape,
        scratch_shapes=[
            pltpu.VMEM((2, seq_len, wk), F32),
            pltpu.VMEM((2, seq_len, wv), F32),
            pltpu.VMEM((2, hp, dv, dk), F32),
        ],
        compiler_params=_cparams(2),
        name="gla_scan",
    )(*args)


def _sgu_kernel(u_ref, v_ref, lg_ref, lb_ref, ws_ref, bs_ref, o_ref):
    v = v_ref[...].astype(F32)
    vn = _layer_norm(v, lg_ref[...], lb_ref[...]).astype(BF16)
    rows, half = vn.shape
    cg = half // SGU_GROUPS
    for r in range(rows // SGU_CHUNK):
        rs = slice(r * SGU_CHUNK, (r + 1) * SGU_CHUNK)
        for g in range(SGU_GROUPS):
            cs = slice(g * cg, (g + 1) * cg)
            s = jnp.dot(ws_ref[g].astype(BF16), vn[rs, cs], preferred_element_type=F32) + bs_ref[g]
            o_ref[rs, cs] = (u_ref[rs, cs].astype(F32) * s).astype(o_ref.dtype)


def _sgu_gate(z, ln_g, ln_b, w_s, b_s):
    t, ffn = z.shape
    half = ffn // 2
    rows = SGU_CHUNK
    return pl.pallas_call(
        _sgu_kernel,
        grid=(t // rows,),
        in_specs=[
            pl.BlockSpec((rows, half), lambda i: (i, 0)),
            pl.BlockSpec((rows, half), lambda i: (i, 1)),
            pl.BlockSpec((1, half), lambda i: (0, 0)),
            pl.BlockSpec((1, half), lambda i: (0, 0)),
            pl.BlockSpec((SGU_GROUPS, SGU_CHUNK, SGU_CHUNK), lambda i: (0, 0, 0)),
            pl.BlockSpec((SGU_GROUPS, SGU_CHUNK, 1), lambda i: (0, 0, 0)),
        ],
        out_specs=pl.BlockSpec((rows, half), lambda i: (i, 0)),
        out_shape=jax.ShapeDtypeStruct((t, half), BF16),
        compiler_params=_cparams(1),
        name="sgu_gate",
    )(z, z, ln_g.reshape(1, half), ln_b.reshape(1, half), w_s, b_s.reshape(SGU_GROUPS, SGU_CHUNK, 1))


def _router_kernel(x_ref, wrt_ref, br_ref, sel_ref, wd_ref, pos_ref, cnt_ref, carry):
    i = pl.program_id(0)

    @pl.when(i == 0)
    def _():
        carry[...] = jnp.zeros_like(carry)

    x = _unpack_pairs_bf16(x_ref[...])
    logits = lax.dot_general(wrt_ref[...].astype(BF16), x, (((1,), (1,)), ((), ())),
                             preferred_element_type=F32)
    n_exp, tm = logits.shape
    ng = N_GROUPS
    per = n_exp // ng
    scores = jax.nn.sigmoid(logits)
    selv = scores + br_ref[...]
    a = [selv[j * ng:(j + 1) * ng] for j in range(per)]
    sc = [scores[j * ng:(j + 1) * ng] for j in range(per)]

    t1 = a[0]
    t2 = jnp.full_like(t1, -jnp.inf)
    for j in range(1, per):
        t2 = jnp.maximum(t2, jnp.minimum(t1, a[j]))
        t1 = jnp.maximum(t1, a[j])
    gscore = t1 + t2

    gidx = lax.broadcasted_iota(jnp.int32, (ng, tm), 0)
    grank = jnp.zeros((ng, tm), jnp.int32)
    for gp in range(ng):
        other = gscore[gp:gp + 1, :]
        beats = (other > gscore) | ((other == gscore) & (gp < gidx))
        grank = grank + beats.astype(jnp.int32)
    gmask = grank < TOPK_GROUPS

    val = [jnp.where(gmask, a[j], -jnp.inf) for j in range(per)]
    ranks = [jnp.zeros((ng, tm), jnp.int32) for _ in range(per)]
    for gp in range(ng):
        lower_group = gp < gidx
        lower_or_same_group = gp <= gidx
        for jp in range(per):
            other = val[jp][gp:gp + 1, :]
            for j in range(per):
                tie = lower_or_same_group if jp < j else lower_group
                beats = (other > val[j]) | ((other == val[j]) & tie)
                ranks[j] = ranks[j] + beats.astype(jnp.int32)
    selm = [(ranks[j] < TOP_K) & gmask for j in range(per)]

    ssum = jnp.zeros((ng, tm), F32)
    for j in range(per):
        ssum = ssum + jnp.where(selm[j], sc[j], 0.0)
    ssum = jnp.sum(ssum, axis=0, keepdims=True)
    wd = jnp.concatenate(
        [jnp.where(selm[j], sc[j] / ssum * ROUTE_SCALE, 0.0) for j in range(per)], axis=0)
    selmat = jnp.concatenate([jnp.where(selm[j], 1.0, 0.0) for j in range(per)], axis=0)

    s_i = lax.broadcasted_iota(jnp.int32, (tm, tm), 0)
    t_i = lax.broadcasted_iota(jnp.int32, (tm, tm), 1)
    upper = jnp.where(s_i < t_i, 1.0, 0.0).astype(BF16)
    base = carry[...][:, 0:1]
    pos = jnp.dot(selmat.astype(BF16), upper, preferred_element_type=F32) + base
    total = base + jnp.sum(selmat, axis=1, keepdims=True)
    carry[...] = jnp.broadcast_to(total, carry.shape)

    sel_ref[...] = selmat
    wd_ref[...] = wd
    pos_ref[...] = pos
    cnt_ref[...] = jnp.broadcast_to(total, cnt_ref.shape)


def _router(hm, w_rt, b_r_col):
    t = hm.shape[0]
    n_exp, d = w_rt.shape
    tm = _row_tile(512, t)
    col = pl.BlockSpec((n_exp, tm), lambda i: (0, i))
    return pl.pallas_call(
        _router_kernel,
        grid=(t // tm,),
        in_specs=[
            pl.BlockSpec((tm, d // 2), lambda i: (i, 0)),
            pl.BlockSpec((n_exp, d), lambda i: (0, 0)),
            pl.BlockSpec((n_exp, 1), lambda i: (0, 0)),
        ],
        out_specs=[col, col, col, pl.BlockSpec((n_exp, LANES), lambda i: (0, 0))],
        out_shape=[
            jax.ShapeDtypeStruct((n_exp, t), F32),
            jax.ShapeDtypeStruct((n_exp, t), F32),
            jax.ShapeDtypeStruct((n_exp, t), F32),
            jax.ShapeDtypeStruct((n_exp, LANES), F32),
        ],
        scratch_shapes=[pltpu.VMEM((n_exp, LANES), F32)],
        compiler_params=_cparams(1),
        name="router",
    )(hm, w_rt, b_r_col)


def _slots_kernel(sel_ref, wd_ref, pos_ref, ps_ref, d8_ref, w8_ref):
    sel = sel_ref[...]
    n_exp = sel.shape[0]
    r_i = lax.broadcasted_iota(jnp.int32, (n_exp, n_exp), 0)
    c_i = lax.broadcasted_iota(jnp.int32, (n_exp, n_exp), 1)
    lower = jnp.where(c_i < r_i, 1.0, 0.0).astype(BF16)
    slot = jnp.dot(lower, sel.astype(BF16), preferred_element_type=F32)
    chosen = sel > 0.5
    dest = pos_ref[...] + ps_ref[...]
    wd = wd_ref[...]
    d_rows, w_rows = [], []
    for k in range(TOP_K):
        pick = chosen & (slot == k)
        d_rows.append(jnp.sum(jnp.where(pick, dest, 0.0), axis=0, keepdims=True))
        w_rows.append(jnp.sum(jnp.where(pick, wd, 0.0), axis=0, keepdims=True))
    d8_ref[...] = jnp.concatenate(d_rows, axis=0).astype(jnp.int32)
    w8_ref[...] = jnp.concatenate(w_rows, axis=0)


def _slots(sel, wd, pos, pstart_col):
    n_exp, t = sel.shape
    tm = _row_tile(512, t)
    col = pl.BlockSpec((n_exp, tm), lambda i: (0, i))
    out = pl.BlockSpec((TOP_K, tm), lambda i: (0, i))
    return pl.pallas_call(
        _slots_kernel,
        grid=(t // tm,),
        in_specs=[col, col, col, pl.BlockSpec((n_exp, 1), lambda i: (0, 0))],
        out_specs=[out, out],
        out_shape=[jax.ShapeDtypeStruct((TOP_K, t), jnp.int32),
                   jax.ShapeDtypeStruct((TOP_K, t), F32)],
        compiler_params=_cparams(1),
        name="moe_slots",
    )(sel, wd, pos, pstart_col)


def _dispatch_kernel(dest_ref, fill_ref, x_ref, o_hbm, zrow, sem, *, tm, fill_steps, fill_per_step):
    def body(t, carry):
        for kk in range(TOP_K):
            d = dest_ref[0, t * TOP_K + kk]
            pltpu.make_async_copy(x_ref.at[pl.ds(t, 1), :], o_hbm.at[pl.ds(d, 1), :], sem).start()
        return carry

    lax.fori_loop(0, tm, body, 0)

    @pl.when(pl.program_id(0) < fill_steps)
    def _():
        zrow[...] = jnp.zeros_like(zrow)

        def fill(f, carry):
            d = fill_ref[0, f]
            pltpu.make_async_copy(zrow.at[pl.ds(0, 1), :], o_hbm.at[pl.ds(d, 1), :], sem).start()
            return carry

        lax.fori_loop(0, fill_per_step, fill, 0)
        for _ in range(fill_per_step // tm):
            pltpu.make_async_copy(x_ref, o_hbm.at[pl.ds(0, tm), :], sem).wait()

    for _ in range(TOP_K):
        pltpu.make_async_copy(x_ref, o_hbm.at[pl.ds(0, tm), :], sem).wait()


def _dispatch(hm, dest_tiles, fill, cap, tm):
    t, d = hm.shape
    steps = t // tm
    fill_steps = 1
    while fill_steps * 2 <= steps:
        fill_steps *= 2
    n_fill = fill.shape[0]
    fill_per_step = n_fill // fill_steps
    assert fill_per_step * fill_steps == n_fill and fill_per_step % tm == 0
    return pl.pallas_call(
        functools.partial(_dispatch_kernel, tm=tm, fill_steps=fill_steps,
                          fill_per_step=fill_per_step),
        grid=(steps,),
        in_specs=[
            pl.BlockSpec((None, 1, tm * TOP_K), lambda i: (i, 0, 0), memory_space=pltpu.SMEM),
            pl.BlockSpec((None, 1, fill_per_step),
                         lambda i: (jnp.minimum(i, fill_steps - 1), 0, 0),
                         memory_space=pltpu.SMEM),
            pl.BlockSpec((tm, d), lambda i: (i, 0)),
        ],
        out_specs=pl.BlockSpec(memory_space=pl.ANY),
        out_shape=jax.ShapeDtypeStruct((cap, d), hm.dtype),
        scratch_shapes=[pltpu.VMEM((SUBLANES, d), hm.dtype), pltpu.SemaphoreType.DMA],
        compiler_params=_cparams(1),
        name="moe_dispatch",
    )(dest_tiles, fill.reshape(fill_steps, 1, fill_per_step), hm)


def _swiglu(x, wgu, wd, ff):
    h = jnp.dot(x, wgu, preferred_element_type=F32)
    act = (_silu(h[:, :ff]) * h[:, ff:]).astype(BF16)
    return jnp.dot(act, wd, preferred_element_type=F32)


def _expert_kernel(be_ref, nu_ref, x_ref, wgu_ref, wd_ref, o_ref, wgu_bf, wd_bf, *, ff):
    b = pl.program_id(0)

    @pl.when(b < nu_ref[0])
    def _():
        @pl.when((b == 0) | (be_ref[b] != be_ref[jnp.maximum(b - 1, 0)]))
        def _():
            wgu_bf[...] = wgu_ref[...].astype(BF16)
            wd_bf[...] = wd_ref[...].astype(BF16)

        y = _swiglu(_unpack_pairs_bf16(x_ref[...]), wgu_bf[...], wd_bf[...], ff)
        o_ref[...] = _pack_pairs(y)

    @pl.when(b >= nu_ref[0])
    def _():
        o_ref[...] = jnp.zeros_like(o_ref)


def _experts(xs, block_e, n_used, w_gu, w_down, layer):
    cap, dh = xs.shape
    d = 2 * dh
    ff = w_down.shape[2]
    n_blocks = cap // MOE_BLOCK

    def blk(b, be, nu):
        return jnp.minimum(b, nu[0] - 1)

    return pl.pallas_call(
        functools.partial(_expert_kernel, ff=ff),
        grid_spec=pltpu.PrefetchScalarGridSpec(
            num_scalar_prefetch=2,
            grid=(n_blocks,),
            in_specs=[
                pl.BlockSpec((MOE_BLOCK, dh), lambda b, be, nu: (blk(b, be, nu), 0)),
                pl.BlockSpec((None, None, d, 2 * ff),
                             lambda b, be, nu: (layer, be[blk(b, be, nu)], 0, 0)),
                pl.BlockSpec((None, None, ff, d),
                             lambda b, be, nu: (layer, be[blk(b, be, nu)], 0, 0)),
            ],
            out_specs=pl.BlockSpec((MOE_BLOCK, dh), lambda b, be, nu: (b, 0)),
            scratch_shapes=[pltpu.VMEM((d, 2 * ff), BF16), pltpu.VMEM((ff, d), BF16)],
        ),
        out_shape=jax.ShapeDtypeStruct((cap, dh), jnp.uint32),
        compiler_params=_cparams(1),
        name="moe_experts",
    )(block_e, n_used, xs, w_gu, w_down)


def _shared_kernel(x_ref, wgu_ref, wd_ref, o_ref, wgu_bf, wd_bf, *, ff):
    @pl.when(pl.program_id(0) == 0)
    def _():
        wgu_bf[...] = wgu_ref[...].astype(BF16)
        wd_bf[...] = wd_ref[...].astype(BF16)

    o_ref[...] = _swiglu(_unpack_pairs_bf16(x_ref[...]), wgu_bf[...], wd_bf[...], ff)


def _shared_expert(hm, ws_gu, ws_down, layer):
    t, dh = hm.shape
    d = 2 * dh
    ff = ws_down.shape[1]
    tm = _row_tile(512, t)
    return pl.pallas_call(
        functools.partial(_shared_kernel, ff=ff),
        grid=(t // tm,),
        in_specs=[
            pl.BlockSpec((tm, dh), lambda i: (i, 0)),
            pl.BlockSpec((None, d, 2 * ff), lambda i: (layer, 0, 0)),
            pl.BlockSpec((None, ff, d), lambda i: (layer, 0, 0)),
        ],
        out_specs=pl.BlockSpec((tm, d), lambda i: (i, 0)),
        out_shape=jax.ShapeDtypeStruct((t, d), F32),
        scratch_shapes=[pltpu.VMEM((d, 2 * ff), BF16), pltpu.VMEM((ff, d), BF16)],
        compiler_params=_cparams(1),
        name="moe_shared",
    )(hm, ws_gu, ws_down)


def _combine_kernel(dcur_ref, dnext_ref, w8_ref, ysh_ref, x_ref, gate_ref, g_ref, b_ref, *rest,
                    tm, alpha, n_steps, with_h):
    if with_h:
        sh_ref, sc_ref, yb_hbm, xo_ref, ho_ref, buf, sem = rest
    else:
        yb_hbm, xo_ref, buf, sem = rest
    i = pl.program_id(0)
    slot = i % 2

    def issue(dref, s):
        def body(t, carry):
            for kk in range(TOP_K):
                d = dref[0, t * TOP_K + kk]
                pltpu.make_async_copy(yb_hbm.at[pl.ds(d, 1), :],
                                      buf.at[s, kk, pl.ds(t, 1), :], sem.at[s]).start()
            return carry

        lax.fori_loop(0, tm, body, 0)

    @pl.when(i == 0)
    def _():
        issue(dcur_ref, 0)

    @pl.when(i + 1 < n_steps)
    def _():
        issue(dnext_ref, 1 - slot)

    for kk in range(TOP_K):
        pltpu.make_async_copy(yb_hbm.at[pl.ds(0, tm), :], buf.at[slot, kk], sem.at[slot]).wait()

    w8 = w8_ref[...]
    half = buf.shape[-1]
    acc_lo = ysh_ref[:, :half]
    acc_hi = ysh_ref[:, half:]
    for kk in range(TOP_K):
        rows = buf[slot, kk]
        wk = w8[:, kk:kk + 1]
        acc_lo = acc_lo + wk * _unpack_lo(rows)
        acc_hi = acc_hi + wk * _unpack_hi(rows)
    acc = jnp.concatenate([acc_lo, acc_hi], axis=-1)
    y = alpha * x_ref[...] + gate_ref[...] * acc
    xn = _layer_norm(y, g_ref[...], b_ref[...])
    xo_ref[...] = xn
    if with_h:
        _store_h(ho_ref, xn * (1.0 + sc_ref[...]) + sh_ref[...])


def _combine(yb, dest_tiles, w8, y_sh, x, gate, ln_g, ln_b, shift, scale, alpha, tp, seq_s, tm):
    t, d = x.shape
    n_steps = t // tm
    with_h = shift is not None
    cid = _cond_index(tm, tp, seq_s)
    row = pl.BlockSpec((tm, d), lambda i: (i, 0))
    mod_spec = pl.BlockSpec((None, 1, d), lambda i: (cid(i), 0, 0))
    vec = pl.BlockSpec((1, d), lambda i: (0, 0))
    in_specs = [
        pl.BlockSpec((None, 1, tm * TOP_K), lambda i: (i, 0, 0), memory_space=pltpu.SMEM),
        pl.BlockSpec((None, 1, tm * TOP_K), lambda i: (jnp.minimum(i + 1, n_steps - 1), 0, 0),
                     memory_space=pltpu.SMEM),
        pl.BlockSpec((tm, TOP_K), lambda i: (i, 0)),
        row, row, mod_spec, vec, vec,
    ]
    args = [dest_tiles, dest_tiles, w8, y_sh, x, gate, ln_g.reshape(1, d), ln_b.reshape(1, d)]
    out_specs = [row]
    out_shape = [jax.ShapeDtypeStruct((t, d), F32)]
    if with_h:
        in_specs += [mod_spec, mod_spec]
        args += [shift, scale]
        out_specs.append(row)
        out_shape.append(jax.ShapeDtypeStruct((t, d), BF16))
    in_specs.append(pl.BlockSpec(memory_space=pl.ANY))
    args.append(yb)
    return pl.pallas_call(
        functools.partial(_combine_kernel, tm=tm, alpha=alpha, n_steps=n_steps, with_h=with_h),
        grid=(n_steps,),
        in_specs=in_specs,
        out_specs=out_specs,
        out_shape=out_shape,
        scratch_shapes=[pltpu.VMEM((2, TOP_K, tm, d // 2), jnp.uint32),
                        pltpu.SemaphoreType.DMA((2,))],
        compiler_params=_cparams(1),
        name="moe_combine",
    )(*args)


def _expert_row_order(n_exp):
    per = n_exp // N_GROUPS
    r = jnp.arange(n_exp)
    return (r % N_GROUPS) * per + r // N_GROUPS


def _count_le(edges, v):
    return jnp.sum((edges[None, :] <= v[:, None]).astype(jnp.int32), axis=1)


def _moe_layer(x, hm, layer, w_r, b_r, w_gu, w_down, ws_gu, ws_down, gate, ln_g, ln_b, shift,
               scale, alpha, tp, seq_s):
    t = hm.shape[0]
    n_exp = w_r.shape[1]
    order = _expert_row_order(n_exp)
    sel, wd, pos, cnt = _router(hm, w_r.T[order], b_r[order].reshape(n_exp, 1))

    assert (t * TOP_K) % MOE_BLOCK == 0
    n_blocks = t * TOP_K // MOE_BLOCK + n_exp
    cap = n_blocks * MOE_BLOCK
    counts = cnt[:, 0].astype(jnp.int32)
    padded = (counts + MOE_BLOCK - 1) // MOE_BLOCK * MOE_BLOCK
    pend = jnp.cumsum(padded)
    pstart = pend - padded
    n_used = (pend[-1:] // MOE_BLOCK).astype(jnp.int32)
    block_row = jnp.minimum(_count_le(pend, jnp.arange(n_blocks) * MOE_BLOCK), n_exp - 1)
    block_e = order[block_row].astype(jnp.int32)
    dest8, w8 = _slots(sel, wd, pos, pstart.astype(F32).reshape(n_exp, 1))
    dest8 = dest8.T

    padlen = padded - counts
    cpad_end = jnp.cumsum(padlen)
    f = jnp.arange(cap - t * TOP_K)
    fr = jnp.minimum(_count_le(cpad_end, f), n_exp - 1)
    fill = jnp.where(f < cpad_end[-1],
                     pstart[fr] + counts[fr] + f - (cpad_end - padlen)[fr],
                     pend[-1] + f - cpad_end[-1]).astype(jnp.int32)

    tm_d = _row_tile(256, tp, seq_s)
    xs = _dispatch(hm, dest8.reshape(t // tm_d, 1, tm_d * TOP_K), fill, cap, tm_d)
    yb = _experts(xs, block_e, n_used, w_gu, w_down, layer)
    y_sh = _shared_expert(hm, ws_gu, ws_down, layer)
    tm_c = _row_tile(128, tp, seq_s)
    return _combine(yb, dest8.reshape(t // tm_c, 1, tm_c * TOP_K), w8.T, y_sh, x, gate, ln_g,
                    ln_b, shift, scale, alpha, tp, seq_s, tm_c)


def _gla_layer(h, state_gla, layer_j, w_in, w_a1, w_a2, b_a, norm_g, w_out, dims):
    bp, sp, bs, ss = dims
    tp = bp * sp
    d = h.shape[1]
    rank = w_a1.shape[3]
    proj = _matmul(h, w_in, layer_j, BF16)
    w_low = jnp.transpose(w_a1[layer_j], (1, 0, 2)).reshape(1, d, 2 * rank)
    low = _matmul(h, w_low, 0, F32)
    low = jnp.transpose(low.reshape(-1, 2, rank), (1, 0, 2))
    og_p, st = _gla_scan(proj, low, w_a2[layer_j], b_a[layer_j], norm_g[layer_j], None, layer_j,
                         batch=bp, seq_len=sp, row0=0, hp=GLA_HEADS, out_state=True)
    (og_s,) = _gla_scan(proj, low, w_a2[layer_j], b_a[layer_j], norm_g[layer_j], state_gla,
                        layer_j, batch=bs, seq_len=ss, row0=tp, hp=GLA_HEADS // 2,
                        out_state=False)
    out = _matmul(jnp.concatenate([og_p, og_s], axis=0), w_out, layer_j, BF16)
    return out, st


def _sgu_layer(h, layer_j, w_in, ln_g, ln_b, w_s, b_s, w_out):
    z = _matmul(h, w_in, layer_j, BF16, act="gelu")
    gated = _sgu_gate(z, ln_g[layer_j], ln_b[layer_j], w_s[layer_j], b_s[layer_j])
    return _matmul(gated, w_out, layer_j, BF16)


def _grid_pos_embed(rows, d):
    t = jnp.arange(rows * GRID_W)
    r = (t // GRID_W).astype(F32)
    col = (t % GRID_W).astype(F32)
    quarter = d // 4
    omega = 1.0 / (10000.0 ** (jnp.arange(quarter, dtype=F32) / quarter))

    def emb(p):
        a = p[:, None] * omega[None, :]
        return jnp.concatenate([jnp.sin(a), jnp.cos(a)], axis=-1)

    return jnp.concatenate([emb(r), emb(col)], axis=-1)


def kernel(x_prompt, x_sample, state_gla, c, c_ctx, w_ada, b_ada, ln_g, ln_b, gla_w_in, gla_w_a1, gla_w_a2, gla_b_a, gla_norm_g, gla_w_out, sgu_w_in, sgu_ln_g, sgu_ln_b, sgu_w_s, sgu_b_s, sgu_w_out, moe_w_router, moe_b_router, moe_w_gu, moe_w_down, moe_ws_gu, moe_ws_down):
    bp, sp, d = x_prompt.shape
    bs, ss, _ = x_sample.shape
    assert 1 + bs <= COND_ROWS
    tp, ts = bp * sp, bs * ss
    depth = w_ada.shape[0]
    alpha = (2 * depth) ** 0.25
    dims = (bp, sp, bs, ss)

    cond = jnp.zeros((COND_ROWS, d), F32).at[0].set(c_ctx).at[1:1 + bs].set(c)
    mods = _ada_mods(cond, w_ada, b_ada).reshape(depth, COND_ROWS, 6, 1, d)

    def mod(l, i):
        return mods[l, :, i]

    pos = _grid_pos_embed(ss // GRID_W, d).astype(x_sample.dtype)
    x, h = _prep(x_prompt.reshape(tp, d), x_sample.reshape(ts, d), pos, mod(0, 0), mod(0, 1), ss)

    states = []
    for l in range(depth):
        j = l // 2
        if l % 2 == 0:
            out, st = _gla_layer(h, state_gla, j, gla_w_in, gla_w_a1, gla_w_a2, gla_b_a,
                                 gla_norm_g, gla_w_out, dims)
            states.append(st)
        else:
            out = _sgu_layer(h, j, sgu_w_in, sgu_ln_g, sgu_ln_b, sgu_w_s, sgu_b_s, sgu_w_out)
        x, hm = _post_norm(x, out, mod(l, 2), ln_g[l, 0], ln_b[l, 0], mod(l, 3), mod(l, 4),
                           alpha, tp, ss)
        last = l == depth - 1
        res = _moe_layer(x, hm, l, moe_w_router[l], moe_b_router[l], moe_w_gu, moe_w_down,
                         moe_ws_gu, moe_ws_down, mod(l, 5), ln_g[l, 1], ln_b[l, 1],
                         None if last else mod(l + 1, 0), None if last else mod(l + 1, 1),
                         alpha, tp, ss)
        if last:
            (x,) = res
        else:
            x, h = res

    new_state = jnp.stack(states, axis=1)
    return x[:tp].reshape(bp, sp, d), x[tp:].reshape(bs, ss, d), new_state
```

```python
import functools

import jax
import jax.numpy as jnp
from jax import lax
from jax.experimental import pallas as pl
from jax.experimental.pallas import tpu as pltpu

F32 = jnp.float32
BF16 = jnp.bfloat16

V7X_VMEM_LIMIT_BYTES = 56 * 1024 * 1024
LANES = 128
SUBLANES = 8

GRID_W = 64
GLA_HEADS = 4
GLA_CHUNK = 64
GLA_NORMALIZER = 16.0
SGU_GROUPS = 4
SGU_CHUNK = 128
TOP_K = 8
N_GROUPS = 8
TOPK_GROUPS = 4
ROUTE_SCALE = 2.5
LN_EPS = 1e-5
RMS_EPS = 1e-6
MOE_BLOCK = 256
COND_ROWS = 8


def _cparams(n_axes):
    return pltpu.CompilerParams(
        dimension_semantics=("arbitrary",) * n_axes,
        vmem_limit_bytes=V7X_VMEM_LIMIT_BYTES,
    )


def _silu(x):
    return x * jax.nn.sigmoid(x)


def _layer_norm(y, g, b):
    mu = jnp.mean(y, axis=-1, keepdims=True)
    d = y - mu
    var = jnp.mean(d * d, axis=-1, keepdims=True)
    return d * lax.rsqrt(var + LN_EPS) * g + b


_HI16 = 0xFFFF0000


def _pack_pairs(x):
    half = x.shape[-1] // 2
    lo = lax.bitcast_convert_type(x[:, :half].astype(BF16).astype(F32), jnp.uint32)
    hi = lax.bitcast_convert_type(x[:, half:].astype(BF16).astype(F32), jnp.uint32)
    return lax.shift_right_logical(lo, jnp.uint32(16)) | (hi & jnp.uint32(_HI16))


def _unpack_lo(w):
    return lax.bitcast_convert_type(lax.shift_left(w, jnp.uint32(16)), F32)


def _unpack_hi(w):
    return lax.bitcast_convert_type(w & jnp.uint32(_HI16), F32)


def _unpack_pairs_bf16(w):
    return jnp.concatenate([_unpack_lo(w).astype(BF16), _unpack_hi(w).astype(BF16)], axis=-1)


def _row_tile(limit, *sizes):
    tm = limit
    while any(s % tm for s in sizes):
        tm //= 2
    return tm


def _cond_index(tm, tp, seq_s):
    def index(i):
        return jnp.where(i * tm < tp, 0, 1 + (i * tm - tp) // seq_s)

    return index


def _ada_kernel(c_ref, w_ref, b_ref, o_ref):
    s = _silu(c_ref[...]).astype(BF16)
    o_ref[...] = jnp.dot(s, w_ref[...].astype(BF16), preferred_element_type=F32) + b_ref[...]


def _ada_mods(cond, w_ada, b_ada):
    depth, d, n = w_ada.shape
    tn = _row_tile(1024, n)
    return pl.pallas_call(
        _ada_kernel,
        grid=(depth, n // tn),
        in_specs=[
            pl.BlockSpec((COND_ROWS, d), lambda l, j: (0, 0)),
            pl.BlockSpec((None, d, tn), lambda l, j: (l, 0, j)),
            pl.BlockSpec((None, 1, tn), lambda l, j: (l, 0, j)),
        ],
        out_specs=pl.BlockSpec((None, COND_ROWS, tn), lambda l, j: (l, 0, j)),
        out_shape=jax.ShapeDtypeStruct((depth, COND_ROWS, n), F32),
        compiler_params=_cparams(2),
        name="ada_mods",
    )(cond, w_ada, b_ada.reshape(depth, 1, n))


def _prep_kernel(xp_ref, xs_ref, pos_ref, sh_ref, sc_ref, x_ref, h_ref, *, n_p):
    x = jnp.where(pl.program_id(0) < n_p, xp_ref[...], xs_ref[...] + pos_ref[...])
    x_ref[...] = x
    h_ref[...] = (x * (1.0 + sc_ref[...]) + sh_ref[...]).astype(h_ref.dtype)


def _prep(xp, xs, pos, shift, scale, seq_s):
    tp, d = xp.shape
    ts = xs.shape[0]
    tm = _row_tile(512, tp, seq_s)
    n_p = tp // tm
    n_pos = seq_s // tm
    cid = _cond_index(tm, tp, seq_s)
    mod_spec = pl.BlockSpec((None, 1, d), lambda i: (cid(i), 0, 0))
    return pl.pallas_call(
        functools.partial(_prep_kernel, n_p=n_p),
        grid=((tp + ts) // tm,),
        in_specs=[
            pl.BlockSpec((tm, d), lambda i: (jnp.minimum(i, n_p - 1), 0)),
            pl.BlockSpec((tm, d), lambda i: (jnp.maximum(i - n_p, 0), 0)),
            pl.BlockSpec((tm, d), lambda i: (jnp.maximum(i - n_p, 0) % n_pos, 0)),
            mod_spec,
            mod_spec,
        ],
        out_specs=[pl.BlockSpec((tm, d), lambda i: (i, 0))] * 2,
        out_shape=[
            jax.ShapeDtypeStruct((tp + ts, d), F32),
            jax.ShapeDtypeStruct((tp + ts, d), BF16),
        ],
        compiler_params=_cparams(1),
        name="prep",
    )(xp, xs, pos, shift, scale)


def _mm_kernel(x_ref, w_ref, o_ref, wbf_ref, *, act):
    @pl.when(pl.program_id(1) == 0)
    def _():
        wbf_ref[...] = w_ref[...].astype(BF16)

    acc = jnp.dot(x_ref[...].astype(BF16), wbf_ref[...], preferred_element_type=F32)
    if act == "gelu":
        acc = 0.5 * acc * (1.0 + lax.erf(acc * 0.7071067811865476))
    o_ref[...] = acc.astype(o_ref.dtype)


def _matmul(x, w, layer, out_dtype, act=None):
    m, k = x.shape
    n = w.shape[2]
    tm = _row_tile(512, m)
    tn_limit = 1024
    while k * tn_limit * 10 > V7X_VMEM_LIMIT_BYTES * 4 // 7 and tn_limit > LANES:
        tn_limit //= 2
    tn = _row_tile(tn_limit, n)
    return pl.pallas_call(
        functools.partial(_mm_kernel, act=act),
        grid=(n // tn, m // tm),
        in_specs=[
            pl.BlockSpec((tm, k), lambda j, i: (i, 0)),
            pl.BlockSpec((None, k, tn), lambda j, i: (layer, 0, j)),
        ],
        out_specs=pl.BlockSpec((tm, tn), lambda j, i: (i, j)),
        out_shape=jax.ShapeDtypeStruct((m, n), out_dtype),
        scratch_shapes=[pltpu.VMEM((k, tn), BF16)],
        compiler_params=_cparams(2),
        name="matmul_" + (act or "lin"),
    )(x, w)


def _post_norm_kernel(x_ref, o_ref, gate_ref, g_ref, b_ref, *rest, alpha, with_h):
    y = alpha * x_ref[...] + gate_ref[...] * o_ref[...].astype(F32)
    xn = _layer_norm(y, g_ref[...], b_ref[...])
    if with_h:
        sh_ref, sc_ref, xo_ref, ho_ref = rest
        _store_h(ho_ref, xn * (1.0 + sc_ref[...]) + sh_ref[...])
    else:
        (xo_ref,) = rest
    xo_ref[...] = xn


def _store_h(ho_ref, h):
    ho_ref[...] = _pack_pairs(h) if ho_ref.dtype == jnp.uint32 else h.astype(ho_ref.dtype)


def _post_norm(x, out, gate, ln_g, ln_b, shift, scale, alpha, tp, seq_s):
    t, d = x.shape
    tm = _row_tile(512, tp, seq_s)
    cid = _cond_index(tm, tp, seq_s)
    row = pl.BlockSpec((tm, d), lambda i: (i, 0))
    mod_spec = pl.BlockSpec((None, 1, d), lambda i: (cid(i), 0, 0))
    vec = pl.BlockSpec((1, d), lambda i: (0, 0))
    return pl.pallas_call(
        functools.partial(_post_norm_kernel, alpha=alpha, with_h=True),
        grid=(t // tm,),
        in_specs=[row, row, mod_spec, vec, vec, mod_spec, mod_spec],
        out_specs=[row, pl.BlockSpec((tm, d // 2), lambda i: (i, 0))],
        out_shape=[jax.ShapeDtypeStruct((t, d), F32),
                   jax.ShapeDtypeStruct((t, d // 2), jnp.uint32)],
        compiler_params=_cparams(1),
        name="post_norm",
    )(x, out, gate, ln_g.reshape(1, d), ln_b.reshape(1, d), shift, scale)


def _prefix_sum_mm(tri, x):
    hi = x.astype(BF16)
    r1 = x - hi.astype(F32)
    mid = r1.astype(BF16)
    lo = (r1 - mid.astype(F32)).astype(BF16)
    out = jnp.dot(tri, hi, preferred_element_type=F32)
    out = out + jnp.dot(tri, mid, preferred_element_type=F32)
    return out + jnp.dot(tri, lo, preferred_element_type=F32)


def _gla_kernel(*refs, seq_len, hp, has_s0, out_state, layer_j):
    q_ref, k_ref, v_ref, g_ref, low_ref, wa2_ref, ba_ref, ng_ref = refs[:8]
    rest = list(refs[8:])
    s0_ref = rest.pop(0) if has_s0 else None
    if out_state == "next":
        rest.pop(0)
    o_ref = rest.pop(0)
    st_ref = rest.pop(0) if out_state else None
    gk_scr, oacc, s_t = rest

    c = GLA_CHUNK
    n = seq_len // c
    dk = q_ref.shape[1] // hp
    dv = v_ref.shape[1] // hp
    scale = dk ** -0.5

    for z in range(2):
        pre = jnp.dot(low_ref[z].astype(BF16), wa2_ref[z].astype(BF16),
                      preferred_element_type=F32) + ba_ref[z]
        log_sig = jnp.minimum(pre, 0.0) - jnp.log1p(jnp.exp(-jnp.abs(pre)))
        gk_scr[z] = log_sig * (1.0 / GLA_NORMALIZER)
        for hh in range(hp):
            s_t[z, hh] = s0_ref[z, hh].T if has_s0 else jnp.zeros((dv, dk), F32)

    row = lax.broadcasted_iota(jnp.int32, (c, c), 0)
    col = lax.broadcasted_iota(jnp.int32, (c, c), 1)
    masks = (col <= row, col >= row)
    tris = tuple(jnp.where(m, 1.0, 0.0).astype(BF16) for m in masks)
    nt_dims = (((1,), (1,)), ((), ()))
    tn_dims = (((0,), (0,)), ((), ()))

    n_stages = 5

    def chunk_stages(ci, z, hh):
        rows = pl.ds(pl.multiple_of(ci * c, c), c)
        ck = slice(hh * dk, (hh + 1) * dk)
        cv = slice(hh * dv, (hh + 1) * dv)
        gcum = _prefix_sum_mm(tris[z], gk_scr[z, rows, ck])
        yield
        qc = q_ref[rows, ck].astype(F32) * scale
        kc = k_ref[rows, ck].astype(F32)
        if z == 0:
            g_mid = gcum[c // 2 - 1:c // 2]
            g_last = gcum[c - 1:c]
        else:
            g_mid = gcum[c // 2:c // 2 + 1]
            g_last = gcum[0:1]
        q_in = (qc * jnp.exp(gcum - g_mid)).astype(BF16)
        k_in = (kc * jnp.exp(g_mid - gcum)).astype(BF16)
        a = lax.dot_general(q_in, k_in, nt_dims, preferred_element_type=F32)
        yield
        vc = v_ref[rows, cv].astype(BF16)
        a = jnp.where(masks[z], a, 0.0).astype(BF16)
        o = jnp.dot(a, vc, preferred_element_type=F32)
        yield
        s_prev = s_t[z, hh]
        q_dec = (qc * jnp.exp(gcum)).astype(BF16)
        o = o + lax.dot_general(q_dec, s_prev.astype(BF16), nt_dims, preferred_element_type=F32)
        oacc[z, rows, cv] = o
        yield
        k_dec = (kc * jnp.exp(g_last - gcum)).astype(BF16)
        u_t = lax.dot_general(vc, k_dec, tn_dims, preferred_element_type=F32)
        s_t[z, hh] = s_prev * jnp.exp(g_last) + u_t
        yield

    def body(i, carry):
        scans = [chunk_stages(i if z == 0 else n - 1 - i, z, hh)
                 for z in range(2) for hh in range(hp)]
        for _ in range(n_stages):
            for scan in scans:
                next(scan)
        return carry

    lax.fori_loop(0, n, body, 0)

    for hh in range(hp):
        cv = slice(hh * dv, (hh + 1) * dv)
        for z in range(2):
            if out_state == "next":
                st_ref[z, hh] = s_t[z, hh].T
            elif out_state == "first":
                for j in range(st_ref.shape[0]):
                    st_ref[j, z, hh] = s_t[z, hh].T if j == layer_j else jnp.zeros((dk, dv), F32)
        o = oacc[0, :, cv] + oacc[1, :, cv]
        on = o * lax.rsqrt(jnp.mean(o * o, axis=-1, keepdims=True) + RMS_EPS) * ng_ref[...]
        o_ref[:, cv] = (on * _silu(g_ref[:, cv].astype(F32))).astype(o_ref.dtype)


def _gla_scan(proj, low, w_a2, b_a, norm_g, state, layer_j, *, batch, seq_len, row0, hp,
              new_state=None, n_state_layers=1):
    h = GLA_HEADS
    hk = w_a2.shape[2]
    dk = hk // h
    hv = (proj.shape[1] - 2 * hk) // 2
    dv = hv // h
    rank = w_a2.shape[1]
    off = row0 // seq_len
    has_s0 = state is not None
    wk, wv = hp * dk, hp * dv
    in_specs = [
        pl.BlockSpec((seq_len, wk), lambda b, i: (b + off, i)),
        pl.BlockSpec((seq_len, wk), lambda b, i: (b + off, hk // wk + i)),
        pl.BlockSpec((seq_len, wv), lambda b, i: (b + off, 2 * hk // wv + i)),
        pl.BlockSpec((seq_len, wv), lambda b, i: (b + off, (2 * hk + hv) // wv + i)),
        pl.BlockSpec((2, seq_len, rank), lambda b, i: (0, b + off, 0)),
        pl.BlockSpec((2, rank, wk), lambda b, i: (0, 0, i)),
        pl.BlockSpec((2, 1, wk), lambda b, i: (0, 0, i)),
        pl.BlockSpec((1, dv), lambda b, i: (0, 0)),
    ]
    args = [proj, proj, proj, proj, low, w_a2, b_a.reshape(2, 1, hk), norm_g.reshape(1, dv)]
    if has_s0:
        in_specs.append(pl.BlockSpec((None, None, 2, hp, dk, dv),
                                     lambda b, i: (b, layer_j, 0, i, 0, 0)))
        args.append(state)
    out_specs = [pl.BlockSpec((seq_len, wv), lambda b, i: (b, i))]
    out_shape = [jax.ShapeDtypeStruct((batch * seq_len, hv), BF16)]
    aliases = {}
    out_state = None
    if isinstance(new_state, str):
        out_state = "first"
        out_specs.append(pl.BlockSpec((None, n_state_layers, 2, hp, dk, dv),
                                      lambda b, i: (b, 0, 0, i, 0, 0)))
        out_shape.append(jax.ShapeDtypeStruct((batch, n_state_layers, 2, h, dk, dv), F32))
    elif new_state is not None:
        out_state = "next"
        aliases = {len(args): 1}
        in_specs.append(pl.BlockSpec(memory_space=pl.ANY))
        args.append(new_state)
        out_specs.append(pl.BlockSpec((None, None, 2, hp, dk, dv),
                                      lambda b, i: (b, layer_j, 0, i, 0, 0)))
        out_shape.append(jax.ShapeDtypeStruct(new_state.shape, F32))
    return pl.pallas_call(
        functools.partial(_gla_kernel, seq_len=seq_len, hp=hp, has_s0=has_s0,
                          out_state=out_state, layer_j=layer_j),
        grid=(batch, h // hp),
        in_specs=in_specs,
        out_specs=out_specs,
        out_shape=out_shape,
        input_output_aliases=aliases,
        scratch_shapes=[
            pltpu.VMEM((2, seq_len, wk), F32),
            pltpu.VMEM((2, seq_len, wv), F32),
            pltpu.VMEM((2, hp, dv, dk), F32),
        ],
        compiler_params=_cparams(2),
        name="gla_scan",
    )(*args)


def _sgu_kernel(u_ref, v_ref, lg_ref, lb_ref, ws_ref, bs_ref, o_ref):
    v = v_ref[...].astype(F32)
    vn = _layer_norm(v, lg_ref[...], lb_ref[...]).astype(BF16)
    rows, half = vn.shape
    cg = half // SGU_GROUPS
    for r in range(rows // SGU_CHUNK):
        rs = slice(r * SGU_CHUNK, (r + 1) * SGU_CHUNK)
        for g in range(SGU_GROUPS):
            cs = slice(g * cg, (g + 1) * cg)
            s = jnp.dot(ws_ref[g].astype(BF16), vn[rs, cs], preferred_element_type=F32) + bs_ref[g]
            o_ref[rs, cs] = (u_ref[rs, cs].astype(F32) * s).astype(o_ref.dtype)


def _sgu_gate(z, ln_g, ln_b, w_s, b_s):
    t, ffn = z.shape
    half = ffn // 2
    rows = SGU_CHUNK
    return pl.pallas_call(
        _sgu_kernel,
        grid=(t // rows,),
        in_specs=[
            pl.BlockSpec((rows, half), lambda i: (i, 0)),
            pl.BlockSpec((rows, half), lambda i: (i, 1)),
            pl.BlockSpec((1, half), lambda i: (0, 0)),
            pl.BlockSpec((1, half), lambda i: (0, 0)),
            pl.BlockSpec((SGU_GROUPS, SGU_CHUNK, SGU_CHUNK), lambda i: (0, 0, 0)),
            pl.BlockSpec((SGU_GROUPS, SGU_CHUNK, 1), lambda i: (0, 0, 0)),
        ],
        out_specs=pl.BlockSpec((rows, half), lambda i: (i, 0)),
        out_shape=jax.ShapeDtypeStruct((t, half), BF16),
        compiler_params=_cparams(1),
        name="sgu_gate",
    )(z, z, ln_g.reshape(1, half), ln_b.reshape(1, half), w_s, b_s.reshape(SGU_GROUPS, SGU_CHUNK, 1))


def _router_kernel(x_ref, wrt_ref, br_ref, sel_ref, wd_ref, pos_ref, cnt_ref, carry):
    i = pl.program_id(0)

    @pl.when(i == 0)
    def _():
        carry[...] = jnp.zeros_like(carry)

    x = _unpack_pairs_bf16(x_ref[...])
    logits = lax.dot_general(wrt_ref[...].astype(BF16), x, (((1,), (1,)), ((), ())),
                             preferred_element_type=F32)
    n_exp, tm = logits.shape
    ng = N_GROUPS
    per = n_exp // ng
    scores = jax.nn.sigmoid(logits)
    selv = scores + br_ref[...]
    a = [selv[j * ng:(j + 1) * ng] for j in range(per)]
    sc = [scores[j * ng:(j + 1) * ng] for j in range(per)]

    t1 = a[0]
    t2 = jnp.full_like(t1, -jnp.inf)
    for j in range(1, per):
        t2 = jnp.maximum(t2, jnp.minimum(t1, a[j]))
        t1 = jnp.maximum(t1, a[j])
    gscore = t1 + t2

    gidx = lax.broadcasted_iota(jnp.int32, (ng, tm), 0)
    grank = jnp.zeros((ng, tm), jnp.int32)
    for gp in range(ng):
        other = gscore[gp:gp + 1, :]
        beats = (other > gscore) | ((other == gscore) & (gp < gidx))
        grank = grank + beats.astype(jnp.int32)
    gmask = grank < TOPK_GROUPS

    val = [jnp.where(gmask, a[j], -jnp.inf) for j in range(per)]
    ranks = [jnp.zeros((ng, tm), jnp.int32) for _ in range(per)]
    for gp in range(ng):
        lower_group = gp < gidx
        lower_or_same_group = gp <= gidx
        for jp in range(per):
            other = val[jp][gp:gp + 1, :]
            for j in range(per):
                tie = lower_or_same_group if jp < j else lower_group
                beats = (other > val[j]) | ((other == val[j]) & tie)
                ranks[j] = ranks[j] + beats.astype(jnp.int32)
    selm = [(ranks[j] < TOP_K) & gmask for j in range(per)]

    ssum = jnp.zeros((ng, tm), F32)
    for j in range(per):
        ssum = ssum + jnp.where(selm[j], sc[j], 0.0)
    ssum = jnp.sum(ssum, axis=0, keepdims=True)
    wd = jnp.concatenate(
        [jnp.where(selm[j], sc[j] / ssum * ROUTE_SCALE, 0.0) for j in range(per)], axis=0)
    selmat = jnp.concatenate([jnp.where(selm[j], 1.0, 0.0) for j in range(per)], axis=0)

    s_i = lax.broadcasted_iota(jnp.int32, (tm, tm), 0)
    t_i = lax.broadcasted_iota(jnp.int32, (tm, tm), 1)
    upper = jnp.where(s_i < t_i, 1.0, 0.0).astype(BF16)
    base = carry[...][:, 0:1]
    pos = jnp.dot(selmat.astype(BF16), upper, preferred_element_type=F32) + base
    total = base + jnp.sum(selmat, axis=1, keepdims=True)
    carry[...] = jnp.broadcast_to(total, carry.shape)

    sel_ref[...] = selmat
    wd_ref[...] = wd
    pos_ref[...] = pos
    cnt_ref[...] = jnp.broadcast_to(total, cnt_ref.shape)


def _router(hm, w_rt, b_r_col):
    t = hm.shape[0]
    n_exp, d = w_rt.shape
    tm = _row_tile(512, t)
    col = pl.BlockSpec((n_exp, tm), lambda i: (0, i))
    return pl.pallas_call(
        _router_kernel,
        grid=(t // tm,),
        in_specs=[
            pl.BlockSpec((tm, d // 2), lambda i: (i, 0)),
            pl.BlockSpec((n_exp, d), lambda i: (0, 0)),
            pl.BlockSpec((n_exp, 1), lambda i: (0, 0)),
        ],
        out_specs=[col, col, col, pl.BlockSpec((n_exp, LANES), lambda i: (0, 0))],
        out_shape=[
            jax.ShapeDtypeStruct((n_exp, t), F32),
            jax.ShapeDtypeStruct((n_exp, t), F32),
            jax.ShapeDtypeStruct((n_exp, t), F32),
            jax.ShapeDtypeStruct((n_exp, LANES), F32),
        ],
        scratch_shapes=[pltpu.VMEM((n_exp, LANES), F32)],
        compiler_params=_cparams(1),
        name="router",
    )(hm, w_rt, b_r_col)


def _slots_kernel(sel_ref, wd_ref, pos_ref, ps_ref, d8_ref, w8_ref):
    sel = sel_ref[...]
    n_exp = sel.shape[0]
    r_i = lax.broadcasted_iota(jnp.int32, (n_exp, n_exp), 0)
    c_i = lax.broadcasted_iota(jnp.int32, (n_exp, n_exp), 1)
    lower = jnp.where(c_i < r_i, 1.0, 0.0).astype(BF16)
    slot = jnp.dot(lower, sel.astype(BF16), preferred_element_type=F32)
    chosen = sel > 0.5
    dest = pos_ref[...] + ps_ref[...]
    wd = wd_ref[...]
    d_rows, w_rows = [], []
    for k in range(TOP_K):
        pick = chosen & (slot == k)
        d_rows.append(jnp.sum(jnp.where(pick, dest, 0.0), axis=0, keepdims=True))
        w_rows.append(jnp.sum(jnp.where(pick, wd, 0.0), axis=0, keepdims=True))
    d8_ref[...] = jnp.concatenate(d_rows, axis=0).astype(jnp.int32)
    w8_ref[...] = jnp.concatenate(w_rows, axis=0)


def _slots(sel, wd, pos, pstart_col):
    n_exp, t = sel.shape
    tm = _row_tile(512, t)
    col = pl.BlockSpec((n_exp, tm), lambda i: (0, i))
    out = pl.BlockSpec((TOP_K, tm), lambda i: (0, i))
    return pl.pallas_call(
        _slots_kernel,
        grid=(t // tm,),
        in_specs=[col, col, col, pl.BlockSpec((n_exp, 1), lambda i: (0, 0))],
        out_specs=[out, out],
        out_shape=[jax.ShapeDtypeStruct((TOP_K, t), jnp.int32),
                   jax.ShapeDtypeStruct((TOP_K, t), F32)],
        compiler_params=_cparams(1),
        name="moe_slots",
    )(sel, wd, pos, pstart_col)


def _swiglu(x, wgu, wd, ff):
    h = jnp.dot(x, wgu, preferred_element_type=F32)
    act = (_silu(h[:, :ff]) * h[:, ff:]).astype(BF16)
    return jnp.dot(act, wd, preferred_element_type=F32)


def _expert_kernel(be_ref, nu_ref, src_cur, src_next, dst_prev, dst_cur, hm_hbm, wgu_ref, wd_ref,
                   y_hbm, xbuf0, xbuf1, ybuf0, ybuf1, zrow, wgu_bf, wd_bf, gsem, ssem, *, ff):
    b = pl.program_id(0)
    nu = nu_ref[0]
    xbufs, ybufs = (xbuf0, xbuf1), (ybuf0, ybuf1)
    rows = MOE_BLOCK

    def gather(ids, par):
        for i in range(rows):
            pltpu.make_async_copy(hm_hbm.at[pl.ds(ids[0, i], 1), :],
                                  xbufs[par].at[pl.ds(i, 1), :], gsem.at[par]).start()

    def scatter(ids, par):
        for i in range(rows):
            pltpu.make_async_copy(ybufs[par].at[pl.ds(i, 1), :],
                                  y_hbm.at[pl.ds(ids[0, i], 1), :], ssem.at[par]).start()

    def wait_gather(par):
        pltpu.make_async_copy(hm_hbm.at[pl.ds(0, rows), :], xbufs[par], gsem.at[par]).wait()

    def wait_scatter(par):
        pltpu.make_async_copy(ybufs[par], y_hbm.at[pl.ds(0, rows), :], ssem.at[par]).wait()

    def block_step(par, first):
        if first:
            gather(src_cur, par)
        wait_gather(par)

        @pl.when((b == 0) | (be_ref[b] != be_ref[jnp.maximum(b - 1, 0)]))
        def _():
            wgu_bf[...] = wgu_ref[...].astype(BF16)
            wd_bf[...] = wd_ref[...].astype(BF16)

        gather(src_next, 1 - par)
        if not first:
            scatter(dst_prev, 1 - par)
        y = _swiglu(_unpack_pairs_bf16(xbufs[par][...]), wgu_bf[...], wd_bf[...], ff)
        if not first:
            @pl.when(b >= 2)
            def _():
                wait_scatter(par)
        ybufs[par][...] = _pack_pairs(y)

    @pl.when(b == 0)
    def _():
        block_step(0, True)

    for par in range(2):
        @pl.when((b > 0) & (b < nu) & (b % 2 == par))
        def _(par=par):
            block_step(par, False)

        @pl.when((b == nu) & (b % 2 == par))
        def _(par=par):
            wait_gather(par)
            scatter(dst_prev, 1 - par)

            @pl.when(b >= 2)
            def _():
                wait_scatter(par)

            wait_scatter(1 - par)

    @pl.when(b >= nu)
    def _():
        zrow[...] = jnp.zeros_like(zrow)

        def fill(i, carry):
            pltpu.make_async_copy(zrow.at[pl.ds(0, 1), :],
                                  y_hbm.at[pl.ds(dst_cur[0, i], 1), :], ssem.at[0]).start()
            return carry

        lax.fori_loop(0, rows, fill, 0)
        wait_scatter(0)


def _experts(hm, src_ids, dst_ids, block_e, n_used, w_gu, w_down, layer):
    dh = hm.shape[1]
    d = 2 * dh
    ff = w_down.shape[2]
    n_blocks = src_ids.shape[0]
    cap = n_blocks * MOE_BLOCK

    def blk(b, be, nu):
        return jnp.minimum(b, nu[0] - 1)

    def ids_spec(index):
        return pl.BlockSpec((None, 1, MOE_BLOCK), lambda b, be, nu: (index(b), 0, 0),
                            memory_space=pltpu.SMEM)

    row_buf = pltpu.VMEM((MOE_BLOCK, dh), jnp.uint32)
    return pl.pallas_call(
        functools.partial(_expert_kernel, ff=ff),
        grid_spec=pltpu.PrefetchScalarGridSpec(
            num_scalar_prefetch=2,
            grid=(n_blocks,),
            in_specs=[
                ids_spec(lambda b: b),
                ids_spec(lambda b: jnp.minimum(b + 1, n_blocks - 1)),
                ids_spec(lambda b: jnp.maximum(b - 1, 0)),
                ids_spec(lambda b: b),
                pl.BlockSpec(memory_space=pl.ANY),
                pl.BlockSpec((None, None, d, 2 * ff),
                             lambda b, be, nu: (layer, be[blk(b, be, nu)], 0, 0)),
                pl.BlockSpec((None, None, ff, d),
                             lambda b, be, nu: (layer, be[blk(b, be, nu)], 0, 0)),
            ],
            out_specs=pl.BlockSpec(memory_space=pl.ANY),
            scratch_shapes=[row_buf, row_buf, row_buf, row_buf,
                            pltpu.VMEM((SUBLANES, dh), jnp.uint32),
                            pltpu.VMEM((d, 2 * ff), BF16), pltpu.VMEM((ff, d), BF16),
                            pltpu.SemaphoreType.DMA((2,)), pltpu.SemaphoreType.DMA((2,))],
        ),
        out_shape=jax.ShapeDtypeStruct((cap, dh), jnp.uint32),
        compiler_params=_cparams(1),
        name="moe_experts",
    )(block_e, n_used, src_ids, src_ids, dst_ids, dst_ids, hm, w_gu, w_down)


def _shared_kernel(x_ref, wgu_ref, wd_ref, o_ref, wgu_bf, wd_bf, *, ff):
    @pl.when(pl.program_id(0) == 0)
    def _():
        wgu_bf[...] = wgu_ref[...].astype(BF16)
        wd_bf[...] = wd_ref[...].astype(BF16)

    o_ref[...] = _swiglu(_unpack_pairs_bf16(x_ref[...]), wgu_bf[...], wd_bf[...], ff)


def _shared_expert(hm, ws_gu, ws_down, layer):
    t, dh = hm.shape
    d = 2 * dh
    ff = ws_down.shape[1]
    tm = _row_tile(512, t)
    return pl.pallas_call(
        functools.partial(_shared_kernel, ff=ff),
        grid=(t // tm,),
        in_specs=[
            pl.BlockSpec((tm, dh), lambda i: (i, 0)),
            pl.BlockSpec((None, d, 2 * ff), lambda i: (layer, 0, 0)),
            pl.BlockSpec((None, ff, d), lambda i: (layer, 0, 0)),
        ],
        out_specs=pl.BlockSpec((tm, d), lambda i: (i, 0)),
        out_shape=jax.ShapeDtypeStruct((t, d), F32),
        scratch_shapes=[pltpu.VMEM((d, 2 * ff), BF16), pltpu.VMEM((ff, d), BF16)],
        compiler_params=_cparams(1),
        name="moe_shared",
    )(hm, ws_gu, ws_down)


def _combine_kernel(*refs, alpha, n_p, with_h):
    yk_refs = refs[:TOP_K]
    w8_ref, ysh_ref, x_ref, gate_ref, g_ref, b_ref = refs[TOP_K:TOP_K + 6]
    rest = refs[TOP_K + 6:]
    w8 = w8_ref[...]
    half = yk_refs[0].shape[-1]
    acc_lo = ysh_ref[:, :half]
    acc_hi = ysh_ref[:, half:]
    for kk in range(TOP_K):
        rows = yk_refs[kk][...]
        wk = w8[:, kk:kk + 1]
        acc_lo = acc_lo + wk * _unpack_lo(rows)
        acc_hi = acc_hi + wk * _unpack_hi(rows)
    acc = jnp.concatenate([acc_lo, acc_hi], axis=-1)
    y = alpha * x_ref[...] + gate_ref[...] * acc
    xn = _layer_norm(y, g_ref[...], b_ref[...])
    if with_h:
        sh_ref, sc_ref, xo_ref, ho_ref = rest
        xo_ref[...] = xn
        _store_h(ho_ref, xn * (1.0 + sc_ref[...]) + sh_ref[...])
    else:
        xp_ref, xs_ref = rest
        i = pl.program_id(0)

        @pl.when(i < n_p)
        def _():
            xp_ref[...] = xn

        @pl.when(i >= n_p)
        def _():
            xs_ref[...] = xn


def _combine(yk, w8, y_sh, x, gate, ln_g, ln_b, shift, scale, alpha, tp, seq_s):
    t, d = x.shape
    tm = _row_tile(256, tp, seq_s)
    n_steps = t // tm
    n_p = tp // tm
    with_h = shift is not None
    cid = _cond_index(tm, tp, seq_s)
    row = pl.BlockSpec((tm, d), lambda i: (i, 0))
    mod_spec = pl.BlockSpec((None, 1, d), lambda i: (cid(i), 0, 0))
    vec = pl.BlockSpec((1, d), lambda i: (0, 0))
    in_specs = [pl.BlockSpec((tm, d // 2), lambda i, kk=kk: (kk * n_steps + i, 0))
                for kk in range(TOP_K)]
    in_specs += [pl.BlockSpec((tm, TOP_K), lambda i: (i, 0)), row, row, mod_spec, vec, vec]
    args = [yk] * TOP_K + [w8, y_sh, x, gate, ln_g.reshape(1, d), ln_b.reshape(1, d)]
    if with_h:
        in_specs += [mod_spec, mod_spec]
        args += [shift, scale]
        out_specs = [row, row]
        out_shape = [jax.ShapeDtypeStruct((t, d), F32), jax.ShapeDtypeStruct((t, d), BF16)]
    else:
        out_specs = [pl.BlockSpec((tm, d), lambda i: (jnp.minimum(i, n_p - 1), 0)),
                     pl.BlockSpec((tm, d), lambda i: (jnp.maximum(i - n_p, 0), 0))]
        out_shape = [jax.ShapeDtypeStruct((tp, d), F32), jax.ShapeDtypeStruct((t - tp, d), F32)]
    return pl.pallas_call(
        functools.partial(_combine_kernel, alpha=alpha, n_p=n_p, with_h=with_h),
        grid=(n_steps,),
        in_specs=in_specs,
        out_specs=out_specs,
        out_shape=out_shape,
        compiler_params=_cparams(1),
        name="moe_combine",
    )(*args)


def _expert_row_order(n_exp):
    per = n_exp // N_GROUPS
    r = jnp.arange(n_exp)
    return (r % N_GROUPS) * per + r // N_GROUPS


def _count_le(edges, v):
    return jnp.sum((edges[None, :] <= v[:, None]).astype(jnp.int32), axis=1)


def _moe_layer(x, hm, layer, w_r, b_r, w_gu, w_down, ws_gu, ws_down, gate, ln_g, ln_b, shift,
               scale, alpha, tp, seq_s):
    t = hm.shape[0]
    n_exp = w_r.shape[1]
    order = _expert_row_order(n_exp)
    sel, wd, pos, cnt = _router(hm, w_r.T[order], b_r[order].reshape(n_exp, 1))

    assert (t * TOP_K) % MOE_BLOCK == 0
    n_blocks = t * TOP_K // MOE_BLOCK + n_exp
    cap = n_blocks * MOE_BLOCK
    counts = cnt[:, 0].astype(jnp.int32)
    padded = (counts + MOE_BLOCK - 1) // MOE_BLOCK * MOE_BLOCK
    pend = jnp.cumsum(padded)
    pstart = pend - padded
    n_used = (pend[-1:] // MOE_BLOCK).astype(jnp.int32)
    block_row = jnp.minimum(_count_le(pend, jnp.arange(n_blocks) * MOE_BLOCK), n_exp - 1)
    per = n_exp // N_GROUPS
    block_e = ((block_row % N_GROUPS) * per + block_row // N_GROUPS).astype(jnp.int32)
    dest8, w8 = _slots(sel, wd, pos, pstart.astype(F32).reshape(n_exp, 1))

    padlen = padded - counts
    cpad_start = jnp.cumsum(padlen) - padlen
    f = jnp.arange(cap - t * TOP_K)
    fill = f + jnp.sum(jnp.where(cpad_start[None, :] <= f[:, None], counts[None, :], 0), axis=1)
    position = jnp.concatenate([dest8.reshape(-1), fill.astype(jnp.int32)])
    assign = jnp.argsort(position).astype(jnp.int32)
    src_ids = (assign % t).reshape(n_blocks, 1, MOE_BLOCK)
    dst_ids = assign.reshape(n_blocks, 1, MOE_BLOCK)

    yk = _experts(hm, src_ids, dst_ids, block_e, n_used, w_gu, w_down, layer)
    y_sh = _shared_expert(hm, ws_gu, ws_down, layer)
    return _combine(yk, w8.T, y_sh, x, gate, ln_g, ln_b, shift, scale, alpha, tp, seq_s)


def _gla_layer(h, state_gla, new_state, layer_j, w_in, w_a1, w_a2, b_a, norm_g, w_out, dims):
    bp, sp, bs, ss = dims
    tp = bp * sp
    d = h.shape[1]
    rank = w_a1.shape[3]
    proj = _matmul(h, w_in, layer_j, BF16)
    w_low = jnp.transpose(w_a1[layer_j], (1, 0, 2)).reshape(1, d, 2 * rank)
    low = _matmul(h, w_low, 0, F32)
    low = jnp.transpose(low.reshape(-1, 2, rank), (1, 0, 2))
    og_p, st = _gla_scan(proj, low, w_a2[layer_j], b_a[layer_j], norm_g[layer_j], None, layer_j,
                         batch=bp, seq_len=sp, row0=0, hp=GLA_HEADS,
                         new_state="first" if new_state is None else new_state,
                         n_state_layers=w_in.shape[0])
    (og_s,) = _gla_scan(proj, low, w_a2[layer_j], b_a[layer_j], norm_g[layer_j], state_gla,
                        layer_j, batch=bs, seq_len=ss, row0=tp, hp=GLA_HEADS // 2)
    out = _matmul(jnp.concatenate([og_p, og_s], axis=0), w_out, layer_j, BF16)
    return out, st


def _sgu_layer(h, layer_j, w_in, ln_g, ln_b, w_s, b_s, w_out):
    z = _matmul(h, w_in, layer_j, BF16, act="gelu")
    gated = _sgu_gate(z, ln_g[layer_j], ln_b[layer_j], w_s[layer_j], b_s[layer_j])
    return _matmul(gated, w_out, layer_j, BF16)


def _grid_pos_embed(rows, d):
    t = jnp.arange(rows * GRID_W)
    r = (t // GRID_W).astype(F32)
    col = (t % GRID_W).astype(F32)
    quarter = d // 4
    omega = 1.0 / (10000.0 ** (jnp.arange(quarter, dtype=F32) / quarter))

    def emb(p):
        a = p[:, None] * omega[None, :]
        return jnp.concatenate([jnp.sin(a), jnp.cos(a)], axis=-1)

    return jnp.concatenate([emb(r), emb(col)], axis=-1)


def kernel(x_prompt, x_sample, state_gla, c, c_ctx, w_ada, b_ada, ln_g, ln_b, gla_w_in, gla_w_a1, gla_w_a2, gla_b_a, gla_norm_g, gla_w_out, sgu_w_in, sgu_ln_g, sgu_ln_b, sgu_w_s, sgu_b_s, sgu_w_out, moe_w_router, moe_b_router, moe_w_gu, moe_w_down, moe_ws_gu, moe_ws_down):
    bp, sp, d = x_prompt.shape
    bs, ss, _ = x_sample.shape
    assert 1 + bs <= COND_ROWS
    tp, ts = bp * sp, bs * ss
    depth = w_ada.shape[0]
    alpha = (2 * depth) ** 0.25
    dims = (bp, sp, bs, ss)

    cond = jnp.zeros((COND_ROWS, d), F32).at[0].set(c_ctx).at[1:1 + bs].set(c)
    mods = _ada_mods(cond, w_ada, b_ada).reshape(depth, COND_ROWS, 6, 1, d)

    def mod(l, i):
        return mods[l, :, i]

    pos = _grid_pos_embed(ss // GRID_W, d).astype(x_sample.dtype)
    x, h = _prep(x_prompt.reshape(tp, d), x_sample.reshape(ts, d), pos, mod(0, 0), mod(0, 1), ss)

    new_state = None
    for l in range(depth):
        j = l // 2
        if l % 2 == 0:
            out, new_state = _gla_layer(h, state_gla, new_state, j, gla_w_in, gla_w_a1, gla_w_a2,
                                        gla_b_a, gla_norm_g, gla_w_out, dims)
        else:
            out = _sgu_layer(h, j, sgu_w_in, sgu_ln_g, sgu_ln_b, sgu_w_s, sgu_b_s, sgu_w_out)
        x, hm = _post_norm(x, out, mod(l, 2), ln_g[l, 0], ln_b[l, 0], mod(l, 3), mod(l, 4),
                           alpha, tp, ss)
        last = l == depth - 1
        res = _moe_layer(x, hm, l, moe_w_router[l], moe_b_router[l], moe_w_gu, moe_w_down,
                         moe_ws_gu, moe_ws_down, mod(l, 5), ln_g[l, 1], ln_b[l, 1],
                         None if last else mod(l + 1, 0), None if last else mod(l + 1, 1),
                         alpha, tp, ss)
        if last:
            y_prompt, y_sample = res
        else:
            x, h = res

    return y_prompt.reshape(bp, sp, d), y_sample.reshape(bs, ss, d), new_state
```

```python
import functools

import jax
import jax.numpy as jnp
from jax import lax
from jax.experimental import pallas as pl
from jax.experimental.pallas import tpu as pltpu

F32 = jnp.float32
BF16 = jnp.bfloat16

V7X_VMEM_LIMIT_BYTES = 56 * 1024 * 1024
LANES = 128
SUBLANES = 8

GRID_W = 64
GLA_HEADS = 4
GLA_CHUNK = 64
GLA_NORMALIZER = 16.0
SGU_GROUPS = 4
SGU_CHUNK = 128
TOP_K = 8
N_GROUPS = 8
TOPK_GROUPS = 4
ROUTE_SCALE = 2.5
LN_EPS = 1e-5
RMS_EPS = 1e-6
MOE_BLOCK = 256
MOE_DMA_PRIORITIES = (("p0", "p0"), ("p0", "alt"), ("alt", "alt"), ("p0", "p1"))
COND_ROWS = 8


def _cparams(n_axes):
    return pltpu.CompilerParams(
        dimension_semantics=("arbitrary",) * n_axes,
        vmem_limit_bytes=V7X_VMEM_LIMIT_BYTES,
    )


def _silu(x):
    return x * jax.nn.sigmoid(x)


def _layer_norm(y, g, b):
    mu = jnp.mean(y, axis=-1, keepdims=True)
    d = y - mu
    var = jnp.mean(d * d, axis=-1, keepdims=True)
    return d * lax.rsqrt(var + LN_EPS) * g + b


_HI16 = 0xFFFF0000


def _pack_pairs(x):
    half = x.shape[-1] // 2
    lo = lax.bitcast_convert_type(x[:, :half].astype(BF16).astype(F32), jnp.uint32)
    hi = lax.bitcast_convert_type(x[:, half:].astype(BF16).astype(F32), jnp.uint32)
    return lax.shift_right_logical(lo, jnp.uint32(16)) | (hi & jnp.uint32(_HI16))


def _unpack_lo(w):
    return lax.bitcast_convert_type(lax.shift_left(w, jnp.uint32(16)), F32)


def _unpack_hi(w):
    return lax.bitcast_convert_type(w & jnp.uint32(_HI16), F32)


def _unpack_pairs_bf16(w):
    return jnp.concatenate([_unpack_lo(w).astype(BF16), _unpack_hi(w).astype(BF16)], axis=-1)


def _row_tile(limit, *sizes):
    tm = limit
    while any(s % tm for s in sizes):
        tm //= 2
    return tm


def _cond_index(tm, tp, seq_s):
    def index(i):
        return jnp.where(i * tm < tp, 0, 1 + (i * tm - tp) // seq_s)

    return index


def _ada_kernel(c_ref, w_ref, b_ref, o_ref):
    s = _silu(c_ref[...]).astype(BF16)
    o_ref[...] = jnp.dot(s, w_ref[...].astype(BF16), preferred_element_type=F32) + b_ref[...]


def _ada_mods(cond, w_ada, b_ada):
    depth, d, n = w_ada.shape
    tn = _row_tile(1024, n)
    return pl.pallas_call(
        _ada_kernel,
        grid=(depth, n // tn),
        in_specs=[
            pl.BlockSpec((COND_ROWS, d), lambda l, j: (0, 0)),
            pl.BlockSpec((None, d, tn), lambda l, j: (l, 0, j)),
            pl.BlockSpec((None, 1, tn), lambda l, j: (l, 0, j)),
        ],
        out_specs=pl.BlockSpec((None, COND_ROWS, tn), lambda l, j: (l, 0, j)),
        out_shape=jax.ShapeDtypeStruct((depth, COND_ROWS, n), F32),
        compiler_params=_cparams(2),
        name="ada_mods",
    )(cond, w_ada, b_ada.reshape(depth, 1, n))


def _prep_kernel(xp_ref, xs_ref, pos_ref, sh_ref, sc_ref, x_ref, h_ref, *, n_p):
    x = jnp.where(pl.program_id(0) < n_p, xp_ref[...], xs_ref[...] + pos_ref[...])
    x_ref[...] = x
    h_ref[...] = (x * (1.0 + sc_ref[...]) + sh_ref[...]).astype(h_ref.dtype)


def _prep(xp, xs, pos, shift, scale, seq_s):
    tp, d = xp.shape
    ts = xs.shape[0]
    tm = _row_tile(512, tp, seq_s)
    n_p = tp // tm
    n_pos = seq_s // tm
    cid = _cond_index(tm, tp, seq_s)
    mod_spec = pl.BlockSpec((None, 1, d), lambda i: (cid(i), 0, 0))
    return pl.pallas_call(
        functools.partial(_prep_kernel, n_p=n_p),
        grid=((tp + ts) // tm,),
        in_specs=[
            pl.BlockSpec((tm, d), lambda i: (jnp.minimum(i, n_p - 1), 0)),
            pl.BlockSpec((tm, d), lambda i: (jnp.maximum(i - n_p, 0), 0)),
            pl.BlockSpec((tm, d), lambda i: (jnp.maximum(i - n_p, 0) % n_pos, 0)),
            mod_spec,
            mod_spec,
        ],
        out_specs=[pl.BlockSpec((tm, d), lambda i: (i, 0))] * 2,
        out_shape=[
            jax.ShapeDtypeStruct((tp + ts, d), F32),
            jax.ShapeDtypeStruct((tp + ts, d), BF16),
        ],
        compiler_params=_cparams(1),
        name="prep",
    )(xp, xs, pos, shift, scale)


def _mm_kernel(x_ref, w_ref, o_ref, wbf_ref, *, act):
    @pl.when(pl.program_id(1) == 0)
    def _():
        wbf_ref[...] = w_ref[...].astype(BF16)

    acc = jnp.dot(x_ref[...].astype(BF16), wbf_ref[...], preferred_element_type=F32)
    if act == "gelu":
        acc = 0.5 * acc * (1.0 + lax.erf(acc * 0.7071067811865476))
    o_ref[...] = acc.astype(o_ref.dtype)


def _matmul(x, w, layer, out_dtype, act=None):
    m, k = x.shape
    n = w.shape[2]
    tn_limit = 1024
    while k * tn_limit * 10 > V7X_VMEM_LIMIT_BYTES * 4 // 7 and tn_limit > LANES:
        tn_limit //= 2
    tn = _row_tile(tn_limit, n)
    out_bytes = jnp.dtype(out_dtype).itemsize
    tm_limit = 1024
    while (k * tn * 10 + tm_limit * (2 * k * x.dtype.itemsize + tn * (2 * out_bytes + 4))
           > V7X_VMEM_LIMIT_BYTES * 7 // 8 and tm_limit > SUBLANES):
        tm_limit //= 2
    tm = _row_tile(tm_limit, m)
    return pl.pallas_call(
        functools.partial(_mm_kernel, act=act),
        grid=(n // tn, m // tm),
        in_specs=[
            pl.BlockSpec((tm, k), lambda j, i: (i, 0)),
            pl.BlockSpec((None, k, tn), lambda j, i: (layer, 0, j)),
        ],
        out_specs=pl.BlockSpec((tm, tn), lambda j, i: (i, j)),
        out_shape=jax.ShapeDtypeStruct((m, n), out_dtype),
        scratch_shapes=[pltpu.VMEM((k, tn), BF16)],
        compiler_params=_cparams(2),
        name="matmul_" + (act or "lin"),
    )(x, w)


def _post_norm_kernel(x_ref, o_ref, gate_ref, g_ref, b_ref, *rest, alpha, with_h):
    y = alpha * x_ref[...] + gate_ref[...] * o_ref[...].astype(F32)
    xn = _layer_norm(y, g_ref[...], b_ref[...])
    if with_h:
        sh_ref, sc_ref, xo_ref, ho_ref = rest
        _store_h(ho_ref, xn * (1.0 + sc_ref[...]) + sh_ref[...])
    else:
        (xo_ref,) = rest
    xo_ref[...] = xn


def _store_h(ho_ref, h):
    ho_ref[...] = _pack_pairs(h) if ho_ref.dtype == jnp.uint32 else h.astype(ho_ref.dtype)


def _post_norm(x, out, gate, ln_g, ln_b, shift, scale, alpha, tp, seq_s):
    t, d = x.shape
    tm = _row_tile(512, tp, seq_s)
    cid = _cond_index(tm, tp, seq_s)
    row = pl.BlockSpec((tm, d), lambda i: (i, 0))
    mod_spec = pl.BlockSpec((None, 1, d), lambda i: (cid(i), 0, 0))
    vec = pl.BlockSpec((1, d), lambda i: (0, 0))
    return pl.pallas_call(
        functools.partial(_post_norm_kernel, alpha=alpha, with_h=True),
        grid=(t // tm,),
        in_specs=[row, row, mod_spec, vec, vec, mod_spec, mod_spec],
        out_specs=[row, pl.BlockSpec((tm, d // 2), lambda i: (i, 0))],
        out_shape=[jax.ShapeDtypeStruct((t, d), F32),
                   jax.ShapeDtypeStruct((t, d // 2), jnp.uint32)],
        compiler_params=_cparams(1),
        name="post_norm",
    )(x, out, gate, ln_g.reshape(1, d), ln_b.reshape(1, d), shift, scale)


def _prefix_sum_mm(tri, x):
    hi = x.astype(BF16)
    r1 = x - hi.astype(F32)
    mid = r1.astype(BF16)
    lo = (r1 - mid.astype(F32)).astype(BF16)
    out = jnp.dot(tri, hi, preferred_element_type=F32)
    out = out + jnp.dot(tri, mid, preferred_element_type=F32)
    return out + jnp.dot(tri, lo, preferred_element_type=F32)


def _gla_kernel(*refs, seq_len, hp, has_s0, out_state, layer_j):
    q_ref, k_ref, v_ref, g_ref, low_ref, wa2_ref, ba_ref, ng_ref = refs[:8]
    rest = list(refs[8:])
    s0_ref = rest.pop(0) if has_s0 else None
    if out_state == "next":
        rest.pop(0)
    o_ref = rest.pop(0)
    st_ref = rest.pop(0) if out_state else None
    gk_scr, oacc, s_t = rest

    c = GLA_CHUNK
    n = seq_len // c
    dk = q_ref.shape[1] // hp
    dv = v_ref.shape[1] // hp
    scale = dk ** -0.5

    for z in range(2):
        pre = jnp.dot(low_ref[z].astype(BF16), wa2_ref[z].astype(BF16),
                      preferred_element_type=F32) + ba_ref[z]
        log_sig = jnp.minimum(pre, 0.0) - jnp.log1p(jnp.exp(-jnp.abs(pre)))
        gk_scr[z] = log_sig * (1.0 / GLA_NORMALIZER)
        for hh in range(hp):
            s_t[z, hh] = s0_ref[z, hh].T if has_s0 else jnp.zeros((dv, dk), F32)

    row = lax.broadcasted_iota(jnp.int32, (c, c), 0)
    col = lax.broadcasted_iota(jnp.int32, (c, c), 1)
    masks = (col <= row, col >= row)
    tris = tuple(jnp.where(m, 1.0, 0.0).astype(BF16) for m in masks)
    nt_dims = (((1,), (1,)), ((), ()))
    tn_dims = (((0,), (0,)), ((), ()))

    n_stages = 5

    def chunk_stages(ci, z, hh):
        rows = pl.ds(pl.multiple_of(ci * c, c), c)
        ck = slice(hh * dk, (hh + 1) * dk)
        cv = slice(hh * dv, (hh + 1) * dv)
        gcum = _prefix_sum_mm(tris[z], gk_scr[z, rows, ck])
        yield
        qc = q_ref[rows, ck].astype(F32) * scale
        kc = k_ref[rows, ck].astype(F32)
        if z == 0:
            g_mid = gcum[c // 2 - 1:c // 2]
            g_last = gcum[c - 1:c]
        else:
            g_mid = gcum[c // 2:c // 2 + 1]
            g_last = gcum[0:1]
        q_in = (qc * jnp.exp(gcum - g_mid)).astype(BF16)
        k_in = (kc * jnp.exp(g_mid - gcum)).astype(BF16)
        a = lax.dot_general(q_in, k_in, nt_dims, preferred_element_type=F32)
        yield
        vc = v_ref[rows, cv].astype(BF16)
        a = jnp.where(masks[z], a, 0.0).astype(BF16)
        o = jnp.dot(a, vc, preferred_element_type=F32)
        yield
        s_prev = s_t[z, hh]
        q_dec = (qc * jnp.exp(gcum)).astype(BF16)
        o = o + lax.dot_general(q_dec, s_prev.astype(BF16), nt_dims, preferred_element_type=F32)
        oacc[z, rows, cv] = o
        yield
        k_dec = (kc * jnp.exp(g_last - gcum)).astype(BF16)
        u_t = lax.dot_general(vc, k_dec, tn_dims, preferred_element_type=F32)
        s_t[z, hh] = s_prev * jnp.exp(g_last) + u_t
        yield

    def body(i, carry):
        scans = [chunk_stages(i if z == 0 else n - 1 - i, z, hh)
                 for z in range(2) for hh in range(hp)]
        for _ in range(n_stages):
            for scan in scans:
                next(scan)
        return carry

    lax.fori_loop(0, n, body, 0)

    for hh in range(hp):
        cv = slice(hh * dv, (hh + 1) * dv)
        for z in range(2):
            if out_state == "next":
                st_ref[z, hh] = s_t[z, hh].T
            elif out_state == "first":
                for j in range(st_ref.shape[0]):
                    st_ref[j, z, hh] = s_t[z, hh].T if j == layer_j else jnp.zeros((dk, dv), F32)
        o = oacc[0, :, cv] + oacc[1, :, cv]
        on = o * lax.rsqrt(jnp.mean(o * o, axis=-1, keepdims=True) + RMS_EPS) * ng_ref[...]
        o_ref[:, cv] = (on * _silu(g_ref[:, cv].astype(F32))).astype(o_ref.dtype)


def _gla_scan(proj, low, w_a2, b_a, norm_g, state, layer_j, *, batch, seq_len, row0, hp,
              new_state=None, n_state_layers=1):
    h = GLA_HEADS
    hk = w_a2.shape[2]
    dk = hk // h
    hv = (proj.shape[1] - 2 * hk) // 2
    dv = hv // h
    rank = w_a2.shape[1]
    off = row0 // seq_len
    has_s0 = state is not None
    wk, wv = hp * dk, hp * dv
    in_specs = [
        pl.BlockSpec((seq_len, wk), lambda b, i: (b + off, i)),
        pl.BlockSpec((seq_len, wk), lambda b, i: (b + off, hk // wk + i)),
        pl.BlockSpec((seq_len, wv), lambda b, i: (b + off, 2 * hk // wv + i)),
        pl.BlockSpec((seq_len, wv), lambda b, i: (b + off, (2 * hk + hv) // wv + i)),
        pl.BlockSpec((2, seq_len, rank), lambda b, i: (0, b + off, 0)),
        pl.BlockSpec((2, rank, wk), lambda b, i: (0, 0, i)),
        pl.BlockSpec((2, 1, wk), lambda b, i: (0, 0, i)),
        pl.BlockSpec((1, dv), lambda b, i: (0, 0)),
    ]
    args = [proj, proj, proj, proj, low, w_a2, b_a.reshape(2, 1, hk), norm_g.reshape(1, dv)]
    if has_s0:
        in_specs.append(pl.BlockSpec((None, None, 2, hp, dk, dv),
                                     lambda b, i: (b, layer_j, 0, i, 0, 0)))
        args.append(state)
    out_specs = [pl.BlockSpec((seq_len, wv), lambda b, i: (b, i))]
    out_shape = [jax.ShapeDtypeStruct((batch * seq_len, hv), BF16)]
    aliases = {}
    out_state = None
    if isinstance(new_state, str):
        out_state = "first"
        out_specs.append(pl.BlockSpec((None, n_state_layers, 2, hp, dk, dv),
                                      lambda b, i: (b, 0, 0, i, 0, 0)))
        out_shape.append(jax.ShapeDtypeStruct((batch, n_state_layers, 2, h, dk, dv), F32))
    elif new_state is not None:
        out_state = "next"
        aliases = {len(args): 1}
        in_specs.append(pl.BlockSpec(memory_space=pl.ANY))
        args.append(new_state)
        out_specs.append(pl.BlockSpec((None, None, 2, hp, dk, dv),
                                      lambda b, i: (b, layer_j, 0, i, 0, 0)))
        out_shape.append(jax.ShapeDtypeStruct(new_state.shape, F32))
    return pl.pallas_call(
        functools.partial(_gla_kernel, seq_len=seq_len, hp=hp, has_s0=has_s0,
                          out_state=out_state, layer_j=layer_j),
        grid=(batch, h // hp),
        in_specs=in_specs,
        out_specs=out_specs,
        out_shape=out_shape,
        input_output_aliases=aliases,
        scratch_shapes=[
            pltpu.VMEM((2, seq_len, wk), F32),
            pltpu.VMEM((2, seq_len, wv), F32),
            pltpu.VMEM((2, hp, dv, dk), F32),
        ],
        compiler_params=_cparams(2),
        name="gla_scan",
    )(*args)


def _sgu_kernel(u_ref, v_ref, lg_ref, lb_ref, ws_ref, bs_ref, o_ref):
    v = v_ref[...].astype(F32)
    vn = _layer_norm(v, lg_ref[...], lb_ref[...]).astype(BF16)
    rows, half = vn.shape
    cg = half // SGU_GROUPS
    for r in range(rows // SGU_CHUNK):
        rs = slice(r * SGU_CHUNK, (r + 1) * SGU_CHUNK)
        for g in range(SGU_GROUPS):
            cs = slice(g * cg, (g + 1) * cg)
            s = jnp.dot(ws_ref[g].astype(BF16), vn[rs, cs], preferred_element_type=F32) + bs_ref[g]
            o_ref[rs, cs] = (u_ref[rs, cs].astype(F32) * s).astype(o_ref.dtype)


def _sgu_gate(z, ln_g, ln_b, w_s, b_s):
    t, ffn = z.shape
    half = ffn // 2
    rows = SGU_CHUNK
    return pl.pallas_call(
        _sgu_kernel,
        grid=(t // rows,),
        in_specs=[
            pl.BlockSpec((rows, half), lambda i: (i, 0)),
            pl.BlockSpec((rows, half), lambda i: (i, 1)),
            pl.BlockSpec((1, half), lambda i: (0, 0)),
            pl.BlockSpec((1, half), lambda i: (0, 0)),
            pl.BlockSpec((SGU_GROUPS, SGU_CHUNK, SGU_CHUNK), lambda i: (0, 0, 0)),
            pl.BlockSpec((SGU_GROUPS, SGU_CHUNK, 1), lambda i: (0, 0, 0)),
        ],
        out_specs=pl.BlockSpec((rows, half), lambda i: (i, 0)),
        out_shape=jax.ShapeDtypeStruct((t, half), BF16),
        compiler_params=_cparams(1),
        name="sgu_gate",
    )(z, z, ln_g.reshape(1, half), ln_b.reshape(1, half), w_s, b_s.reshape(SGU_GROUPS, SGU_CHUNK, 1))


def _router_kernel(x_ref, wrt_ref, br_ref, sel_ref, wd_ref, pos_ref, cnt_ref, carry):
    i = pl.program_id(0)

    @pl.when(i == 0)
    def _():
        carry[...] = jnp.zeros_like(carry)

    x = _unpack_pairs_bf16(x_ref[...])
    logits = lax.dot_general(wrt_ref[...].astype(BF16), x, (((1,), (1,)), ((), ())),
                             preferred_element_type=F32)
    n_exp, tm = logits.shape
    ng = N_GROUPS
    per = n_exp // ng
    scores = jax.nn.sigmoid(logits)
    selv = scores + br_ref[...]
    a = [selv[j * ng:(j + 1) * ng] for j in range(per)]
    sc = [scores[j * ng:(j + 1) * ng] for j in range(per)]

    t1 = a[0]
    t2 = jnp.full_like(t1, -jnp.inf)
    for j in range(1, per):
        t2 = jnp.maximum(t2, jnp.minimum(t1, a[j]))
        t1 = jnp.maximum(t1, a[j])
    gscore = t1 + t2

    gidx = lax.broadcasted_iota(jnp.int32, (ng, tm), 0)
    grank = jnp.zeros((ng, tm), jnp.int32)
    for gp in range(ng):
        other = gscore[gp:gp + 1, :]
        beats = (other > gscore) | ((other == gscore) & (gp < gidx))
        grank = grank + beats.astype(jnp.int32)
    gmask = grank < TOPK_GROUPS

    val = [jnp.where(gmask, a[j], -jnp.inf) for j in range(per)]
    ranks = [jnp.zeros((ng, tm), jnp.int32) for _ in range(per)]
    for gp in range(ng):
        lower_group = gp < gidx
        lower_or_same_group = gp <= gidx
        for jp in range(per):
            other = val[jp][gp:gp + 1, :]
            for j in range(per):
                tie = lower_or_same_group if jp < j else lower_group
                beats = (other > val[j]) | ((other == val[j]) & tie)
                ranks[j] = ranks[j] + beats.astype(jnp.int32)
    selm = [(ranks[j] < TOP_K) & gmask for j in range(per)]

    ssum = jnp.zeros((ng, tm), F32)
    for j in range(per):
        ssum = ssum + jnp.where(selm[j], sc[j], 0.0)
    ssum = jnp.sum(ssum, axis=0, keepdims=True)
    wd = jnp.concatenate(
        [jnp.where(selm[j], sc[j] / ssum * ROUTE_SCALE, 0.0) for j in range(per)], axis=0)
    selmat = jnp.concatenate([jnp.where(selm[j], 1.0, 0.0) for j in range(per)], axis=0)

    s_i = lax.broadcasted_iota(jnp.int32, (tm, tm), 0)
    t_i = lax.broadcasted_iota(jnp.int32, (tm, tm), 1)
    upper = jnp.where(s_i < t_i, 1.0, 0.0).astype(BF16)
    base = carry[...][:, 0:1]
    pos = jnp.dot(selmat.astype(BF16), upper, preferred_element_type=F32) + base
    total = base + jnp.sum(selmat, axis=1, keepdims=True)
    carry[...] = jnp.broadcast_to(total, carry.shape)

    sel_ref[...] = selmat
    wd_ref[...] = wd
    pos_ref[...] = pos
    cnt_ref[...] = jnp.broadcast_to(total, cnt_ref.shape)


def _router(hm, w_rt, b_r_col):
    t = hm.shape[0]
    n_exp, d = w_rt.shape
    tm = _row_tile(512, t)
    col = pl.BlockSpec((n_exp, tm), lambda i: (0, i))
    return pl.pallas_call(
        _router_kernel,
        grid=(t // tm,),
        in_specs=[
            pl.BlockSpec((tm, d // 2), lambda i: (i, 0)),
            pl.BlockSpec((n_exp, d), lambda i: (0, 0)),
            pl.BlockSpec((n_exp, 1), lambda i: (0, 0)),
        ],
        out_specs=[col, col, col, pl.BlockSpec((n_exp, LANES), lambda i: (0, 0))],
        out_shape=[
            jax.ShapeDtypeStruct((n_exp, t), F32),
            jax.ShapeDtypeStruct((n_exp, t), F32),
            jax.ShapeDtypeStruct((n_exp, t), F32),
            jax.ShapeDtypeStruct((n_exp, LANES), F32),
        ],
        scratch_shapes=[pltpu.VMEM((n_exp, LANES), F32)],
        compiler_params=_cparams(1),
        name="router",
    )(hm, w_rt, b_r_col)


def _slots_kernel(sel_ref, wd_ref, pos_ref, ps_ref, d8_ref, w8_ref):
    sel = sel_ref[...]
    n_exp = sel.shape[0]
    r_i = lax.broadcasted_iota(jnp.int32, (n_exp, n_exp), 0)
    c_i = lax.broadcasted_iota(jnp.int32, (n_exp, n_exp), 1)
    lower = jnp.where(c_i < r_i, 1.0, 0.0).astype(BF16)
    slot = jnp.dot(lower, sel.astype(BF16), preferred_element_type=F32)
    chosen = sel > 0.5
    dest = pos_ref[...] + ps_ref[...]
    wd = wd_ref[...]
    d_rows, w_rows = [], []
    for k in range(TOP_K):
        pick = chosen & (slot == k)
        d_rows.append(jnp.sum(jnp.where(pick, dest, 0.0), axis=0, keepdims=True))
        w_rows.append(jnp.sum(jnp.where(pick, wd, 0.0), axis=0, keepdims=True))
    d8_ref[...] = jnp.concatenate(d_rows, axis=0).astype(jnp.int32)
    w8_ref[...] = jnp.concatenate(w_rows, axis=0)


def _slots(sel, wd, pos, pstart_col):
    n_exp, t = sel.shape
    tm = _row_tile(512, t)
    col = pl.BlockSpec((n_exp, tm), lambda i: (0, i))
    out = pl.BlockSpec((TOP_K, tm), lambda i: (0, i))
    return pl.pallas_call(
        _slots_kernel,
        grid=(t // tm,),
        in_specs=[col, col, col, pl.BlockSpec((n_exp, 1), lambda i: (0, 0))],
        out_specs=[out, out],
        out_shape=[jax.ShapeDtypeStruct((TOP_K, t), jnp.int32),
                   jax.ShapeDtypeStruct((TOP_K, t), F32)],
        compiler_params=_cparams(1),
        name="moe_slots",
    )(sel, wd, pos, pstart_col)


def _swiglu(x, wgu, wd, ff):
    h = jnp.dot(x, wgu, preferred_element_type=F32)
    act = (_silu(h[:, :ff]) * h[:, ff:]).astype(BF16)
    return jnp.dot(act, wd, preferred_element_type=F32)


def _dma_priority(mode, i):
    return {"p0": 0, "p1": 1, "alt": i % 2}[mode]


def _expert_kernel(be_ref, nu_ref, src_cur, src_next, dst_prev, hm_hbm, wgu_ref, wd_ref,
                   y_hbm, xbuf0, xbuf1, ybuf0, ybuf1, wgu_bf, wd_bf, gsem, ssem, *, ff, n_real,
                   gather_prio, scatter_prio):
    b = pl.program_id(0)
    nu = nu_ref[0]
    xbufs, ybufs = (xbuf0, xbuf1), (ybuf0, ybuf1)
    rows = MOE_BLOCK

    def gather(ids, par):
        for i in range(rows):
            pltpu.make_async_copy(hm_hbm.at[pl.ds(ids[0, i], 1), :],
                                  xbufs[par].at[pl.ds(i, 1), :],
                                  gsem.at[par]).start(priority=_dma_priority(gather_prio, i))

    def scatter(ids, par):
        for i in range(rows):
            pltpu.make_async_copy(ybufs[par].at[pl.ds(i, 1), :],
                                  y_hbm.at[pl.ds(ids[0, i], 1), :],
                                  ssem.at[par]).start(priority=_dma_priority(scatter_prio, i))

    def wait_gather(par):
        pltpu.make_async_copy(hm_hbm.at[pl.ds(0, rows), :], xbufs[par], gsem.at[par]).wait()

    def wait_scatter(par):
        pltpu.make_async_copy(ybufs[par], y_hbm.at[pl.ds(0, rows), :], ssem.at[par]).wait()

    def block_step(par, first):
        if first:
            for k in range(2):
                ybufs[k][...] = jnp.zeros_like(ybufs[k])
                spare = y_hbm.at[pl.ds(n_real + k * rows, rows), :]
                pltpu.make_async_copy(ybufs[k], spare, ssem.at[k]).start()
            for k in range(2):
                wait_scatter(k)
            gather(src_cur, par)
        wait_gather(par)

        @pl.when((b == 0) | (be_ref[b] != be_ref[jnp.maximum(b - 1, 0)]))
        def _():
            wgu_bf[...] = wgu_ref[...].astype(BF16)
            wd_bf[...] = wd_ref[...].astype(BF16)

        gather(src_next, 1 - par)
        if not first:
            scatter(dst_prev, 1 - par)
        y = _swiglu(_unpack_pairs_bf16(xbufs[par][...]), wgu_bf[...], wd_bf[...], ff)
        if not first:
            @pl.when(b >= 2)
            def _():
                wait_scatter(par)
        ybufs[par][...] = _pack_pairs(y)

    @pl.when(b == 0)
    def _():
        block_step(0, True)

    for par in range(2):
        @pl.when((b > 0) & (b < nu) & (b % 2 == par))
        def _(par=par):
            block_step(par, False)

        @pl.when((b == nu) & (b % 2 == par))
        def _(par=par):
            wait_gather(par)
            scatter(dst_prev, 1 - par)

            @pl.when(b >= 2)
            def _():
                wait_scatter(par)

            wait_scatter(1 - par)


def _experts(hm, src_ids, dst_ids, block_e, n_used, w_gu, w_down, layer, n_real, gather_prio,
             scatter_prio):
    dh = hm.shape[1]
    d = 2 * dh
    ff = w_down.shape[2]
    n_blocks = src_ids.shape[0]

    def blk(b, be, nu):
        return jnp.minimum(b, nu[0] - 1)

    def ids_spec(index):
        return pl.BlockSpec((None, 1, MOE_BLOCK), lambda b, be, nu: (index(b), 0, 0),
                            memory_space=pltpu.SMEM)

    row_buf = pltpu.VMEM((MOE_BLOCK, dh), jnp.uint32)
    return pl.pallas_call(
        functools.partial(_expert_kernel, ff=ff, n_real=n_real, gather_prio=gather_prio,
                          scatter_prio=scatter_prio),
        grid_spec=pltpu.PrefetchScalarGridSpec(
            num_scalar_prefetch=2,
            grid=(n_blocks,),
            in_specs=[
                ids_spec(lambda b: b),
                ids_spec(lambda b: jnp.minimum(b + 1, n_blocks - 1)),
                ids_spec(lambda b: jnp.maximum(b - 1, 0)),
                pl.BlockSpec(memory_space=pl.ANY),
                pl.BlockSpec((None, None, d, 2 * ff),
                             lambda b, be, nu: (layer, be[blk(b, be, nu)], 0, 0)),
                pl.BlockSpec((None, None, ff, d),
                             lambda b, be, nu: (layer, be[blk(b, be, nu)], 0, 0)),
            ],
            out_specs=pl.BlockSpec(memory_space=pl.ANY),
            scratch_shapes=[row_buf, row_buf, row_buf, row_buf,
                            pltpu.VMEM((d, 2 * ff), BF16), pltpu.VMEM((ff, d), BF16),
                            pltpu.SemaphoreType.DMA((2,)), pltpu.SemaphoreType.DMA((2,))],
        ),
        out_shape=jax.ShapeDtypeStruct((n_real + 2 * MOE_BLOCK, dh), jnp.uint32),
        compiler_params=_cparams(1),
        name="moe_experts",
    )(block_e, n_used, src_ids, src_ids, dst_ids, hm, w_gu, w_down)


def _shared_kernel(x_ref, wgu_ref, wd_ref, o_ref, wgu_bf, wd_bf, *, ff):
    @pl.when(pl.program_id(0) == 0)
    def _():
        wgu_bf[...] = wgu_ref[...].astype(BF16)
        wd_bf[...] = wd_ref[...].astype(BF16)

    o_ref[...] = _swiglu(_unpack_pairs_bf16(x_ref[...]), wgu_bf[...], wd_bf[...], ff)


def _shared_expert(hm, ws_gu, ws_down, layer):
    t, dh = hm.shape
    d = 2 * dh
    ff = ws_down.shape[1]
    tm = _row_tile(512, t)
    return pl.pallas_call(
        functools.partial(_shared_kernel, ff=ff),
        grid=(t // tm,),
        in_specs=[
            pl.BlockSpec((tm, dh), lambda i: (i, 0)),
            pl.BlockSpec((None, d, 2 * ff), lambda i: (layer, 0, 0)),
            pl.BlockSpec((None, ff, d), lambda i: (layer, 0, 0)),
        ],
        out_specs=pl.BlockSpec((tm, d), lambda i: (i, 0)),
        out_shape=jax.ShapeDtypeStruct((t, d), F32),
        scratch_shapes=[pltpu.VMEM((d, 2 * ff), BF16), pltpu.VMEM((ff, d), BF16)],
        compiler_params=_cparams(1),
        name="moe_shared",
    )(hm, ws_gu, ws_down)


def _combine_kernel(*refs, alpha, n_p, with_h):
    yk_refs = refs[:TOP_K]
    w8_ref, ysh_ref, x_ref, gate_ref, g_ref, b_ref = refs[TOP_K:TOP_K + 6]
    rest = refs[TOP_K + 6:]
    w8 = w8_ref[...]
    half = yk_refs[0].shape[-1]
    acc_lo = ysh_ref[:, :half]
    acc_hi = ysh_ref[:, half:]
    for kk in range(TOP_K):
        rows = yk_refs[kk][...]
        wk = w8[:, kk:kk + 1]
        acc_lo = acc_lo + wk * _unpack_lo(rows)
        acc_hi = acc_hi + wk * _unpack_hi(rows)
    acc = jnp.concatenate([acc_lo, acc_hi], axis=-1)
    y = alpha * x_ref[...] + gate_ref[...] * acc
    xn = _layer_norm(y, g_ref[...], b_ref[...])
    if with_h:
        sh_ref, sc_ref, xo_ref, ho_ref = rest
        xo_ref[...] = xn
        _store_h(ho_ref, xn * (1.0 + sc_ref[...]) + sh_ref[...])
    else:
        xp_ref, xs_ref = rest
        i = pl.program_id(0)

        @pl.when(i < n_p)
        def _():
            xp_ref[...] = xn

        @pl.when(i >= n_p)
        def _():
            xs_ref[...] = xn


def _combine(yk, w8, y_sh, x, gate, ln_g, ln_b, shift, scale, alpha, tp, seq_s):
    t, d = x.shape
    tm = _row_tile(256, tp, seq_s)
    n_steps = t // tm
    n_p = tp // tm
    with_h = shift is not None
    cid = _cond_index(tm, tp, seq_s)
    row = pl.BlockSpec((tm, d), lambda i: (i, 0))
    mod_spec = pl.BlockSpec((None, 1, d), lambda i: (cid(i), 0, 0))
    vec = pl.BlockSpec((1, d), lambda i: (0, 0))
    in_specs = [pl.BlockSpec((tm, d // 2), lambda i, kk=kk: (kk * n_steps + i, 0))
                for kk in range(TOP_K)]
    in_specs += [pl.BlockSpec((tm, TOP_K), lambda i: (i, 0)), row, row, mod_spec, vec, vec]
    args = [yk] * TOP_K + [w8, y_sh, x, gate, ln_g.reshape(1, d), ln_b.reshape(1, d)]
    if with_h:
        in_specs += [mod_spec, mod_spec]
        args += [shift, scale]
        out_specs = [row, row]
        out_shape = [jax.ShapeDtypeStruct((t, d), F32), jax.ShapeDtypeStruct((t, d), BF16)]
    else:
        out_specs = [pl.BlockSpec((tm, d), lambda i: (jnp.minimum(i, n_p - 1), 0)),
                     pl.BlockSpec((tm, d), lambda i: (jnp.maximum(i - n_p, 0), 0))]
        out_shape = [jax.ShapeDtypeStruct((tp, d), F32), jax.ShapeDtypeStruct((t - tp, d), F32)]
    return pl.pallas_call(
        functools.partial(_combine_kernel, alpha=alpha, n_p=n_p, with_h=with_h),
        grid=(n_steps,),
        in_specs=in_specs,
        out_specs=out_specs,
        out_shape=out_shape,
        compiler_params=_cparams(1),
        name="moe_combine",
    )(*args)


def _expert_row_order(n_exp):
    per = n_exp // N_GROUPS
    r = jnp.arange(n_exp)
    return (r % N_GROUPS) * per + r // N_GROUPS


def _count_le(edges, v):
    return jnp.sum((edges[None, :] <= v[:, None]).astype(jnp.int32), axis=1)


def _moe_layer(x, hm, layer, w_r, b_r, w_gu, w_down, ws_gu, ws_down, gate, ln_g, ln_b, shift,
               scale, alpha, tp, seq_s):
    t = hm.shape[0]
    n_exp = w_r.shape[1]
    order = _expert_row_order(n_exp)
    sel, wd, pos, cnt = _router(hm, w_r.T[order], b_r[order].reshape(n_exp, 1))

    assert (t * TOP_K) % MOE_BLOCK == 0
    n_blocks = t * TOP_K // MOE_BLOCK + n_exp
    cap = n_blocks * MOE_BLOCK
    counts = cnt[:, 0].astype(jnp.int32)
    padded = (counts + MOE_BLOCK - 1) // MOE_BLOCK * MOE_BLOCK
    pend = jnp.cumsum(padded)
    pstart = pend - padded
    n_used = (pend[-1:] // MOE_BLOCK).astype(jnp.int32)
    block_row = jnp.minimum(_count_le(pend, jnp.arange(n_blocks) * MOE_BLOCK), n_exp - 1)
    per = n_exp // N_GROUPS
    block_e = ((block_row % N_GROUPS) * per + block_row // N_GROUPS).astype(jnp.int32)
    dest8, w8 = _slots(sel, wd, pos, pstart.astype(F32).reshape(n_exp, 1))

    padlen = padded - counts
    cpad_start = jnp.cumsum(padlen) - padlen
    f = jnp.arange(cap - t * TOP_K)
    fill = f + jnp.sum(jnp.where(cpad_start[None, :] <= f[:, None], counts[None, :], 0), axis=1)
    position = jnp.concatenate([dest8.reshape(-1), fill.astype(jnp.int32)])
    assign = jnp.argsort(position).astype(jnp.int32)
    src_ids = (assign % t).reshape(n_blocks, 1, MOE_BLOCK)
    n_real = t * TOP_K
    spare = n_real + jnp.arange(cap, dtype=jnp.int32) % (2 * MOE_BLOCK)
    dst_ids = jnp.where(assign >= n_real, spare, assign).reshape(n_blocks, 1, MOE_BLOCK)

    gather_prio, scatter_prio = MOE_DMA_PRIORITIES[layer % len(MOE_DMA_PRIORITIES)]
    yk = _experts(hm, src_ids, dst_ids, block_e, n_used, w_gu, w_down, layer, n_real,
                  gather_prio, scatter_prio)
    y_sh = _shared_expert(hm, ws_gu, ws_down, layer)
    return _combine(yk, w8.T, y_sh, x, gate, ln_g, ln_b, shift, scale, alpha, tp, seq_s)


def _gla_layer(h, state_gla, new_state, layer_j, w_in, w_a1, w_a2, b_a, norm_g, w_out, dims):
    bp, sp, bs, ss = dims
    tp = bp * sp
    d = h.shape[1]
    rank = w_a1.shape[3]
    proj = _matmul(h, w_in, layer_j, BF16)
    w_low = jnp.transpose(w_a1[layer_j], (1, 0, 2)).reshape(1, d, 2 * rank)
    low = _matmul(h, w_low, 0, F32)
    low = jnp.transpose(low.reshape(-1, 2, rank), (1, 0, 2))
    og_p, st = _gla_scan(proj, low, w_a2[layer_j], b_a[layer_j], norm_g[layer_j], None, layer_j,
                         batch=bp, seq_len=sp, row0=0, hp=GLA_HEADS,
                         new_state="first" if new_state is None else new_state,
                         n_state_layers=w_in.shape[0])
    (og_s,) = _gla_scan(proj, low, w_a2[layer_j], b_a[layer_j], norm_g[layer_j], state_gla,
                        layer_j, batch=bs, seq_len=ss, row0=tp, hp=GLA_HEADS // 2)
    out = _matmul(jnp.concatenate([og_p, og_s], axis=0), w_out, layer_j, BF16)
    return out, st


def _sgu_layer(h, layer_j, w_in, ln_g, ln_b, w_s, b_s, w_out):
    z = _matmul(h, w_in, layer_j, BF16, act="gelu")
    gated = _sgu_gate(z, ln_g[layer_j], ln_b[layer_j], w_s[layer_j], b_s[layer_j])
    return _matmul(gated, w_out, layer_j, BF16)


def _grid_pos_embed(rows, d):
    t = jnp.arange(rows * GRID_W)
    r = (t // GRID_W).astype(F32)
    col = (t % GRID_W).astype(F32)
    quarter = d // 4
    omega = 1.0 / (10000.0 ** (jnp.arange(quarter, dtype=F32) / quarter))

    def emb(p):
        a = p[:, None] * omega[None, :]
        return jnp.concatenate([jnp.sin(a), jnp.cos(a)], axis=-1)

    return jnp.concatenate([emb(r), emb(col)], axis=-1)


def kernel(x_prompt, x_sample, state_gla, c, c_ctx, w_ada, b_ada, ln_g, ln_b, gla_w_in, gla_w_a1, gla_w_a2, gla_b_a, gla_norm_g, gla_w_out, sgu_w_in, sgu_ln_g, sgu_ln_b, sgu_w_s, sgu_b_s, sgu_w_out, moe_w_router, moe_b_router, moe_w_gu, moe_w_down, moe_ws_gu, moe_ws_down):
    bp, sp, d = x_prompt.shape
    bs, ss, _ = x_sample.shape
    assert 1 + bs <= COND_ROWS
    tp, ts = bp * sp, bs * ss
    depth = w_ada.shape[0]
    alpha = (2 * depth) ** 0.25
    dims = (bp, sp, bs, ss)

    cond = jnp.zeros((COND_ROWS, d), F32).at[0].set(c_ctx).at[1:1 + bs].set(c)
    mods = _ada_mods(cond, w_ada, b_ada).reshape(depth, COND_ROWS, 6, 1, d)

    def mod(l, i):
        return mods[l, :, i]

    pos = _grid_pos_embed(ss // GRID_W, d).astype(x_sample.dtype)
    x, h = _prep(x_prompt.reshape(tp, d), x_sample.reshape(ts, d), pos, mod(0, 0), mod(0, 1), ss)

    new_state = None
    for l in range(depth):
        j = l // 2
        if l % 2 == 0:
            out, new_state = _gla_layer(h, state_gla, new_state, j, gla_w_in, gla_w_a1, gla_w_a2,
                                        gla_b_a, gla_norm_g, gla_w_out, dims)
        else:
            out = _sgu_layer(h, j, sgu_w_in, sgu_ln_g, sgu_ln_b, sgu_w_s, sgu_b_s, sgu_w_out)
        x, hm = _post_norm(x, out, mod(l, 2), ln_g[l, 0], ln_b[l, 0], mod(l, 3), mod(l, 4),
                           alpha, tp, ss)
        last = l == depth - 1
        res = _moe_layer(x, hm, l, moe_w_router[l], moe_b_router[l], moe_w_gu, moe_w_down,
                         moe_ws_gu, moe_ws_down, mod(l, 5), ln_g[l, 1], ln_b[l, 1],
                         None if last else mod(l + 1, 0), None if last else mod(l + 1, 1),
                         alpha, tp, ss)
        if last:
            y_prompt, y_sample = res
        else:
            x, h = res

    return y_prompt.reshape(bp, sp, d), y_sample.reshape(bs, ss, d), new_state
```

```python
import functools

import jax
import jax.numpy as jnp
from jax import lax
from jax.experimental import pallas as pl
from jax.experimental.pallas import tpu as pltpu

F32 = jnp.float32
BF16 = jnp.bfloat16

V7X_VMEM_LIMIT_BYTES = 56 * 1024 * 1024
LANES = 128
SUBLANES = 8

GRID_W = 64
GLA_HEADS = 4
GLA_CHUNK = 64
GLA_NORMALIZER = 16.0
SGU_GROUPS = 4
SGU_CHUNK = 128
TOP_K = 8
N_GROUPS = 8
TOPK_GROUPS = 4
ROUTE_SCALE = 2.5
LN_EPS = 1e-5
RMS_EPS = 1e-6
MOE_BLOCK = 256
COND_ROWS = 8


def _cparams(n_axes):
    return pltpu.CompilerParams(
        dimension_semantics=("arbitrary",) * n_axes,
        vmem_limit_bytes=V7X_VMEM_LIMIT_BYTES,
    )


def _silu(x):
    return x * jax.nn.sigmoid(x)


def _layer_norm(y, g, b):
    mu = jnp.mean(y, axis=-1, keepdims=True)
    d = y - mu
    var = jnp.mean(d * d, axis=-1, keepdims=True)
    return d * lax.rsqrt(var + LN_EPS) * g + b


_HI16 = 0xFFFF0000


def _pack_pairs(x):
    half = x.shape[-1] // 2
    lo = lax.bitcast_convert_type(x[:, :half].astype(BF16).astype(F32), jnp.uint32)
    hi = lax.bitcast_convert_type(x[:, half:].astype(BF16).astype(F32), jnp.uint32)
    return lax.shift_right_logical(lo, jnp.uint32(16)) | (hi & jnp.uint32(_HI16))


def _unpack_lo(w):
    return lax.bitcast_convert_type(lax.shift_left(w, jnp.uint32(16)), F32)


def _unpack_hi(w):
    return lax.bitcast_convert_type(w & jnp.uint32(_HI16), F32)


def _unpack_pairs_bf16(w):
    return jnp.concatenate([_unpack_lo(w).astype(BF16), _unpack_hi(w).astype(BF16)], axis=-1)


def _row_tile(limit, *sizes):
    tm = limit
    while any(s % tm for s in sizes):
        tm //= 2
    return tm


def _cond_index(tm, tp, seq_s):
    def index(i):
        return jnp.where(i * tm < tp, 0, 1 + (i * tm - tp) // seq_s)

    return index


def _ada_kernel(c_ref, w_ref, b_ref, o_ref):
    s = _silu(c_ref[...]).astype(BF16)
    o_ref[...] = jnp.dot(s, w_ref[...].astype(BF16), preferred_element_type=F32) + b_ref[...]


def _ada_mods(cond, w_ada, b_ada):
    depth, d, n = w_ada.shape
    tn = _row_tile(1024, n)
    return pl.pallas_call(
        _ada_kernel,
        grid=(depth, n // tn),
        in_specs=[
            pl.BlockSpec((COND_ROWS, d), lambda l, j: (0, 0)),
            pl.BlockSpec((None, d, tn), lambda l, j: (l, 0, j)),
            pl.BlockSpec((None, 1, tn), lambda l, j: (l, 0, j)),
        ],
        out_specs=pl.BlockSpec((None, COND_ROWS, tn), lambda l, j: (l, 0, j)),
        out_shape=jax.ShapeDtypeStruct((depth, COND_ROWS, n), F32),
        compiler_params=_cparams(2),
        name="ada_mods",
    )(cond, w_ada, b_ada.reshape(depth, 1, n))


def _prep_kernel(xp_ref, xs_ref, pos_ref, sh_ref, sc_ref, x_ref, h_ref, *, n_p):
    x = jnp.where(pl.program_id(0) < n_p, xp_ref[...], xs_ref[...] + pos_ref[...])
    x_ref[...] = x
    h_ref[...] = (x * (1.0 + sc_ref[...]) + sh_ref[...]).astype(h_ref.dtype)


def _prep(xp, xs, pos, shift, scale, seq_s):
    tp, d = xp.shape
    ts = xs.shape[0]
    tm = _row_tile(512, tp, seq_s)
    n_p = tp // tm
    n_pos = seq_s // tm
    cid = _cond_index(tm, tp, seq_s)
    mod_spec = pl.BlockSpec((None, 1, d), lambda i: (cid(i), 0, 0))
    return pl.pallas_call(
        functools.partial(_prep_kernel, n_p=n_p),
        grid=((tp + ts) // tm,),
        in_specs=[
            pl.BlockSpec((tm, d), lambda i: (jnp.minimum(i, n_p - 1), 0)),
            pl.BlockSpec((tm, d), lambda i: (jnp.maximum(i - n_p, 0), 0)),
            pl.BlockSpec((tm, d), lambda i: (jnp.maximum(i - n_p, 0) % n_pos, 0)),
            mod_spec,
            mod_spec,
        ],
        out_specs=[pl.BlockSpec((tm, d), lambda i: (i, 0))] * 2,
        out_shape=[
            jax.ShapeDtypeStruct((tp + ts, d), F32),
            jax.ShapeDtypeStruct((tp + ts, d), BF16),
        ],
        compiler_params=_cparams(1),
        name="prep",
    )(xp, xs, pos, shift, scale)


def _mm_kernel(x_ref, w_ref, o_ref, wbf_ref, *, act):
    @pl.when(pl.program_id(1) == 0)
    def _():
        wbf_ref[...] = w_ref[...].astype(BF16)

    acc = jnp.dot(x_ref[...].astype(BF16), wbf_ref[...], preferred_element_type=F32)
    if act == "gelu":
        acc = 0.5 * acc * (1.0 + lax.erf(acc * 0.7071067811865476))
    o_ref[...] = acc.astype(o_ref.dtype)


def _matmul(x, w, layer, out_dtype, act=None):
    m, k = x.shape
    n = w.shape[2]
    tn_limit = 1024
    while k * tn_limit * 10 > V7X_VMEM_LIMIT_BYTES * 4 // 7 and tn_limit > LANES:
        tn_limit //= 2
    tn = _row_tile(tn_limit, n)
    out_bytes = jnp.dtype(out_dtype).itemsize
    tm_limit = 1024
    while (k * tn * 10 + tm_limit * (2 * k * x.dtype.itemsize + tn * (2 * out_bytes + 4))
           > V7X_VMEM_LIMIT_BYTES * 7 // 8 and tm_limit > SUBLANES):
        tm_limit //= 2
    tm = _row_tile(tm_limit, m)
    return pl.pallas_call(
        functools.partial(_mm_kernel, act=act),
        grid=(n // tn, m // tm),
        in_specs=[
            pl.BlockSpec((tm, k), lambda j, i: (i, 0)),
            pl.BlockSpec((None, k, tn), lambda j, i: (layer, 0, j)),
        ],
        out_specs=pl.BlockSpec((tm, tn), lambda j, i: (i, j)),
        out_shape=jax.ShapeDtypeStruct((m, n), out_dtype),
        scratch_shapes=[pltpu.VMEM((k, tn), BF16)],
        compiler_params=_cparams(2),
        name="matmul_" + (act or "lin"),
    )(x, w)


def _post_norm_kernel(x_ref, o_ref, gate_ref, g_ref, b_ref, *rest, alpha, with_h):
    y = alpha * x_ref[...] + gate_ref[...] * o_ref[...].astype(F32)
    xn = _layer_norm(y, g_ref[...], b_ref[...])
    if with_h:
        sh_ref, sc_ref, xo_ref, ho_ref = rest
        _store_h(ho_ref, xn * (1.0 + sc_ref[...]) + sh_ref[...])
    else:
        (xo_ref,) = rest
    xo_ref[...] = xn


def _store_h(ho_ref, h):
    ho_ref[...] = _pack_pairs(h) if ho_ref.dtype == jnp.uint32 else h.astype(ho_ref.dtype)


def _post_norm(x, out, gate, ln_g, ln_b, shift, scale, alpha, tp, seq_s):
    t, d = x.shape
    tm = _row_tile(512, tp, seq_s)
    cid = _cond_index(tm, tp, seq_s)
    row = pl.BlockSpec((tm, d), lambda i: (i, 0))
    mod_spec = pl.BlockSpec((None, 1, d), lambda i: (cid(i), 0, 0))
    vec = pl.BlockSpec((1, d), lambda i: (0, 0))
    return pl.pallas_call(
        functools.partial(_post_norm_kernel, alpha=alpha, with_h=True),
        grid=(t // tm,),
        in_specs=[row, row, mod_spec, vec, vec, mod_spec, mod_spec],
        out_specs=[row, pl.BlockSpec((tm, d // 2), lambda i: (i, 0))],
        out_shape=[jax.ShapeDtypeStruct((t, d), F32),
                   jax.ShapeDtypeStruct((t, d // 2), jnp.uint32)],
        compiler_params=_cparams(1),
        name="post_norm",
    )(x, out, gate, ln_g.reshape(1, d), ln_b.reshape(1, d), shift, scale)


def _prefix_sum_mm(tri, x):
    hi = x.astype(BF16)
    r1 = x - hi.astype(F32)
    mid = r1.astype(BF16)
    lo = (r1 - mid.astype(F32)).astype(BF16)
    out = jnp.dot(tri, hi, preferred_element_type=F32)
    out = out + jnp.dot(tri, mid, preferred_element_type=F32)
    return out + jnp.dot(tri, lo, preferred_element_type=F32)


def _gla_kernel(*refs, seq_len, hp, has_s0, out_state, layer_j):
    q_ref, k_ref, v_ref, g_ref, low_ref, wa2_ref, ba_ref, ng_ref = refs[:8]
    rest = list(refs[8:])
    s0_ref = rest.pop(0) if has_s0 else None
    if out_state == "next":
        rest.pop(0)
    o_ref = rest.pop(0)
    st_ref = rest.pop(0) if out_state else None
    gk_scr, oacc, s_t = rest

    c = GLA_CHUNK
    n = seq_len // c
    dk = q_ref.shape[1] // hp
    dv = v_ref.shape[1] // hp
    scale = dk ** -0.5

    for z in range(2):
        pre = jnp.dot(low_ref[z].astype(BF16), wa2_ref[z].astype(BF16),
                      preferred_element_type=F32) + ba_ref[z]
        log_sig = jnp.minimum(pre, 0.0) - jnp.log(1.0 + jnp.exp(-jnp.abs(pre)))
        gk_scr[z] = log_sig * (1.0 / GLA_NORMALIZER)
        for hh in range(hp):
            s_t[z, hh] = s0_ref[z, hh].T if has_s0 else jnp.zeros((dv, dk), F32)

    row = lax.broadcasted_iota(jnp.int32, (c, c), 0)
    col = lax.broadcasted_iota(jnp.int32, (c, c), 1)
    masks = (col <= row, col >= row)
    tris = tuple(jnp.where(m, 1.0, 0.0).astype(BF16) for m in masks)
    nt_dims = (((1,), (1,)), ((), ()))
    tn_dims = (((0,), (0,)), ((), ()))

    n_stages = 5

    def chunk_stages(ci, z, hh):
        rows = pl.ds(pl.multiple_of(ci * c, c), c)
        ck = slice(hh * dk, (hh + 1) * dk)
        cv = slice(hh * dv, (hh + 1) * dv)
        gcum = _prefix_sum_mm(tris[z], gk_scr[z, rows, ck])
        yield
        qc = q_ref[rows, ck].astype(F32) * scale
        kc = k_ref[rows, ck].astype(F32)
        if z == 0:
            g_mid = gcum[c // 2 - 1:c // 2]
            g_last = gcum[c - 1:c]
        else:
            g_mid = gcum[c // 2:c // 2 + 1]
            g_last = gcum[0:1]
        q_in = (qc * jnp.exp(gcum - g_mid)).astype(BF16)
        k_in = (kc * jnp.exp(g_mid - gcum)).astype(BF16)
        a = lax.dot_general(q_in, k_in, nt_dims, preferred_element_type=F32)
        yield
        vc = v_ref[rows, cv].astype(BF16)
        a = jnp.where(masks[z], a, 0.0).astype(BF16)
        o = jnp.dot(a, vc, preferred_element_type=F32)
        yield
        s_prev = s_t[z, hh]
        q_dec = (qc * jnp.exp(gcum)).astype(BF16)
        o = o + lax.dot_general(q_dec, s_prev.astype(BF16), nt_dims, preferred_element_type=F32)
        oacc[z, rows, cv] = o
        yield
        k_dec = (kc * jnp.exp(g_last - gcum)).astype(BF16)
        u_t = lax.dot_general(vc, k_dec, tn_dims, preferred_element_type=F32)
        s_t[z, hh] = s_prev * jnp.exp(g_last) + u_t
        yield

    def body(i, carry):
        scans = [chunk_stages(i if z == 0 else n - 1 - i, z, hh)
                 for z in range(2) for hh in range(hp)]
        for _ in range(n_stages):
            for scan in scans:
                next(scan)
        return carry

    lax.fori_loop(0, n, body, 0)

    for hh in range(hp):
        cv = slice(hh * dv, (hh + 1) * dv)
        for z in range(2):
            if out_state == "next":
                st_ref[z, hh] = s_t[z, hh].T
            elif out_state == "first":
                for j in range(st_ref.shape[0]):
                    st_ref[j, z, hh] = s_t[z, hh].T if j == layer_j else jnp.zeros((dk, dv), F32)
        o = oacc[0, :, cv] + oacc[1, :, cv]
        on = o * lax.rsqrt(jnp.mean(o * o, axis=-1, keepdims=True) + RMS_EPS) * ng_ref[...]
        o_ref[:, cv] = (on * _silu(g_ref[:, cv].astype(F32))).astype(o_ref.dtype)


def _gla_scan(proj, low, w_a2, b_a, norm_g, state, layer_j, *, batch, seq_len, row0, hp,
              new_state=None, n_state_layers=1):
    h = GLA_HEADS
    hk = w_a2.shape[2]
    dk = hk // h
    hv = (proj.shape[1] - 2 * hk) // 2
    dv = hv // h
    rank = w_a2.shape[1]
    off = row0 // seq_len
    has_s0 = state is not None
    wk, wv = hp * dk, hp * dv
    in_specs = [
        pl.BlockSpec((seq_len, wk), lambda b, i: (b + off, i)),
        pl.BlockSpec((seq_len, wk), lambda b, i: (b + off, hk // wk + i)),
        pl.BlockSpec((seq_len, wv), lambda b, i: (b + off, 2 * hk // wv + i)),
        pl.BlockSpec((seq_len, wv), lambda b, i: (b + off, (2 * hk + hv) // wv + i)),
        pl.BlockSpec((2, seq_len, rank), lambda b, i: (0, b + off, 0)),
        pl.BlockSpec((2, rank, wk), lambda b, i: (0, 0, i)),
        pl.BlockSpec((2, 1, wk), lambda b, i: (0, 0, i)),
        pl.BlockSpec((1, dv), lambda b, i: (0, 0)),
    ]
    args = [proj, proj, proj, proj, low, w_a2, b_a.reshape(2, 1, hk), norm_g.reshape(1, dv)]
    if has_s0:
        in_specs.append(pl.BlockSpec((None, None, 2, hp, dk, dv),
                                     lambda b, i: (b, layer_j, 0, i, 0, 0)))
        args.append(state)
    out_specs = [pl.BlockSpec((seq_len, wv), lambda b, i: (b, i))]
    out_shape = [jax.ShapeDtypeStruct((batch * seq_len, hv), BF16)]
    aliases = {}
    out_state = None
    if isinstance(new_state, str):
        out_state = "first"
        out_specs.append(pl.BlockSpec((None, n_state_layers, 2, hp, dk, dv),
                                      lambda b, i: (b, 0, 0, i, 0, 0)))
        out_shape.append(jax.ShapeDtypeStruct((batch, n_state_layers, 2, h, dk, dv), F32))
    elif new_state is not None:
        out_state = "next"
        aliases = {len(args): 1}
        in_specs.append(pl.BlockSpec(memory_space=pl.ANY))
        args.append(new_state)
        out_specs.append(pl.BlockSpec((None, None, 2, hp, dk, dv),
                                      lambda b, i: (b, layer_j, 0, i, 0, 0)))
        out_shape.append(jax.ShapeDtypeStruct(new_state.shape, F32))
    return pl.pallas_call(
        functools.partial(_gla_kernel, seq_len=seq_len, hp=hp, has_s0=has_s0,
                          out_state=out_state, layer_j=layer_j),
        grid=(batch, h // hp),
        in_specs=in_specs,
        out_specs=out_specs,
        out_shape=out_shape,
        input_output_aliases=aliases,
        scratch_shapes=[
            pltpu.VMEM((2, seq_len, wk), F32),
            pltpu.VMEM((2, seq_len, wv), F32),
            pltpu.VMEM((2, hp, dv, dk), F32),
        ],
        compiler_params=_cparams(2),
        name="gla_scan",
    )(*args)


def _sgu_kernel(u_ref, v_ref, lg_ref, lb_ref, ws_ref, bs_ref, o_ref):
    v = v_ref[...].astype(F32)
    vn = _layer_norm(v, lg_ref[...], lb_ref[...]).astype(BF16)
    rows, half = vn.shape
    cg = half // SGU_GROUPS
    for r in range(rows // SGU_CHUNK):
        rs = slice(r * SGU_CHUNK, (r + 1) * SGU_CHUNK)
        for g in range(SGU_GROUPS):
            cs = slice(g * cg, (g + 1) * cg)
            s = jnp.dot(ws_ref[g].astype(BF16), vn[rs, cs], preferred_element_type=F32) + bs_ref[g]
            o_ref[rs, cs] = (u_ref[rs, cs].astype(F32) * s).astype(o_ref.dtype)


def _sgu_gate(z, ln_g, ln_b, w_s, b_s):
    t, ffn = z.shape
    half = ffn // 2
    rows = SGU_CHUNK
    return pl.pallas_call(
        _sgu_kernel,
        grid=(t // rows,),
        in_specs=[
            pl.BlockSpec((rows, half), lambda i: (i, 0)),
            pl.BlockSpec((rows, half), lambda i: (i, 1)),
            pl.BlockSpec((1, half), lambda i: (0, 0)),
            pl.BlockSpec((1, half), lambda i: (0, 0)),
            pl.BlockSpec((SGU_GROUPS, SGU_CHUNK, SGU_CHUNK), lambda i: (0, 0, 0)),
            pl.BlockSpec((SGU_GROUPS, SGU_CHUNK, 1), lambda i: (0, 0, 0)),
        ],
        out_specs=pl.BlockSpec((rows, half), lambda i: (i, 0)),
        out_shape=jax.ShapeDtypeStruct((t, half), BF16),
        compiler_params=_cparams(1),
        name="sgu_gate",
    )(z, z, ln_g.reshape(1, half), ln_b.reshape(1, half), w_s, b_s.reshape(SGU_GROUPS, SGU_CHUNK, 1))


def _router_kernel(x_ref, wrt_ref, br_ref, sel_ref, wd_ref, pos_ref, cnt_ref, carry):
    i = pl.program_id(0)

    @pl.when(i == 0)
    def _():
        carry[...] = jnp.zeros_like(carry)

    x = _unpack_pairs_bf16(x_ref[...])
    logits = lax.dot_general(wrt_ref[...].astype(BF16), x, (((1,), (1,)), ((), ())),
                             preferred_element_type=F32)
    n_exp, tm = logits.shape
    ng = N_GROUPS
    per = n_exp // ng
    scores = jax.nn.sigmoid(logits)
    selv = scores + br_ref[...]
    a = [selv[j * ng:(j + 1) * ng] for j in range(per)]
    sc = [scores[j * ng:(j + 1) * ng] for j in range(per)]

    t1 = a[0]
    t2 = jnp.full_like(t1, -jnp.inf)
    for j in range(1, per):
        t2 = jnp.maximum(t2, jnp.minimum(t1, a[j]))
        t1 = jnp.maximum(t1, a[j])
    gscore = t1 + t2

    gidx = lax.broadcasted_iota(jnp.int32, (ng, tm), 0)
    grank = jnp.zeros((ng, tm), jnp.int32)
    for gp in range(ng):
        other = gscore[gp:gp + 1, :]
        beats = (other > gscore) | ((other == gscore) & (gp < gidx))
        grank = grank + beats.astype(jnp.int32)
    gmask = grank < TOPK_GROUPS

    val = [jnp.where(gmask, a[j], -jnp.inf) for j in range(per)]
    ranks = [jnp.zeros((ng, tm), jnp.int32) for _ in range(per)]
    for gp in range(ng):
        lower_group = gp < gidx
        lower_or_same_group = gp <= gidx
        for jp in range(per):
            other = val[jp][gp:gp + 1, :]
            for j in range(per):
                tie = lower_or_same_group if jp < j else lower_group
                beats = (other > val[j]) | ((other == val[j]) & tie)
                ranks[j] = ranks[j] + beats.astype(jnp.int32)
    selm = [(ranks[j] < TOP_K) & gmask for j in range(per)]

    ssum = jnp.zeros((ng, tm), F32)
    for j in range(per):
        ssum = ssum + jnp.where(selm[j], sc[j], 0.0)
    ssum = jnp.sum(ssum, axis=0, keepdims=True)
    wd = jnp.concatenate(
        [jnp.where(selm[j], sc[j] / ssum * ROUTE_SCALE, 0.0) for j in range(per)], axis=0)
    selmat = jnp.concatenate([jnp.where(selm[j], 1.0, 0.0) for j in range(per)], axis=0)

    s_i = lax.broadcasted_iota(jnp.int32, (tm, tm), 0)
    t_i = lax.broadcasted_iota(jnp.int32, (tm, tm), 1)
    upper = jnp.where(s_i < t_i, 1.0, 0.0).astype(BF16)
    base = carry[...][:, 0:1]
    pos = jnp.dot(selmat.astype(BF16), upper, preferred_element_type=F32) + base
    total = base + jnp.sum(selmat, axis=1, keepdims=True)
    carry[...] = jnp.broadcast_to(total, carry.shape)

    sel_ref[...] = selmat
    wd_ref[...] = wd
    pos_ref[...] = pos
    cnt_ref[...] = jnp.broadcast_to(total, cnt_ref.shape)


def _router(hm, w_rt, b_r_col):
    t = hm.shape[0]
    n_exp, d = w_rt.shape
    tm = _row_tile(512, t)
    col = pl.BlockSpec((n_exp, tm), lambda i: (0, i))
    return pl.pallas_call(
        _router_kernel,
        grid=(t // tm,),
        in_specs=[
            pl.BlockSpec((tm, d // 2), lambda i: (i, 0)),
            pl.BlockSpec((n_exp, d), lambda i: (0, 0)),
            pl.BlockSpec((n_exp, 1), lambda i: (0, 0)),
        ],
        out_specs=[col, col, col, pl.BlockSpec((n_exp, LANES), lambda i: (0, 0))],
        out_shape=[
            jax.ShapeDtypeStruct((n_exp, t), F32),
            jax.ShapeDtypeStruct((n_exp, t), F32),
            jax.ShapeDtypeStruct((n_exp, t), F32),
            jax.ShapeDtypeStruct((n_exp, LANES), F32),
        ],
        scratch_shapes=[pltpu.VMEM((n_exp, LANES), F32)],
        compiler_params=_cparams(1),
        name="router",
    )(hm, w_rt, b_r_col)


def _slots_kernel(sel_ref, wd_ref, pos_ref, ps_ref, d8_ref, w8_ref):
    sel = sel_ref[...]
    n_exp = sel.shape[0]
    r_i = lax.broadcasted_iota(jnp.int32, (n_exp, n_exp), 0)
    c_i = lax.broadcasted_iota(jnp.int32, (n_exp, n_exp), 1)
    lower = jnp.where(c_i < r_i, 1.0, 0.0).astype(BF16)
    slot = jnp.dot(lower, sel.astype(BF16), preferred_element_type=F32)
    chosen = sel > 0.5
    dest = pos_ref[...] + ps_ref[...]
    wd = wd_ref[...]
    d_rows, w_rows = [], []
    for k in range(TOP_K):
        pick = chosen & (slot == k)
        d_rows.append(jnp.sum(jnp.where(pick, dest, 0.0), axis=0, keepdims=True))
        w_rows.append(jnp.sum(jnp.where(pick, wd, 0.0), axis=0, keepdims=True))
    d8_ref[...] = jnp.concatenate(d_rows, axis=0).astype(jnp.int32)
    w8_ref[...] = jnp.concatenate(w_rows, axis=0)


def _slots(sel, wd, pos, pstart_col):
    n_exp, t = sel.shape
    tm = _row_tile(512, t)
    col = pl.BlockSpec((n_exp, tm), lambda i: (0, i))
    out = pl.BlockSpec((TOP_K, tm), lambda i: (0, i))
    return pl.pallas_call(
        _slots_kernel,
        grid=(t // tm,),
        in_specs=[col, col, col, pl.BlockSpec((n_exp, 1), lambda i: (0, 0))],
        out_specs=[out, out],
        out_shape=[jax.ShapeDtypeStruct((TOP_K, t), jnp.int32),
                   jax.ShapeDtypeStruct((TOP_K, t), F32)],
        compiler_params=_cparams(1),
        name="moe_slots",
    )(sel, wd, pos, pstart_col)


def _swiglu(x, wgu, wd, ff):
    h = jnp.dot(x, wgu, preferred_element_type=F32)
    act = (_silu(h[:, :ff]) * h[:, ff:]).astype(BF16)
    return jnp.dot(act, wd, preferred_element_type=F32)


def _expert_kernel(be_ref, nu_ref, src_cur, src_next, dst_prev, hm_hbm, wgu_ref, wd_ref,
                   y_hbm, xbuf0, xbuf1, ybuf0, ybuf1, wgu_bf, wd_bf, gsem, ssem, *, ff, n_real):
    b = pl.program_id(0)
    nu = nu_ref[0]
    xbufs, ybufs = (xbuf0, xbuf1), (ybuf0, ybuf1)
    rows = MOE_BLOCK

    def gather(ids, par):
        for i in range(rows):
            pltpu.make_async_copy(hm_hbm.at[pl.ds(ids[0, i], 1), :],
                                  xbufs[par].at[pl.ds(i, 1), :], gsem.at[par]).start()

    def scatter(ids, par):
        for i in range(rows):
            pltpu.make_async_copy(ybufs[par].at[pl.ds(i, 1), :],
                                  y_hbm.at[pl.ds(ids[0, i], 1), :], ssem.at[par]).start()

    def wait_gather(par):
        pltpu.make_async_copy(hm_hbm.at[pl.ds(0, rows), :], xbufs[par], gsem.at[par]).wait()

    def wait_scatter(par):
        pltpu.make_async_copy(ybufs[par], y_hbm.at[pl.ds(0, rows), :], ssem.at[par]).wait()

    def block_step(par, first):
        if first:
            for k in range(2):
                ybufs[k][...] = jnp.zeros_like(ybufs[k])
                spare = y_hbm.at[pl.ds(n_real + k * rows, rows), :]
                pltpu.make_async_copy(ybufs[k], spare, ssem.at[k]).start()
            for k in range(2):
                wait_scatter(k)
            gather(src_cur, par)
        wait_gather(par)

        @pl.when((b == 0) | (be_ref[b] != be_ref[jnp.maximum(b - 1, 0)]))
        def _():
            wgu_bf[...] = wgu_ref[...].astype(BF16)
            wd_bf[...] = wd_ref[...].astype(BF16)

        gather(src_next, 1 - par)
        if not first:
            scatter(dst_prev, 1 - par)
        y = _swiglu(_unpack_pairs_bf16(xbufs[par][...]), wgu_bf[...], wd_bf[...], ff)
        if not first:
            @pl.when(b >= 2)
            def _():
                wait_scatter(par)
        ybufs[par][...] = _pack_pairs(y)

    @pl.when(b == 0)
    def _():
        block_step(0, True)

    for par in range(2):
        @pl.when((b > 0) & (b < nu) & (b % 2 == par))
        def _(par=par):
            block_step(par, False)

        @pl.when((b == nu) & (b % 2 == par))
        def _(par=par):
            wait_gather(par)
            scatter(dst_prev, 1 - par)

            @pl.when(b >= 2)
            def _():
                wait_scatter(par)

            wait_scatter(1 - par)


def _experts(hm, src_ids, dst_ids, block_e, n_used, w_gu, w_down, layer, n_real):
    dh = hm.shape[1]
    d = 2 * dh
    ff = w_down.shape[2]
    n_blocks = src_ids.shape[0]

    def blk(b, be, nu):
        return jnp.minimum(b, nu[0] - 1)

    def ids_spec(index):
        return pl.BlockSpec((None, 1, MOE_BLOCK), lambda b, be, nu: (index(b), 0, 0),
                            memory_space=pltpu.SMEM)

    row_buf = pltpu.VMEM((MOE_BLOCK, dh), jnp.uint32)
    return pl.pallas_call(
        functools.partial(_expert_kernel, ff=ff, n_real=n_real),
        grid_spec=pltpu.PrefetchScalarGridSpec(
            num_scalar_prefetch=2,
            grid=(n_blocks,),
            in_specs=[
                ids_spec(lambda b: b),
                ids_spec(lambda b: jnp.minimum(b + 1, n_blocks - 1)),
                ids_spec(lambda b: jnp.maximum(b - 1, 0)),
                pl.BlockSpec(memory_space=pl.ANY),
                pl.BlockSpec((None, None, d, 2 * ff),
                             lambda b, be, nu: (layer, be[blk(b, be, nu)], 0, 0)),
                pl.BlockSpec((None, None, ff, d),
                             lambda b, be, nu: (layer, be[blk(b, be, nu)], 0, 0)),
            ],
            out_specs=pl.BlockSpec(memory_space=pl.ANY),
            scratch_shapes=[row_buf, row_buf, row_buf, row_buf,
                            pltpu.VMEM((d, 2 * ff), BF16), pltpu.VMEM((ff, d), BF16),
                            pltpu.SemaphoreType.DMA((2,)), pltpu.SemaphoreType.DMA((2,))],
        ),
        out_shape=jax.ShapeDtypeStruct((n_real + 2 * MOE_BLOCK, dh), jnp.uint32),
        compiler_params=_cparams(1),
        name="moe_experts",
    )(block_e, n_used, src_ids, src_ids, dst_ids, hm, w_gu, w_down)


def _combine_kernel(*refs, alpha, n_p, with_h, ff):
    yk_refs = refs[:TOP_K]
    w8_ref, hm_ref, wgu_ref, wd_ref, x_ref, gate_ref, g_ref, b_ref = refs[TOP_K:TOP_K + 8]
    rest = list(refs[TOP_K + 8:])
    wd_bf = rest.pop()
    wgu_bf = rest.pop()

    @pl.when(pl.program_id(0) == 0)
    def _():
        wgu_bf[...] = wgu_ref[...].astype(BF16)
        wd_bf[...] = wd_ref[...].astype(BF16)

    y_sh = _swiglu(_unpack_pairs_bf16(hm_ref[...]), wgu_bf[...], wd_bf[...], ff)
    w8 = w8_ref[...]
    half = yk_refs[0].shape[-1]
    acc_lo = y_sh[:, :half]
    acc_hi = y_sh[:, half:]
    for kk in range(TOP_K):
        rows = yk_refs[kk][...]
        wk = w8[:, kk:kk + 1]
        acc_lo = acc_lo + wk * _unpack_lo(rows)
        acc_hi = acc_hi + wk * _unpack_hi(rows)
    acc = jnp.concatenate([acc_lo, acc_hi], axis=-1)
    y = alpha * x_ref[...] + gate_ref[...] * acc
    xn = _layer_norm(y, g_ref[...], b_ref[...])
    if with_h:
        sh_ref, sc_ref, xo_ref, ho_ref = rest
        xo_ref[...] = xn
        _store_h(ho_ref, xn * (1.0 + sc_ref[...]) + sh_ref[...])
    else:
        xp_ref, xs_ref = rest
        i = pl.program_id(0)

        @pl.when(i < n_p)
        def _():
            xp_ref[...] = xn

        @pl.when(i >= n_p)
        def _():
            xs_ref[...] = xn


def _combine(yk, w8, hm, ws_gu, ws_down, layer, x, gate, ln_g, ln_b, shift, scale, alpha, tp,
             seq_s):
    t, d = x.shape
    ff = ws_down.shape[1]
    tm = _row_tile(256, tp, seq_s)
    n_steps = t // tm
    n_p = tp // tm
    with_h = shift is not None
    cid = _cond_index(tm, tp, seq_s)
    row = pl.BlockSpec((tm, d), lambda i: (i, 0))
    packed_row = pl.BlockSpec((tm, d // 2), lambda i: (i, 0))
    mod_spec = pl.BlockSpec((None, 1, d), lambda i: (cid(i), 0, 0))
    vec = pl.BlockSpec((1, d), lambda i: (0, 0))
    resident = pl.Buffered(1)
    in_specs = [pl.BlockSpec((tm, d // 2), lambda i, kk=kk: (kk * n_steps + i, 0))
                for kk in range(TOP_K)]
    in_specs += [pl.BlockSpec((tm, TOP_K), lambda i: (i, 0)), packed_row,
                 pl.BlockSpec((None, d, 2 * ff), lambda i: (layer, 0, 0), pipeline_mode=resident),
                 pl.BlockSpec((None, ff, d), lambda i: (layer, 0, 0), pipeline_mode=resident),
                 row, mod_spec, vec, vec]
    args = [yk] * TOP_K + [w8, hm, ws_gu, ws_down, x, gate, ln_g.reshape(1, d),
                           ln_b.reshape(1, d)]
    if with_h:
        in_specs += [mod_spec, mod_spec]
        args += [shift, scale]
        out_specs = [row, row]
        out_shape = [jax.ShapeDtypeStruct((t, d), F32), jax.ShapeDtypeStruct((t, d), BF16)]
    else:
        out_specs = [pl.BlockSpec((tm, d), lambda i: (jnp.minimum(i, n_p - 1), 0)),
                     pl.BlockSpec((tm, d), lambda i: (jnp.maximum(i - n_p, 0), 0))]
        out_shape = [jax.ShapeDtypeStruct((tp, d), F32), jax.ShapeDtypeStruct((t - tp, d), F32)]
    return pl.pallas_call(
        functools.partial(_combine_kernel, alpha=alpha, n_p=n_p, with_h=with_h, ff=ff),
        grid=(n_steps,),
        in_specs=in_specs,
        out_specs=out_specs,
        out_shape=out_shape,
        scratch_shapes=[pltpu.VMEM((d, 2 * ff), BF16), pltpu.VMEM((ff, d), BF16)],
        compiler_params=_cparams(1),
        name="moe_combine",
    )(*args)


def _expert_row_order(n_exp):
    per = n_exp // N_GROUPS
    r = jnp.arange(n_exp)
    return (r % N_GROUPS) * per + r // N_GROUPS


def _count_le(edges, v):
    return jnp.sum((edges[None, :] <= v[:, None]).astype(jnp.int32), axis=1)


def _moe_layer(x, hm, layer, w_r, b_r, w_gu, w_down, ws_gu, ws_down, gate, ln_g, ln_b, shift,
               scale, alpha, tp, seq_s):
    t = hm.shape[0]
    n_exp = w_r.shape[1]
    order = _expert_row_order(n_exp)
    sel, wd, pos, cnt = _router(hm, w_r.T[order], b_r[order].reshape(n_exp, 1))

    assert (t * TOP_K) % MOE_BLOCK == 0
    n_blocks = t * TOP_K // MOE_BLOCK + n_exp
    cap = n_blocks * MOE_BLOCK
    counts = cnt[:, 0].astype(jnp.int32)
    padded = (counts + MOE_BLOCK - 1) // MOE_BLOCK * MOE_BLOCK
    pend = jnp.cumsum(padded)
    pstart = pend - padded
    n_used = (pend[-1:] // MOE_BLOCK).astype(jnp.int32)
    block_row = jnp.minimum(_count_le(pend, jnp.arange(n_blocks) * MOE_BLOCK), n_exp - 1)
    per = n_exp // N_GROUPS
    block_e = ((block_row % N_GROUPS) * per + block_row // N_GROUPS).astype(jnp.int32)
    dest8, w8 = _slots(sel, wd, pos, pstart.astype(F32).reshape(n_exp, 1))

    padlen = padded - counts
    cpad_start = jnp.cumsum(padlen) - padlen
    f = jnp.arange(cap - t * TOP_K)
    fill = f + jnp.sum(jnp.where(cpad_start[None, :] <= f[:, None], counts[None, :], 0), axis=1)
    position = jnp.concatenate([dest8.reshape(-1), fill.astype(jnp.int32)])
    assign = jnp.argsort(position).astype(jnp.int32)
    src_ids = (assign % t).reshape(n_blocks, 1, MOE_BLOCK)
    n_real = t * TOP_K
    spare = n_real + jnp.arange(cap, dtype=jnp.int32) % (2 * MOE_BLOCK)
    dst_ids = jnp.where(assign >= n_real, spare, assign).reshape(n_blocks, 1, MOE_BLOCK)

    yk = _experts(hm, src_ids, dst_ids, block_e, n_used, w_gu, w_down, layer, n_real)
    return _combine(yk, w8.T, hm, ws_gu, ws_down, layer, x, gate, ln_g, ln_b, shift, scale, alpha,
                    tp, seq_s)


def _gla_layer(h, state_gla, new_state, layer_j, w_in, w_a1, w_a2, b_a, norm_g, w_out, dims):
    bp, sp, bs, ss = dims
    tp = bp * sp
    d = h.shape[1]
    rank = w_a1.shape[3]
    proj = _matmul(h, w_in, layer_j, BF16)
    w_low = jnp.transpose(w_a1[layer_j], (1, 0, 2)).reshape(1, d, 2 * rank)
    low = _matmul(h, w_low, 0, F32)
    low = jnp.transpose(low.reshape(-1, 2, rank), (1, 0, 2))
    og_p, st = _gla_scan(proj, low, w_a2[layer_j], b_a[layer_j], norm_g[layer_j], None, layer_j,
                         batch=bp, seq_len=sp, row0=0, hp=GLA_HEADS,
                         new_state="first" if new_state is None else new_state,
                         n_state_layers=w_in.shape[0])
    (og_s,) = _gla_scan(proj, low, w_a2[layer_j], b_a[layer_j], norm_g[layer_j], state_gla,
                        layer_j, batch=bs, seq_len=ss, row0=tp, hp=GLA_HEADS // 2)
    out = _matmul(jnp.concatenate([og_p, og_s], axis=0), w_out, layer_j, BF16)
    return out, st


def _sgu_layer(h, layer_j, w_in, ln_g, ln_b, w_s, b_s, w_out):
    z = _matmul(h, w_in, layer_j, BF16, act="gelu")
    gated = _sgu_gate(z, ln_g[layer_j], ln_b[layer_j], w_s[layer_j], b_s[layer_j])
    return _matmul(gated, w_out, layer_j, BF16)


def _grid_pos_embed(rows, d):
    t = jnp.arange(rows * GRID_W)
    r = (t // GRID_W).astype(F32)
    col = (t % GRID_W).astype(F32)
    quarter = d // 4
    omega = 1.0 / (10000.0 ** (jnp.arange(quarter, dtype=F32) / quarter))

    def emb(p):
        a = p[:, None] * omega[None, :]
        return jnp.concatenate([jnp.sin(a), jnp.cos(a)], axis=-1)

    return jnp.concatenate([emb(r), emb(col)], axis=-1)


def kernel(x_prompt, x_sample, state_gla, c, c_ctx, w_ada, b_ada, ln_g, ln_b, gla_w_in, gla_w_a1, gla_w_a2, gla_b_a, gla_norm_g, gla_w_out, sgu_w_in, sgu_ln_g, sgu_ln_b, sgu_w_s, sgu_b_s, sgu_w_out, moe_w_router, moe_b_router, moe_w_gu, moe_w_down, moe_ws_gu, moe_ws_down):
    bp, sp, d = x_prompt.shape
    bs, ss, _ = x_sample.shape
    assert 1 + bs <= COND_ROWS
    tp, ts = bp * sp, bs * ss
    depth = w_ada.shape[0]
    alpha = (2 * depth) ** 0.25
    dims = (bp, sp, bs, ss)

    cond = jnp.zeros((COND_ROWS, d), F32).at[0].set(c_ctx).at[1:1 + bs].set(c)
    mods = _ada_mods(cond, w_ada, b_ada).reshape(depth, COND_ROWS, 6, 1, d)

    def mod(l, i):
        return mods[l, :, i]

    pos = _grid_pos_embed(ss // GRID_W, d).astype(x_sample.dtype)
    x, h = _prep(x_prompt.reshape(tp, d), x_sample.reshape(ts, d), pos, mod(0, 0), mod(0, 1), ss)

    new_state = None
    for l in range(depth):
        j = l // 2
        if l % 2 == 0:
            out, new_state = _gla_layer(h, state_gla, new_state, j, gla_w_in, gla_w_a1, gla_w_a2,
                                        gla_b_a, gla_norm_g, gla_w_out, dims)
        else:
            out = _sgu_layer(h, j, sgu_w_in, sgu_ln_g, sgu_ln_b, sgu_w_s, sgu_b_s, sgu_w_out)
        x, hm = _post_norm(x, out, mod(l, 2), ln_g[l, 0], ln_b[l, 0], mod(l, 3), mod(l, 4),
                           alpha, tp, ss)
        last = l == depth - 1
        res = _moe_layer(x, hm, l, moe_w_router[l], moe_b_router[l], moe_w_gu, moe_w_down,
                         moe_ws_gu, moe_ws_down, mod(l, 5), ln_g[l, 1], ln_b[l, 1],
                         None if last else mod(l + 1, 0), None if last else mod(l + 1, 1),
                         alpha, tp, ss)
        if last:
            y_prompt, y_sample = res
        else:
            x, h = res

    return y_prompt.reshape(bp, sp, d), y_sample.reshape(bs, ss, d), new_state
```

```python
import functools

import jax
import jax.numpy as jnp
from jax import lax
from jax.experimental import pallas as pl
from jax.experimental.pallas import tpu as pltpu

F32 = jnp.float32
BF16 = jnp.bfloat16

V7X_VMEM_LIMIT_BYTES = 56 * 1024 * 1024
LANES = 128
SUBLANES = 8

GRID_W = 64
GLA_HEADS = 4
GLA_CHUNK = 64
GLA_NORMALIZER = 16.0
SGU_GROUPS = 4
SGU_CHUNK = 128
TOP_K = 8
N_GROUPS = 8
TOPK_GROUPS = 4
ROUTE_SCALE = 2.5
LN_EPS = 1e-5
RMS_EPS = 1e-6
MOE_BLOCK = 256
COND_ROWS = 8


def _cparams(n_axes):
    return pltpu.CompilerParams(
        dimension_semantics=("arbitrary",) * n_axes,
        vmem_limit_bytes=V7X_VMEM_LIMIT_BYTES,
    )


def _silu(x):
    return x * jax.nn.sigmoid(x)


def _layer_norm(y, g, b):
    mu = jnp.mean(y, axis=-1, keepdims=True)
    d = y - mu
    var = jnp.mean(d * d, axis=-1, keepdims=True)
    return d * lax.rsqrt(var + LN_EPS) * g + b


_HI16 = 0xFFFF0000


def _pack_pairs(x):
    half = x.shape[-1] // 2
    lo = lax.bitcast_convert_type(x[:, :half].astype(BF16).astype(F32), jnp.uint32)
    hi = lax.bitcast_convert_type(x[:, half:].astype(BF16).astype(F32), jnp.uint32)
    return lax.shift_right_logical(lo, jnp.uint32(16)) | (hi & jnp.uint32(_HI16))


def _unpack_lo(w):
    return lax.bitcast_convert_type(lax.shift_left(w, jnp.uint32(16)), F32)


def _unpack_hi(w):
    return lax.bitcast_convert_type(w & jnp.uint32(_HI16), F32)


def _unpack_pairs_bf16(w):
    return jnp.concatenate([_unpack_lo(w).astype(BF16), _unpack_hi(w).astype(BF16)], axis=-1)


def _row_tile(limit, *sizes):
    tm = limit
    while any(s % tm for s in sizes):
        tm //= 2
    return tm


def _cond_index(tm, tp, seq_s):
    def index(i):
        return jnp.where(i * tm < tp, 0, 1 + (i * tm - tp) // seq_s)

    return index


def _ada_kernel(c_ref, w_ref, b_ref, o_ref):
    s = _silu(c_ref[...]).astype(BF16)
    o_ref[...] = jnp.dot(s, w_ref[...].astype(BF16), preferred_element_type=F32) + b_ref[...]


def _ada_mods(cond, w_ada, b_ada):
    depth, d, n = w_ada.shape
    tn = _row_tile(1024, n)
    return pl.pallas_call(
        _ada_kernel,
        grid=(depth, n // tn),
        in_specs=[
            pl.BlockSpec((COND_ROWS, d), lambda l, j: (0, 0)),
            pl.BlockSpec((None, d, tn), lambda l, j: (l, 0, j)),
            pl.BlockSpec((None, 1, tn), lambda l, j: (l, 0, j)),
        ],
        out_specs=pl.BlockSpec((None, COND_ROWS, tn), lambda l, j: (l, 0, j)),
        out_shape=jax.ShapeDtypeStruct((depth, COND_ROWS, n), F32),
        compiler_params=_cparams(2),
        name="ada_mods",
    )(cond, w_ada, b_ada.reshape(depth, 1, n))


def _prep_kernel(xp_ref, xs_ref, pos_ref, sh_ref, sc_ref, x_ref, h_ref, *, n_p):
    x = jnp.where(pl.program_id(0) < n_p, xp_ref[...], xs_ref[...] + pos_ref[...])
    x_ref[...] = x
    h_ref[...] = (x * (1.0 + sc_ref[...]) + sh_ref[...]).astype(h_ref.dtype)


def _prep(xp, xs, pos, shift, scale, seq_s):
    tp, d = xp.shape
    ts = xs.shape[0]
    tm = _row_tile(512, tp, seq_s)
    n_p = tp // tm
    n_pos = seq_s // tm
    cid = _cond_index(tm, tp, seq_s)
    mod_spec = pl.BlockSpec((None, 1, d), lambda i: (cid(i), 0, 0))
    return pl.pallas_call(
        functools.partial(_prep_kernel, n_p=n_p),
        grid=((tp + ts) // tm,),
        in_specs=[
            pl.BlockSpec((tm, d), lambda i: (jnp.minimum(i, n_p - 1), 0)),
            pl.BlockSpec((tm, d), lambda i: (jnp.maximum(i - n_p, 0), 0)),
            pl.BlockSpec((tm, d), lambda i: (jnp.maximum(i - n_p, 0) % n_pos, 0)),
            mod_spec,
            mod_spec,
        ],
        out_specs=[pl.BlockSpec((tm, d), lambda i: (i, 0))] * 2,
        out_shape=[
            jax.ShapeDtypeStruct((tp + ts, d), F32),
            jax.ShapeDtypeStruct((tp + ts, d), BF16),
        ],
        compiler_params=_cparams(1),
        name="prep",
    )(xp, xs, pos, shift, scale)


def _mm_kernel(*refs, act, n_first):
    *x_refs, w_ref, o_ref, wbf_ref = refs

    @pl.when(pl.program_id(1) == 0)
    def _():
        wbf_ref[...] = w_ref[...].astype(BF16)

    x = x_refs[0][...]
    if len(x_refs) == 2:
        x = jnp.where(pl.program_id(1) < n_first, x, x_refs[1][...])
    acc = jnp.dot(x.astype(BF16), wbf_ref[...], preferred_element_type=F32)
    if act == "gelu":
        acc = 0.5 * acc * (1.0 + lax.erf(acc * 0.7071067811865476))
    o_ref[...] = acc.astype(o_ref.dtype)


def _matmul(x, w, layer, out_dtype, act=None):
    xs = x if isinstance(x, tuple) else (x,)
    x = xs[0]
    m = sum(a.shape[0] for a in xs)
    k = x.shape[1]
    n = w.shape[2]
    tn_limit = 1024
    while k * tn_limit * 10 > V7X_VMEM_LIMIT_BYTES * 4 // 7 and tn_limit > LANES:
        tn_limit //= 2
    tn = _row_tile(tn_limit, n)
    out_bytes = jnp.dtype(out_dtype).itemsize
    tm_limit = 1024
    while (k * tn * 10
           + tm_limit * (2 * len(xs) * k * x.dtype.itemsize + tn * (2 * out_bytes + 4))
           > V7X_VMEM_LIMIT_BYTES * 7 // 8 and tm_limit > SUBLANES):
        tm_limit //= 2
    tm = _row_tile(tm_limit, *(a.shape[0] for a in xs))
    n_first = xs[0].shape[0] // tm
    x_specs = [pl.BlockSpec((tm, k), lambda j, i: (jnp.minimum(i, n_first - 1), 0))]
    if len(xs) == 2:
        x_specs.append(pl.BlockSpec((tm, k), lambda j, i: (jnp.maximum(i - n_first, 0), 0)))
    return pl.pallas_call(
        functools.partial(_mm_kernel, act=act, n_first=n_first),
        grid=(n // tn, m // tm),
        in_specs=x_specs + [pl.BlockSpec((None, k, tn), lambda j, i: (layer, 0, j))],
        out_specs=pl.BlockSpec((tm, tn), lambda j, i: (i, j)),
        out_shape=jax.ShapeDtypeStruct((m, n), out_dtype),
        scratch_shapes=[pltpu.VMEM((k, tn), BF16)],
        compiler_params=_cparams(2),
        name="matmul_" + (act or "lin"),
    )(*xs, w)


def _prefix_sum_mm(tri, x):
    hi = x.astype(BF16)
    r1 = x - hi.astype(F32)
    mid = r1.astype(BF16)
    lo = (r1 - mid.astype(F32)).astype(BF16)
    out = jnp.dot(tri, hi, preferred_element_type=F32)
    out = out + jnp.dot(tri, mid, preferred_element_type=F32)
    return out + jnp.dot(tri, lo, preferred_element_type=F32)


def _gla_kernel(*refs, seq_len, hp, has_s0, out_state, layer_j):
    q_ref, k_ref, v_ref, g_ref, low_ref, wa2_ref, ba_ref, ng_ref = refs[:8]
    rest = list(refs[8:])
    s0_ref = rest.pop(0) if has_s0 else None
    if out_state == "next":
        rest.pop(0)
    o_ref = rest.pop(0)
    st_ref = rest.pop(0) if out_state else None
    gk_scr, oacc, s_t = rest

    c = GLA_CHUNK
    n = seq_len // c
    dk = q_ref.shape[1] // hp
    dv = v_ref.shape[1] // hp
    scale = dk ** -0.5

    for z in range(2):
        pre = jnp.dot(low_ref[z].astype(BF16), wa2_ref[z].astype(BF16),
                      preferred_element_type=F32) + ba_ref[z]
        log_sig = jnp.minimum(pre, 0.0) - jnp.log(1.0 + jnp.exp(-jnp.abs(pre)))
        gk_scr[z] = log_sig * (1.0 / GLA_NORMALIZER)
        for hh in range(hp):
            s_t[z, hh] = s0_ref[z, hh].T if has_s0 else jnp.zeros((dv, dk), F32)

    row = lax.broadcasted_iota(jnp.int32, (c, c), 0)
    col = lax.broadcasted_iota(jnp.int32, (c, c), 1)
    masks = (col <= row, col >= row)
    tris = tuple(jnp.where(m, 1.0, 0.0).astype(BF16) for m in masks)
    nt_dims = (((1,), (1,)), ((), ()))
    tn_dims = (((0,), (0,)), ((), ()))

    n_stages = 5

    def chunk_stages(ci, z, hh):
        rows = pl.ds(pl.multiple_of(ci * c, c), c)
        ck = slice(hh * dk, (hh + 1) * dk)
        cv = slice(hh * dv, (hh + 1) * dv)
        gcum = _prefix_sum_mm(tris[z], gk_scr[z, rows, ck])
        yield
        qc = q_ref[rows, ck].astype(F32) * scale
        kc = k_ref[rows, ck].astype(F32)
        if z == 0:
            g_mid = gcum[c // 2 - 1:c // 2]
            g_last = gcum[c - 1:c]
        else:
            g_mid = gcum[c // 2:c // 2 + 1]
            g_last = gcum[0:1]
        q_in = (qc * jnp.exp(gcum - g_mid)).astype(BF16)
        k_in = (kc * jnp.exp(g_mid - gcum)).astype(BF16)
        a = lax.dot_general(q_in, k_in, nt_dims, preferred_element_type=F32)
        yield
        vc = v_ref[rows, cv].astype(BF16)
        a = jnp.where(masks[z], a, 0.0).astype(BF16)
        o = jnp.dot(a, vc, preferred_element_type=F32)
        yield
        s_prev = s_t[z, hh]
        q_dec = (qc * jnp.exp(gcum)).astype(BF16)
        o = o + lax.dot_general(q_dec, s_prev.astype(BF16), nt_dims, preferred_element_type=F32)
        oacc[z, rows, cv] = o
        yield
        k_dec = (kc * jnp.exp(g_last - gcum)).astype(BF16)
        u_t = lax.dot_general(vc, k_dec, tn_dims, preferred_element_type=F32)
        s_t[z, hh] = s_prev * jnp.exp(g_last) + u_t
        yield

    def body(i, carry):
        scans = [chunk_stages(i if z == 0 else n - 1 - i, z, hh)
                 for z in range(2) for hh in range(hp)]
        for _ in range(n_stages):
            for scan in scans:
                next(scan)
        return carry

    lax.fori_loop(0, n, body, 0)

    for hh in range(hp):
        cv = slice(hh * dv, (hh + 1) * dv)
        for z in range(2):
            if out_state == "next":
                st_ref[z, hh] = s_t[z, hh].T
            elif out_state == "first":
                for j in range(st_ref.shape[0]):
                    st_ref[j, z, hh] = s_t[z, hh].T if j == layer_j else jnp.zeros((dk, dv), F32)
        o = oacc[0, :, cv] + oacc[1, :, cv]
        on = o * lax.rsqrt(jnp.mean(o * o, axis=-1, keepdims=True) + RMS_EPS) * ng_ref[...]
        o_ref[:, cv] = (on * _silu(g_ref[:, cv].astype(F32))).astype(o_ref.dtype)


def _gla_scan(proj, low, w_a2, b_a, norm_g, state, layer_j, *, batch, seq_len, row0, hp,
              new_state=None, n_state_layers=1):
    h = GLA_HEADS
    hk = w_a2.shape[2]
    dk = hk // h
    hv = (proj.shape[1] - 2 * hk) // 2
    dv = hv // h
    rank = w_a2.shape[1]
    off = row0 // seq_len
    has_s0 = state is not None
    wk, wv = hp * dk, hp * dv
    in_specs = [
        pl.BlockSpec((seq_len, wk), lambda b, i: (b + off, i)),
        pl.BlockSpec((seq_len, wk), lambda b, i: (b + off, hk // wk + i)),
        pl.BlockSpec((seq_len, wv), lambda b, i: (b + off, 2 * hk // wv + i)),
        pl.BlockSpec((seq_len, wv), lambda b, i: (b + off, (2 * hk + hv) // wv + i)),
        pl.BlockSpec((2, seq_len, rank), lambda b, i: (0, b + off, 0)),
        pl.BlockSpec((2, rank, wk), lambda b, i: (0, 0, i)),
        pl.BlockSpec((2, 1, wk), lambda b, i: (0, 0, i)),
        pl.BlockSpec((1, dv), lambda b, i: (0, 0)),
    ]
    args = [proj, proj, proj, proj, low, w_a2, b_a.reshape(2, 1, hk), norm_g.reshape(1, dv)]
    if has_s0:
        in_specs.append(pl.BlockSpec((None, None, 2, hp, dk, dv),
                                     lambda b, i: (b, layer_j, 0, i, 0, 0)))
        args.append(state)
    out_specs = [pl.BlockSpec((seq_len, wv), lambda b, i: (b, i))]
    out_shape = [jax.ShapeDtypeStruct((batch * seq_len, hv), BF16)]
    aliases = {}
    out_state = None
    if isinstance(new_state, str):
        out_state = "first"
        out_specs.append(pl.BlockSpec((None, n_state_layers, 2, hp, dk, dv),
                                      lambda b, i: (b, 0, 0, i, 0, 0)))
        out_shape.append(jax.ShapeDtypeStruct((batch, n_state_layers, 2, h, dk, dv), F32))
    elif new_state is not None:
        out_state = "next"
        aliases = {len(args): 1}
        in_specs.append(pl.BlockSpec(memory_space=pl.ANY))
        args.append(new_state)
        out_specs.append(pl.BlockSpec((None, None, 2, hp, dk, dv),
                                      lambda b, i: (b, layer_j, 0, i, 0, 0)))
        out_shape.append(jax.ShapeDtypeStruct(new_state.shape, F32))
    return pl.pallas_call(
        functools.partial(_gla_kernel, seq_len=seq_len, hp=hp, has_s0=has_s0,
                          out_state=out_state, layer_j=layer_j),
        grid=(batch, h // hp),
        in_specs=in_specs,
        out_specs=out_specs,
        out_shape=out_shape,
        input_output_aliases=aliases,
        scratch_shapes=[
            pltpu.VMEM((2, seq_len, wk), F32),
            pltpu.VMEM((2, seq_len, wv), F32),
            pltpu.VMEM((2, hp, dv, dk), F32),
        ],
        compiler_params=_cparams(2),
        name="gla_scan",
    )(*args)


def _sgu_kernel(u_ref, v_ref, lg_ref, lb_ref, ws_ref, bs_ref, o_ref):
    v = v_ref[...].astype(F32)
    vn = _layer_norm(v, lg_ref[...], lb_ref[...]).astype(BF16)
    rows, half = vn.shape
    cg = half // SGU_GROUPS
    for r in range(rows // SGU_CHUNK):
        rs = slice(r * SGU_CHUNK, (r + 1) * SGU_CHUNK)
        for g in range(SGU_GROUPS):
            cs = slice(g * cg, (g + 1) * cg)
            s = jnp.dot(ws_ref[g].astype(BF16), vn[rs, cs], preferred_element_type=F32) + bs_ref[g]
            o_ref[rs, cs] = (u_ref[rs, cs].astype(F32) * s).astype(o_ref.dtype)


def _sgu_gate(z, ln_g, ln_b, w_s, b_s):
    t, ffn = z.shape
    half = ffn // 2
    rows = SGU_CHUNK
    return pl.pallas_call(
        _sgu_kernel,
        grid=(t // rows,),
        in_specs=[
            pl.BlockSpec((rows, half), lambda i: (i, 0)),
            pl.BlockSpec((rows, half), lambda i: (i, 1)),
            pl.BlockSpec((1, half), lambda i: (0, 0)),
            pl.BlockSpec((1, half), lambda i: (0, 0)),
            pl.BlockSpec((SGU_GROUPS, SGU_CHUNK, SGU_CHUNK), lambda i: (0, 0, 0)),
            pl.BlockSpec((SGU_GROUPS, SGU_CHUNK, 1), lambda i: (0, 0, 0)),
        ],
        out_specs=pl.BlockSpec((rows, half), lambda i: (i, 0)),
        out_shape=jax.ShapeDtypeStruct((t, half), BF16),
        compiler_params=_cparams(1),
        name="sgu_gate",
    )(z, z, ln_g.reshape(1, half), ln_b.reshape(1, half), w_s, b_s.reshape(SGU_GROUPS, SGU_CHUNK, 1))


def _norm_route_kernel(x_ref, o_ref, gate_ref, g_ref, b_ref, sh_ref, sc_ref, wrt_ref, br_ref,
                       xo_ref, ho_ref, sel_ref, wd_ref, pos_ref, cnt_ref, carry, *, alpha):
    i = pl.program_id(0)

    @pl.when(i == 0)
    def _():
        carry[...] = jnp.zeros_like(carry)

    y = alpha * x_ref[...] + gate_ref[...] * o_ref[...].astype(F32)
    xn = _layer_norm(y, g_ref[...], b_ref[...])
    xo_ref[...] = xn
    h = xn * (1.0 + sc_ref[...]) + sh_ref[...]
    ho_ref[...] = _pack_pairs(h)

    x = h.astype(BF16)
    logits = lax.dot_general(wrt_ref[...].astype(BF16), x, (((1,), (1,)), ((), ())),
                             preferred_element_type=F32)
    n_exp, tm = logits.shape
    ng = N_GROUPS
    per = n_exp // ng
    scores = jax.nn.sigmoid(logits)
    selv = scores + br_ref[...]
    a = [selv[j * ng:(j + 1) * ng] for j in range(per)]
    sc = [scores[j * ng:(j + 1) * ng] for j in range(per)]

    t1 = a[0]
    t2 = jnp.full_like(t1, -jnp.inf)
    for j in range(1, per):
        t2 = jnp.maximum(t2, jnp.minimum(t1, a[j]))
        t1 = jnp.maximum(t1, a[j])
    gscore = t1 + t2

    gidx = lax.broadcasted_iota(jnp.int32, (ng, tm), 0)
    grank = jnp.zeros((ng, tm), jnp.int32)
    for gp in range(ng):
        other = gscore[gp:gp + 1, :]
        beats = (other > gscore) | ((other == gscore) & (gp < gidx))
        grank = grank + beats.astype(jnp.int32)
    gmask = grank < TOPK_GROUPS

    val = [jnp.where(gmask, a[j], -jnp.inf) for j in range(per)]
    ranks = [jnp.zeros((ng, tm), jnp.int32) for _ in range(per)]
    for gp in range(ng):
        lower_group = gp < gidx
        lower_or_same_group = gp <= gidx
        for jp in range(per):
            other = val[jp][gp:gp + 1, :]
            for j in range(per):
                tie = lower_or_same_group if jp < j else lower_group
                beats = (other > val[j]) | ((other == val[j]) & tie)
                ranks[j] = ranks[j] + beats.astype(jnp.int32)
    selm = [(ranks[j] < TOP_K) & gmask for j in range(per)]

    ssum = jnp.zeros((ng, tm), F32)
    for j in range(per):
        ssum = ssum + jnp.where(selm[j], sc[j], 0.0)
    ssum = jnp.sum(ssum, axis=0, keepdims=True)
    wd = jnp.concatenate(
        [jnp.where(selm[j], sc[j] / ssum * ROUTE_SCALE, 0.0) for j in range(per)], axis=0)
    selmat = jnp.concatenate([jnp.where(selm[j], 1.0, 0.0) for j in range(per)], axis=0)

    s_i = lax.broadcasted_iota(jnp.int32, (tm, tm), 0)
    t_i = lax.broadcasted_iota(jnp.int32, (tm, tm), 1)
    upper = jnp.where(s_i < t_i, 1.0, 0.0).astype(BF16)
    base = carry[...][:, 0:1]
    pos = jnp.dot(selmat.astype(BF16), upper, preferred_element_type=F32) + base
    total = base + jnp.sum(selmat, axis=1, keepdims=True)
    carry[...] = jnp.broadcast_to(total, carry.shape)

    sel_ref[...] = selmat
    wd_ref[...] = wd
    pos_ref[...] = pos
    cnt_ref[...] = jnp.broadcast_to(total, cnt_ref.shape)


def _norm_route(x, out, gate, ln_g, ln_b, shift, scale, w_rt, b_r_col, alpha, tp, seq_s):
    t, d = x.shape
    n_exp = w_rt.shape[0]
    tm = _row_tile(512, tp, seq_s)
    cid = _cond_index(tm, tp, seq_s)
    row = pl.BlockSpec((tm, d), lambda i: (i, 0))
    mod_spec = pl.BlockSpec((None, 1, d), lambda i: (cid(i), 0, 0))
    vec = pl.BlockSpec((1, d), lambda i: (0, 0))
    col = pl.BlockSpec((n_exp, tm), lambda i: (0, i))
    return pl.pallas_call(
        functools.partial(_norm_route_kernel, alpha=alpha),
        grid=(t // tm,),
        in_specs=[row, row, mod_spec, vec, vec, mod_spec, mod_spec,
                  pl.BlockSpec((n_exp, d), lambda i: (0, 0)),
                  pl.BlockSpec((n_exp, 1), lambda i: (0, 0))],
        out_specs=[row, pl.BlockSpec((tm, d // 2), lambda i: (i, 0)), col, col, col,
                   pl.BlockSpec((n_exp, LANES), lambda i: (0, 0))],
        out_shape=[
            jax.ShapeDtypeStruct((t, d), F32),
            jax.ShapeDtypeStruct((t, d // 2), jnp.uint32),
            jax.ShapeDtypeStruct((n_exp, t), F32),
            jax.ShapeDtypeStruct((n_exp, t), F32),
            jax.ShapeDtypeStruct((n_exp, t), F32),
            jax.ShapeDtypeStruct((n_exp, LANES), F32),
        ],
        scratch_shapes=[pltpu.VMEM((n_exp, LANES), F32)],
        compiler_params=_cparams(1),
        name="norm_route",
    )(x, out, gate, ln_g.reshape(1, d), ln_b.reshape(1, d), shift, scale, w_rt, b_r_col)


def _slots_kernel(sel_ref, wd_ref, pos_ref, ps_ref, d8_ref, w8_ref):
    sel = sel_ref[...]
    n_exp = sel.shape[0]
    r_i = lax.broadcasted_iota(jnp.int32, (n_exp, n_exp), 0)
    c_i = lax.broadcasted_iota(jnp.int32, (n_exp, n_exp), 1)
    lower = jnp.where(c_i < r_i, 1.0, 0.0).astype(BF16)
    slot = jnp.dot(lower, sel.astype(BF16), preferred_element_type=F32)
    chosen = sel > 0.5
    dest = pos_ref[...] + ps_ref[...]
    wd = wd_ref[...]
    d_rows, w_rows = [], []
    for k in range(TOP_K):
        pick = chosen & (slot == k)
        d_rows.append(jnp.sum(jnp.where(pick, dest, 0.0), axis=0, keepdims=True))
        w_rows.append(jnp.sum(jnp.where(pick, wd, 0.0), axis=0, keepdims=True))
    d8_ref[...] = jnp.concatenate(d_rows, axis=0).astype(jnp.int32)
    w8_ref[...] = jnp.concatenate(w_rows, axis=0)


def _slots(sel, wd, pos, pstart_col):
    n_exp, t = sel.shape
    tm = _row_tile(512, t)
    col = pl.BlockSpec((n_exp, tm), lambda i: (0, i))
    out = pl.BlockSpec((TOP_K, tm), lambda i: (0, i))
    return pl.pallas_call(
        _slots_kernel,
        grid=(t // tm,),
        in_specs=[col, col, col, pl.BlockSpec((n_exp, 1), lambda i: (0, 0))],
        out_specs=[out, out],
        out_shape=[jax.ShapeDtypeStruct((TOP_K, t), jnp.int32),
                   jax.ShapeDtypeStruct((TOP_K, t), F32)],
        compiler_params=_cparams(1),
        name="moe_slots",
    )(sel, wd, pos, pstart_col)


def _swiglu(x, wgu, wd, ff):
    h = jnp.dot(x, wgu, preferred_element_type=F32)
    act = (_silu(h[:, :ff]) * h[:, ff:]).astype(BF16)
    return jnp.dot(act, wd, preferred_element_type=F32)


def _expert_kernel(be_ref, nu_ref, src_cur, src_next, dst_prev, hm_hbm, wgu_ref, wd_ref,
                   y_hbm, xbuf0, xbuf1, ybuf0, ybuf1, wgu_bf, wd_bf, gsem, ssem, *, ff, n_real):
    b = pl.program_id(0)
    nu = nu_ref[0]
    xbufs, ybufs = (xbuf0, xbuf1), (ybuf0, ybuf1)
    rows = MOE_BLOCK

    def gather(ids, par):
        for i in range(rows):
            pltpu.make_async_copy(hm_hbm.at[pl.ds(ids[0, i], 1), :],
                                  xbufs[par].at[pl.ds(i, 1), :], gsem.at[par]).start()

    def scatter(ids, par):
        for i in range(rows):
            pltpu.make_async_copy(ybufs[par].at[pl.ds(i, 1), :],
                                  y_hbm.at[pl.ds(ids[0, i], 1), :], ssem.at[par]).start()

    def wait_gather(par):
        pltpu.make_async_copy(hm_hbm.at[pl.ds(0, rows), :], xbufs[par], gsem.at[par]).wait()

    def wait_scatter(par):
        pltpu.make_async_copy(ybufs[par], y_hbm.at[pl.ds(0, rows), :], ssem.at[par]).wait()

    def block_step(par, first):
        if first:
            for k in range(2):
                ybufs[k][...] = jnp.zeros_like(ybufs[k])
                spare = y_hbm.at[pl.ds(n_real + k * rows, rows), :]
                pltpu.make_async_copy(ybufs[k], spare, ssem.at[k]).start()
            for k in range(2):
                wait_scatter(k)
            gather(src_cur, par)
        wait_gather(par)

        @pl.when((b == 0) | (be_ref[b] != be_ref[jnp.maximum(b - 1, 0)]))
        def _():
            wgu_bf[...] = wgu_ref[...].astype(BF16)
            wd_bf[...] = wd_ref[...].astype(BF16)

        gather(src_next, 1 - par)
        if not first:
            scatter(dst_prev, 1 - par)
        y = _swiglu(_unpack_pairs_bf16(xbufs[par][...]), wgu_bf[...], wd_bf[...], ff)
        if not first:
            @pl.when(b >= 2)
            def _():
                wait_scatter(par)
        ybufs[par][...] = _pack_pairs(y)

    @pl.when(b == 0)
    def _():
        block_step(0, True)

    for par in range(2):
        @pl.when((b > 0) & (b < nu) & (b % 2 == par))
        def _(par=par):
            block_step(par, False)

        @pl.when((b == nu) & (b % 2 == par))
        def _(par=par):
            wait_gather(par)
            scatter(dst_prev, 1 - par)

            @pl.when(b >= 2)
            def _():
                wait_scatter(par)

            wait_scatter(1 - par)


def _experts(hm, src_ids, dst_ids, block_e, n_used, w_gu, w_down, layer, n_real):
    dh = hm.shape[1]
    d = 2 * dh
    ff = w_down.shape[2]
    n_blocks = src_ids.shape[0]

    def blk(b, be, nu):
        return jnp.minimum(b, nu[0] - 1)

    def ids_spec(index):
        return pl.BlockSpec((None, 1, MOE_BLOCK), lambda b, be, nu: (index(b), 0, 0),
                            memory_space=pltpu.SMEM)

    row_buf = pltpu.VMEM((MOE_BLOCK, dh), jnp.uint32)
    return pl.pallas_call(
        functools.partial(_expert_kernel, ff=ff, n_real=n_real),
        grid_spec=pltpu.PrefetchScalarGridSpec(
            num_scalar_prefetch=2,
            grid=(n_blocks,),
            in_specs=[
                ids_spec(lambda b: b),
                ids_spec(lambda b: jnp.minimum(b + 1, n_blocks - 1)),
                ids_spec(lambda b: jnp.maximum(b - 1, 0)),
                pl.BlockSpec(memory_space=pl.ANY),
                pl.BlockSpec((None, None, d, 2 * ff),
                             lambda b, be, nu: (layer, be[blk(b, be, nu)], 0, 0)),
                pl.BlockSpec((None, None, ff, d),
                             lambda b, be, nu: (layer, be[blk(b, be, nu)], 0, 0)),
            ],
            out_specs=pl.BlockSpec(memory_space=pl.ANY),
            scratch_shapes=[row_buf, row_buf, row_buf, row_buf,
                            pltpu.VMEM((d, 2 * ff), BF16), pltpu.VMEM((ff, d), BF16),
                            pltpu.SemaphoreType.DMA((2,)), pltpu.SemaphoreType.DMA((2,))],
        ),
        out_shape=jax.ShapeDtypeStruct((n_real + 2 * MOE_BLOCK, dh), jnp.uint32),
        compiler_params=_cparams(1),
        name="moe_experts",
    )(block_e, n_used, src_ids, src_ids, dst_ids, hm, w_gu, w_down)


def _combine_kernel(*refs, alpha, n_p, with_h, ff):
    yk_refs = refs[:TOP_K]
    w8_ref, hm_ref, wgu_ref, wd_ref, x_ref, gate_ref, g_ref, b_ref = refs[TOP_K:TOP_K + 8]
    rest = list(refs[TOP_K + 8:])
    wd_bf = rest.pop()
    wgu_bf = rest.pop()

    @pl.when(pl.program_id(0) == 0)
    def _():
        wgu_bf[...] = wgu_ref[...].astype(BF16)
        wd_bf[...] = wd_ref[...].astype(BF16)

    y_sh = _swiglu(_unpack_pairs_bf16(hm_ref[...]), wgu_bf[...], wd_bf[...], ff)
    w8 = w8_ref[...]
    half = yk_refs[0].shape[-1]
    acc_lo = y_sh[:, :half]
    acc_hi = y_sh[:, half:]
    for kk in range(TOP_K):
        rows = yk_refs[kk][...]
        wk = w8[:, kk:kk + 1]
        acc_lo = acc_lo + wk * _unpack_lo(rows)
        acc_hi = acc_hi + wk * _unpack_hi(rows)
    acc = jnp.concatenate([acc_lo, acc_hi], axis=-1)
    y = alpha * x_ref[...] + gate_ref[...] * acc
    xn = _layer_norm(y, g_ref[...], b_ref[...])
    if with_h:
        sh_ref, sc_ref, xo_ref, ho_ref = rest
        xo_ref[...] = xn
        ho_ref[...] = (xn * (1.0 + sc_ref[...]) + sh_ref[...]).astype(ho_ref.dtype)
    else:
        xp_ref, xs_ref = rest
        i = pl.program_id(0)

        @pl.when(i < n_p)
        def _():
            xp_ref[...] = xn

        @pl.when(i >= n_p)
        def _():
            xs_ref[...] = xn


def _combine(yk, w8, hm, ws_gu, ws_down, layer, x, gate, ln_g, ln_b, shift, scale, alpha, tp,
             seq_s):
    t, d = x.shape
    ff = ws_down.shape[1]
    tm = _row_tile(256, tp, seq_s)
    n_steps = t // tm
    n_p = tp // tm
    with_h = shift is not None
    cid = _cond_index(tm, tp, seq_s)
    row = pl.BlockSpec((tm, d), lambda i: (i, 0))
    packed_row = pl.BlockSpec((tm, d // 2), lambda i: (i, 0))
    mod_spec = pl.BlockSpec((None, 1, d), lambda i: (cid(i), 0, 0))
    vec = pl.BlockSpec((1, d), lambda i: (0, 0))
    resident = pl.Buffered(1)
    in_specs = [pl.BlockSpec((tm, d // 2), lambda i, kk=kk: (kk * n_steps + i, 0))
                for kk in range(TOP_K)]
    in_specs += [pl.BlockSpec((tm, TOP_K), lambda i: (i, 0)), packed_row,
                 pl.BlockSpec((None, d, 2 * ff), lambda i: (layer, 0, 0), pipeline_mode=resident),
                 pl.BlockSpec((None, ff, d), lambda i: (layer, 0, 0), pipeline_mode=resident),
                 row, mod_spec, vec, vec]
    args = [yk] * TOP_K + [w8, hm, ws_gu, ws_down, x, gate, ln_g.reshape(1, d),
                           ln_b.reshape(1, d)]
    if with_h:
        in_specs += [mod_spec, mod_spec]
        args += [shift, scale]
        out_specs = [row, row]
        out_shape = [jax.ShapeDtypeStruct((t, d), F32), jax.ShapeDtypeStruct((t, d), BF16)]
    else:
        out_specs = [pl.BlockSpec((tm, d), lambda i: (jnp.minimum(i, n_p - 1), 0)),
                     pl.BlockSpec((tm, d), lambda i: (jnp.maximum(i - n_p, 0), 0))]
        out_shape = [jax.ShapeDtypeStruct((tp, d), F32), jax.ShapeDtypeStruct((t - tp, d), F32)]
    return pl.pallas_call(
        functools.partial(_combine_kernel, alpha=alpha, n_p=n_p, with_h=with_h, ff=ff),
        grid=(n_steps,),
        in_specs=in_specs,
        out_specs=out_specs,
        out_shape=out_shape,
        scratch_shapes=[pltpu.VMEM((d, 2 * ff), BF16), pltpu.VMEM((ff, d), BF16)],
        compiler_params=_cparams(1),
        name="moe_combine",
    )(*args)


def _expert_row_order(n_exp):
    per = n_exp // N_GROUPS
    r = jnp.arange(n_exp)
    return (r % N_GROUPS) * per + r // N_GROUPS


def _count_le(edges, v):
    return jnp.sum((edges[None, :] <= v[:, None]).astype(jnp.int32), axis=1)


def _moe_layer(x, mixer_out, layer, pre, w_r, b_r, w_gu, w_down, ws_gu, ws_down, gate, ln_g, ln_b,
               shift, scale, alpha, tp, seq_s):
    t = x.shape[0]
    n_exp = w_r.shape[1]
    order = _expert_row_order(n_exp)
    x, hm, sel, wd, pos, cnt = _norm_route(x, mixer_out, *pre, w_r.T[order],
                                           b_r[order].reshape(n_exp, 1), alpha, tp, seq_s)

    assert (t * TOP_K) % MOE_BLOCK == 0
    n_blocks = t * TOP_K // MOE_BLOCK + n_exp
    cap = n_blocks * MOE_BLOCK
    counts = cnt[:, 0].astype(jnp.int32)
    padded = (counts + MOE_BLOCK - 1) // MOE_BLOCK * MOE_BLOCK
    pend = jnp.cumsum(padded)
    pstart = pend - padded
    n_used = (pend[-1:] // MOE_BLOCK).astype(jnp.int32)
    block_row = jnp.minimum(_count_le(pend, jnp.arange(n_blocks) * MOE_BLOCK), n_exp - 1)
    per = n_exp // N_GROUPS
    block_e = ((block_row % N_GROUPS) * per + block_row // N_GROUPS).astype(jnp.int32)
    dest8, w8 = _slots(sel, wd, pos, pstart.astype(F32).reshape(n_exp, 1))

    padlen = padded - counts
    cpad_start = jnp.cumsum(padlen) - padlen
    f = jnp.arange(cap - t * TOP_K)
    fill = f + jnp.sum(jnp.where(cpad_start[None, :] <= f[:, None], counts[None, :], 0), axis=1)
    position = jnp.concatenate([dest8.reshape(-1), fill.astype(jnp.int32)])
    assign = jnp.argsort(position).astype(jnp.int32)
    src_ids = (assign % t).reshape(n_blocks, 1, MOE_BLOCK)
    n_real = t * TOP_K
    spare = n_real + jnp.arange(cap, dtype=jnp.int32) % (2 * MOE_BLOCK)
    dst_ids = jnp.where(assign >= n_real, spare, assign).reshape(n_blocks, 1, MOE_BLOCK)

    yk = _experts(hm, src_ids, dst_ids, block_e, n_used, w_gu, w_down, layer, n_real)
    return _combine(yk, w8.T, hm, ws_gu, ws_down, layer, x, gate, ln_g, ln_b, shift, scale, alpha,
                    tp, seq_s)


def _gla_layer(h, state_gla, new_state, layer_j, w_in, w_a1, w_a2, b_a, norm_g, w_out, dims):
    bp, sp, bs, ss = dims
    tp = bp * sp
    d = h.shape[1]
    rank = w_a1.shape[3]
    proj = _matmul(h, w_in, layer_j, BF16)
    w_low = jnp.transpose(w_a1[layer_j], (1, 0, 2)).reshape(1, d, 2 * rank)
    low = _matmul(h, w_low, 0, F32)
    low = jnp.transpose(low.reshape(-1, 2, rank), (1, 0, 2))
    og_p, st = _gla_scan(proj, low, w_a2[layer_j], b_a[layer_j], norm_g[layer_j], None, layer_j,
                         batch=bp, seq_len=sp, row0=0, hp=GLA_HEADS,
                         new_state="first" if new_state is None else new_state,
                         n_state_layers=w_in.shape[0])
    (og_s,) = _gla_scan(proj, low, w_a2[layer_j], b_a[layer_j], norm_g[layer_j], state_gla,
                        layer_j, batch=bs, seq_len=ss, row0=tp, hp=GLA_HEADS // 2)
    out = _matmul((og_p, og_s), w_out, layer_j, BF16)
    return out, st


def _sgu_layer(h, layer_j, w_in, ln_g, ln_b, w_s, b_s, w_out):
    z = _matmul(h, w_in, layer_j, BF16, act="gelu")
    gated = _sgu_gate(z, ln_g[layer_j], ln_b[layer_j], w_s[layer_j], b_s[layer_j])
    return _matmul(gated, w_out, layer_j, BF16)


def _grid_pos_embed(rows, d):
    t = jnp.arange(rows * GRID_W)
    r = (t // GRID_W).astype(F32)
    col = (t % GRID_W).astype(F32)
    quarter = d // 4
    omega = 1.0 / (10000.0 ** (jnp.arange(quarter, dtype=F32) / quarter))

    def emb(p):
        a = p[:, None] * omega[None, :]
        return jnp.concatenate([jnp.sin(a), jnp.cos(a)], axis=-1)

    return jnp.concatenate([emb(r), emb(col)], axis=-1)


def kernel(x_prompt, x_sample, state_gla, c, c_ctx, w_ada, b_ada, ln_g, ln_b, gla_w_in, gla_w_a1, gla_w_a2, gla_b_a, gla_norm_g, gla_w_out, sgu_w_in, sgu_ln_g, sgu_ln_b, sgu_w_s, sgu_b_s, sgu_w_out, moe_w_router, moe_b_router, moe_w_gu, moe_w_down, moe_ws_gu, moe_ws_down):
    bp, sp, d = x_prompt.shape
    bs, ss, _ = x_sample.shape
    assert 1 + bs <= COND_ROWS
    tp, ts = bp * sp, bs * ss
    depth = w_ada.shape[0]
    alpha = (2 * depth) ** 0.25
    dims = (bp, sp, bs, ss)

    cond = jnp.zeros((COND_ROWS, d), F32).at[0].set(c_ctx).at[1:1 + bs].set(c)
    mods = _ada_mods(cond, w_ada, b_ada).reshape(depth, COND_ROWS, 6, 1, d)

    def mod(l, i):
        return mods[l, :, i]

    pos = _grid_pos_embed(ss // GRID_W, d).astype(x_sample.dtype)
    x, h = _prep(x_prompt.reshape(tp, d), x_sample.reshape(ts, d), pos, mod(0, 0), mod(0, 1), ss)

    new_state = None
    for l in range(depth):
        j = l // 2
        if l % 2 == 0:
            out, new_state = _gla_layer(h, state_gla, new_state, j, gla_w_in, gla_w_a1, gla_w_a2,
                                        gla_b_a, gla_norm_g, gla_w_out, dims)
        else:
            out = _sgu_layer(h, j, sgu_w_in, sgu_ln_g, sgu_ln_b, sgu_w_s, sgu_b_s, sgu_w_out)
        pre = (mod(l, 2), ln_g[l, 0], ln_b[l, 0], mod(l, 3), mod(l, 4))
        last = l == depth - 1
        res = _moe_layer(x, out, l, pre, moe_w_router[l], moe_b_router[l], moe_w_gu, moe_w_down,
                         moe_ws_gu, moe_ws_down, mod(l, 5), ln_g[l, 1], ln_b[l, 1],
                         None if last else mod(l + 1, 0), None if last else mod(l + 1, 1),
                         alpha, tp, ss)
        if last:
            y_prompt, y_sample = res
        else:
            x, h = res

    return y_prompt.reshape(bp, sp, d), y_sample.reshape(bs, ss, d), new_state
```

```python
import functools

import jax
import jax.numpy as jnp
from jax import lax
from jax.experimental import pallas as pl
from jax.experimental.pallas import tpu as pltpu

F32 = jnp.float32
BF16 = jnp.bfloat16

V7X_VMEM_LIMIT_BYTES = 56 * 1024 * 1024
LANES = 128
SUBLANES = 8

GRID_W = 64
GLA_HEADS = 4
GLA_CHUNK = 64
GLA_NORMALIZER = 16.0
SGU_GROUPS = 4
SGU_CHUNK = 128
TOP_K = 8
N_GROUPS = 8
TOPK_GROUPS = 4
ROUTE_SCALE = 2.5
LN_EPS = 1e-5
RMS_EPS = 1e-6
MOE_BLOCK = 256
COND_ROWS = 8


def _cparams(n_axes):
    return pltpu.CompilerParams(
        dimension_semantics=("arbitrary",) * n_axes,
        vmem_limit_bytes=V7X_VMEM_LIMIT_BYTES,
    )


def _silu(x):
    return x * jax.nn.sigmoid(x)


def _layer_norm(y, g, b):
    mu = jnp.mean(y, axis=-1, keepdims=True)
    d = y - mu
    var = jnp.mean(d * d, axis=-1, keepdims=True)
    return d * lax.rsqrt(var + LN_EPS) * g + b


_HI16 = 0xFFFF0000


def _pack_pairs(x):
    half = x.shape[-1] // 2
    lo = lax.bitcast_convert_type(x[:, :half].astype(BF16).astype(F32), jnp.uint32)
    hi = lax.bitcast_convert_type(x[:, half:].astype(BF16).astype(F32), jnp.uint32)
    return lax.shift_right_logical(lo, jnp.uint32(16)) | (hi & jnp.uint32(_HI16))


def _unpack_lo(w):
    return lax.bitcast_convert_type(lax.shift_left(w, jnp.uint32(16)), F32)


def _unpack_hi(w):
    return lax.bitcast_convert_type(w & jnp.uint32(_HI16), F32)


def _unpack_pairs_bf16(w):
    return jnp.concatenate([_unpack_lo(w).astype(BF16), _unpack_hi(w).astype(BF16)], axis=-1)


def _row_tile(limit, *sizes):
    tm = limit
    while any(s % tm for s in sizes):
        tm //= 2
    return tm


def _cond_index(tm, tp, seq_s):
    def index(i):
        return jnp.where(i * tm < tp, 0, 1 + (i * tm - tp) // seq_s)

    return index


def _ada_kernel(c_ref, w_ref, b_ref, o_ref):
    s = _silu(c_ref[...]).astype(BF16)
    o_ref[...] = jnp.dot(s, w_ref[...].astype(BF16), preferred_element_type=F32) + b_ref[...]


def _ada_mods(cond, w_ada, b_ada):
    depth, d, n = w_ada.shape
    tn = _row_tile(1024, n)
    return pl.pallas_call(
        _ada_kernel,
        grid=(depth, n // tn),
        in_specs=[
            pl.BlockSpec((COND_ROWS, d), lambda l, j: (0, 0)),
            pl.BlockSpec((None, d, tn), lambda l, j: (l, 0, j)),
            pl.BlockSpec((None, 1, tn), lambda l, j: (l, 0, j)),
        ],
        out_specs=pl.BlockSpec((None, COND_ROWS, tn), lambda l, j: (l, 0, j)),
        out_shape=jax.ShapeDtypeStruct((depth, COND_ROWS, n), F32),
        compiler_params=_cparams(2),
        name="ada_mods",
    )(cond, w_ada, b_ada.reshape(depth, 1, n))


def _prep_kernel(xp_ref, xs_ref, pos_ref, sh_ref, sc_ref, x_ref, h_ref, *, n_p):
    x = jnp.where(pl.program_id(0) < n_p, xp_ref[...], xs_ref[...] + pos_ref[...])
    x_ref[...] = x
    h_ref[...] = (x * (1.0 + sc_ref[...]) + sh_ref[...]).astype(h_ref.dtype)


def _prep(xp, xs, pos, shift, scale, seq_s):
    tp, d = xp.shape
    ts = xs.shape[0]
    tm = _row_tile(512, tp, seq_s)
    n_p = tp // tm
    n_pos = seq_s // tm
    cid = _cond_index(tm, tp, seq_s)
    mod_spec = pl.BlockSpec((None, 1, d), lambda i: (cid(i), 0, 0))
    return pl.pallas_call(
        functools.partial(_prep_kernel, n_p=n_p),
        grid=((tp + ts) // tm,),
        in_specs=[
            pl.BlockSpec((tm, d), lambda i: (jnp.minimum(i, n_p - 1), 0)),
            pl.BlockSpec((tm, d), lambda i: (jnp.maximum(i - n_p, 0), 0)),
            pl.BlockSpec((tm, d), lambda i: (jnp.maximum(i - n_p, 0) % n_pos, 0)),
            mod_spec,
            mod_spec,
        ],
        out_specs=[pl.BlockSpec((tm, d), lambda i: (i, 0))] * 2,
        out_shape=[
            jax.ShapeDtypeStruct((tp + ts, d), F32),
            jax.ShapeDtypeStruct((tp + ts, d), BF16),
        ],
        compiler_params=_cparams(1),
        name="prep",
    )(xp, xs, pos, shift, scale)


def _mm_kernel(*refs, act, n_first):
    *x_refs, w_ref, o_ref, wbf_ref = refs

    @pl.when(pl.program_id(1) == 0)
    def _():
        wbf_ref[...] = w_ref[...].astype(BF16)

    x = x_refs[0][...]
    if len(x_refs) == 2:
        x = jnp.where(pl.program_id(1) < n_first, x, x_refs[1][...])
    acc = jnp.dot(x.astype(BF16), wbf_ref[...], preferred_element_type=F32)
    if act == "gelu":
        acc = 0.5 * acc * (1.0 + lax.erf(acc * 0.7071067811865476))
    o_ref[...] = acc.astype(o_ref.dtype)


def _matmul(x, w, layer, out_dtype, act=None):
    xs = x if isinstance(x, tuple) else (x,)
    x = xs[0]
    m = sum(a.shape[0] for a in xs)
    k = x.shape[1]
    n = w.shape[2]
    tn_limit = 1024
    while k * tn_limit * 10 > V7X_VMEM_LIMIT_BYTES * 4 // 7 and tn_limit > LANES:
        tn_limit //= 2
    tn = _row_tile(tn_limit, n)
    out_bytes = jnp.dtype(out_dtype).itemsize
    tm_limit = 1024
    while (k * tn * 10
           + tm_limit * (2 * len(xs) * k * x.dtype.itemsize + tn * (2 * out_bytes + 4))
           > V7X_VMEM_LIMIT_BYTES * 7 // 8 and tm_limit > SUBLANES):
        tm_limit //= 2
    tm = _row_tile(tm_limit, *(a.shape[0] for a in xs))
    n_first = xs[0].shape[0] // tm
    x_specs = [pl.BlockSpec((tm, k), lambda j, i: (jnp.minimum(i, n_first - 1), 0))]
    if len(xs) == 2:
        x_specs.append(pl.BlockSpec((tm, k), lambda j, i: (jnp.maximum(i - n_first, 0), 0)))
    return pl.pallas_call(
        functools.partial(_mm_kernel, act=act, n_first=n_first),
        grid=(n // tn, m // tm),
        in_specs=x_specs + [pl.BlockSpec((None, k, tn), lambda j, i: (layer, 0, j))],
        out_specs=pl.BlockSpec((tm, tn), lambda j, i: (i, j)),
        out_shape=jax.ShapeDtypeStruct((m, n), out_dtype),
        scratch_shapes=[pltpu.VMEM((k, tn), BF16)],
        compiler_params=_cparams(2),
        name="matmul_" + (act or "lin"),
    )(*xs, w)


def _prefix_sum_mm(tri, x):
    hi = x.astype(BF16)
    r1 = x - hi.astype(F32)
    mid = r1.astype(BF16)
    lo = (r1 - mid.astype(F32)).astype(BF16)
    out = jnp.dot(tri, hi, preferred_element_type=F32)
    out = out + jnp.dot(tri, mid, preferred_element_type=F32)
    return out + jnp.dot(tri, lo, preferred_element_type=F32)


def _gla_kernel(*refs, seq_len, hp, has_s0, out_state, layer_j):
    q_ref, k_ref, v_ref, g_ref, low_ref, wa2_ref, ba_ref, ng_ref = refs[:8]
    rest = list(refs[8:])
    s0_ref = rest.pop(0) if has_s0 else None
    if out_state == "next":
        rest.pop(0)
    o_ref = rest.pop(0)
    st_ref = rest.pop(0) if out_state else None
    gk_scr, oacc, s_t = rest

    c = GLA_CHUNK
    n = seq_len // c
    dk = q_ref.shape[1] // hp
    dv = v_ref.shape[1] // hp
    scale = dk ** -0.5

    for z in range(2):
        pre = jnp.dot(low_ref[z].astype(BF16), wa2_ref[z].astype(BF16),
                      preferred_element_type=F32) + ba_ref[z]
        log_sig = jnp.minimum(pre, 0.0) - jnp.log(1.0 + jnp.exp(-jnp.abs(pre)))
        gk_scr[z] = log_sig * (1.0 / GLA_NORMALIZER)
        for hh in range(hp):
            s_t[z, hh] = s0_ref[z, hh].T if has_s0 else jnp.zeros((dv, dk), F32)

    row = lax.broadcasted_iota(jnp.int32, (c, c), 0)
    col = lax.broadcasted_iota(jnp.int32, (c, c), 1)
    masks = (col <= row, col >= row)
    tris = tuple(jnp.where(m, 1.0, 0.0).astype(BF16) for m in masks)
    nt_dims = (((1,), (1,)), ((), ()))
    tn_dims = (((0,), (0,)), ((), ()))

    n_stages = 5

    def chunk_stages(ci, z, hh):
        rows = pl.ds(pl.multiple_of(ci * c, c), c)
        ck = slice(hh * dk, (hh + 1) * dk)
        cv = slice(hh * dv, (hh + 1) * dv)
        gcum = _prefix_sum_mm(tris[z], gk_scr[z, rows, ck])
        yield
        qc = q_ref[rows, ck].astype(F32) * scale
        kc = k_ref[rows, ck].astype(F32)
        if z == 0:
            g_mid = gcum[c // 2 - 1:c // 2]
            g_last = gcum[c - 1:c]
        else:
            g_mid = gcum[c // 2:c // 2 + 1]
            g_last = gcum[0:1]
        q_in = (qc * jnp.exp(gcum - g_mid)).astype(BF16)
        k_in = (kc * jnp.exp(g_mid - gcum)).astype(BF16)
        a = lax.dot_general(q_in, k_in, nt_dims, preferred_element_type=F32)
        yield
        vc = v_ref[rows, cv].astype(BF16)
        a = jnp.where(masks[z], a, 0.0).astype(BF16)
        o = jnp.dot(a, vc, preferred_element_type=F32)
        yield
        s_prev = s_t[z, hh]
        q_dec = (qc * jnp.exp(gcum)).astype(BF16)
        o = o + lax.dot_general(q_dec, s_prev.astype(BF16), nt_dims, preferred_element_type=F32)
        oacc[z, rows, cv] = o
        yield
        k_dec = (kc * jnp.exp(g_last - gcum)).astype(BF16)
        u_t = lax.dot_general(vc, k_dec, tn_dims, preferred_element_type=F32)
        s_t[z, hh] = s_prev * jnp.exp(g_last) + u_t
        yield

    def body(i, carry):
        scans = [chunk_stages(i if z == 0 else n - 1 - i, z, hh)
                 for z in range(2) for hh in range(hp)]
        for _ in range(n_stages):
            for scan in scans:
                next(scan)
        return carry

    lax.fori_loop(0, n, body, 0)

    for hh in range(hp):
        cv = slice(hh * dv, (hh + 1) * dv)
        for z in range(2):
            if out_state == "next":
                st_ref[z, hh] = s_t[z, hh].T
            elif out_state == "first":
                for j in range(st_ref.shape[0]):
                    st_ref[j, z, hh] = s_t[z, hh].T if j == layer_j else jnp.zeros((dk, dv), F32)
        o = oacc[0, :, cv] + oacc[1, :, cv]
        on = o * lax.rsqrt(jnp.mean(o * o, axis=-1, keepdims=True) + RMS_EPS) * ng_ref[...]
        o_ref[:, cv] = (on * _silu(g_ref[:, cv].astype(F32))).astype(o_ref.dtype)


def _gla_scan(proj, low, w_a2, b_a, norm_g, state, layer_j, *, batch, seq_len, row0, hp,
              new_state=None, n_state_layers=1):
    h = GLA_HEADS
    hk = w_a2.shape[2]
    dk = hk // h
    hv = (proj.shape[1] - 2 * hk) // 2
    dv = hv // h
    rank = w_a2.shape[1]
    off = row0 // seq_len
    has_s0 = state is not None
    wk, wv = hp * dk, hp * dv
    in_specs = [
        pl.BlockSpec((seq_len, wk), lambda b, i: (b + off, i)),
        pl.BlockSpec((seq_len, wk), lambda b, i: (b + off, hk // wk + i)),
        pl.BlockSpec((seq_len, wv), lambda b, i: (b + off, 2 * hk // wv + i)),
        pl.BlockSpec((seq_len, wv), lambda b, i: (b + off, (2 * hk + hv) // wv + i)),
        pl.BlockSpec((2, seq_len, rank), lambda b, i: (0, b + off, 0)),
        pl.BlockSpec((2, rank, wk), lambda b, i: (0, 0, i)),
        pl.BlockSpec((2, 1, wk), lambda b, i: (0, 0, i)),
        pl.BlockSpec((1, dv), lambda b, i: (0, 0)),
    ]
    args = [proj, proj, proj, proj, low, w_a2, b_a.reshape(2, 1, hk), norm_g.reshape(1, dv)]
    if has_s0:
        in_specs.append(pl.BlockSpec((None, None, 2, hp, dk, dv),
                                     lambda b, i: (b, layer_j, 0, i, 0, 0)))
        args.append(state)
    out_specs = [pl.BlockSpec((seq_len, wv), lambda b, i: (b, i))]
    out_shape = [jax.ShapeDtypeStruct((batch * seq_len, hv), BF16)]
    aliases = {}
    out_state = None
    if isinstance(new_state, str):
        out_state = "first"
        out_specs.append(pl.BlockSpec((None, n_state_layers, 2, hp, dk, dv),
                                      lambda b, i: (b, 0, 0, i, 0, 0)))
        out_shape.append(jax.ShapeDtypeStruct((batch, n_state_layers, 2, h, dk, dv), F32))
    elif new_state is not None:
        out_state = "next"
        aliases = {len(args): 1}
        in_specs.append(pl.BlockSpec(memory_space=pl.ANY))
        args.append(new_state)
        out_specs.append(pl.BlockSpec((None, None, 2, hp, dk, dv),
                                      lambda b, i: (b, layer_j, 0, i, 0, 0)))
        out_shape.append(jax.ShapeDtypeStruct(new_state.shape, F32))
    return pl.pallas_call(
        functools.partial(_gla_kernel, seq_len=seq_len, hp=hp, has_s0=has_s0,
                          out_state=out_state, layer_j=layer_j),
        grid=(batch, h // hp),
        in_specs=in_specs,
        out_specs=out_specs,
        out_shape=out_shape,
        input_output_aliases=aliases,
        scratch_shapes=[
            pltpu.VMEM((2, seq_len, wk), F32),
            pltpu.VMEM((2, seq_len, wv), F32),
            pltpu.VMEM((2, hp, dv, dk), F32),
        ],
        compiler_params=_cparams(2),
        name="gla_scan",
    )(*args)


def _sgu_kernel(u_ref, v_ref, lg_ref, lb_ref, ws_ref, bs_ref, o_ref):
    v = v_ref[...].astype(F32)
    vn = _layer_norm(v, lg_ref[...], lb_ref[...]).astype(BF16)
    rows, half = vn.shape
    cg = half // SGU_GROUPS
    for r in range(rows // SGU_CHUNK):
        rs = slice(r * SGU_CHUNK, (r + 1) * SGU_CHUNK)
        for g in range(SGU_GROUPS):
            cs = slice(g * cg, (g + 1) * cg)
            s = jnp.dot(ws_ref[g].astype(BF16), vn[rs, cs], preferred_element_type=F32) + bs_ref[g]
            o_ref[rs, cs] = (u_ref[rs, cs].astype(F32) * s).astype(o_ref.dtype)


def _sgu_gate(z, ln_g, ln_b, w_s, b_s):
    t, ffn = z.shape
    half = ffn // 2
    rows = SGU_CHUNK
    return pl.pallas_call(
        _sgu_kernel,
        grid=(t // rows,),
        in_specs=[
            pl.BlockSpec((rows, half), lambda i: (i, 0)),
            pl.BlockSpec((rows, half), lambda i: (i, 1)),
            pl.BlockSpec((1, half), lambda i: (0, 0)),
            pl.BlockSpec((1, half), lambda i: (0, 0)),
            pl.BlockSpec((SGU_GROUPS, SGU_CHUNK, SGU_CHUNK), lambda i: (0, 0, 0)),
            pl.BlockSpec((SGU_GROUPS, SGU_CHUNK, 1), lambda i: (0, 0, 0)),
        ],
        out_specs=pl.BlockSpec((rows, half), lambda i: (i, 0)),
        out_shape=jax.ShapeDtypeStruct((t, half), BF16),
        compiler_params=_cparams(1),
        name="sgu_gate",
    )(z, z, ln_g.reshape(1, half), ln_b.reshape(1, half), w_s, b_s.reshape(SGU_GROUPS, SGU_CHUNK, 1))


def _norm_route_kernel(x_ref, o_ref, gate_ref, g_ref, b_ref, sh_ref, sc_ref, wrt_ref, br_ref,
                       xo_ref, ho_ref, sel_ref, wd_ref, pos_ref, cnt_ref, carry, *, alpha):
    i = pl.program_id(0)

    @pl.when(i == 0)
    def _():
        carry[...] = jnp.zeros_like(carry)

    y = alpha * x_ref[...] + gate_ref[...] * o_ref[...].astype(F32)
    xn = _layer_norm(y, g_ref[...], b_ref[...])
    xo_ref[...] = xn
    h = xn * (1.0 + sc_ref[...]) + sh_ref[...]
    ho_ref[...] = _pack_pairs(h)

    x = h.astype(BF16)
    logits = lax.dot_general(wrt_ref[...].astype(BF16), x, (((1,), (1,)), ((), ())),
                             preferred_element_type=F32)
    n_exp, tm = logits.shape
    ng = N_GROUPS
    per = n_exp // ng
    scores = jax.nn.sigmoid(logits)
    selv = scores + br_ref[...]
    a = [selv[j * ng:(j + 1) * ng] for j in range(per)]
    sc = [scores[j * ng:(j + 1) * ng] for j in range(per)]

    t1 = a[0]
    t2 = jnp.full_like(t1, -jnp.inf)
    for j in range(1, per):
        t2 = jnp.maximum(t2, jnp.minimum(t1, a[j]))
        t1 = jnp.maximum(t1, a[j])
    gscore = t1 + t2

    gidx = lax.broadcasted_iota(jnp.int32, (ng, tm), 0)
    grank = jnp.zeros((ng, tm), jnp.int32)
    for gp in range(ng):
        other = gscore[gp:gp + 1, :]
        beats = (other > gscore) | ((other == gscore) & (gp < gidx))
        grank = grank + beats.astype(jnp.int32)
    gmask = grank < TOPK_GROUPS

    val = [jnp.where(gmask, a[j], -jnp.inf) for j in range(per)]
    ranks = [jnp.zeros((ng, tm), jnp.int32) for _ in range(per)]
    for gp in range(ng):
        lower_group = gp < gidx
        lower_or_same_group = gp <= gidx
        for jp in range(per):
            other = val[jp][gp:gp + 1, :]
            for j in range(per):
                tie = lower_or_same_group if jp < j else lower_group
                beats = (other > val[j]) | ((other == val[j]) & tie)
                ranks[j] = ranks[j] + beats.astype(jnp.int32)
    selm = [(ranks[j] < TOP_K) & gmask for j in range(per)]

    ssum = jnp.zeros((ng, tm), F32)
    for j in range(per):
        ssum = ssum + jnp.where(selm[j], sc[j], 0.0)
    ssum = jnp.sum(ssum, axis=0, keepdims=True)
    wd = jnp.concatenate(
        [jnp.where(selm[j], sc[j] / ssum * ROUTE_SCALE, 0.0) for j in range(per)], axis=0)
    selmat = jnp.concatenate([jnp.where(selm[j], 1.0, 0.0) for j in range(per)], axis=0)

    s_i = lax.broadcasted_iota(jnp.int32, (tm, tm), 0)
    t_i = lax.broadcasted_iota(jnp.int32, (tm, tm), 1)
    upper = jnp.where(s_i < t_i, 1.0, 0.0).astype(BF16)
    base = carry[...][:, 0:1]
    pos = jnp.dot(selmat.astype(BF16), upper, preferred_element_type=F32) + base
    total = base + jnp.sum(selmat, axis=1, keepdims=True)
    carry[...] = jnp.broadcast_to(total, carry.shape)

    sel_ref[...] = selmat
    wd_ref[...] = wd
    pos_ref[...] = pos
    cnt_ref[...] = jnp.broadcast_to(total, cnt_ref.shape)


def _norm_route(x, out, gate, ln_g, ln_b, shift, scale, w_rt, b_r_col, alpha, tp, seq_s):
    t, d = x.shape
    n_exp = w_rt.shape[0]
    tm = _row_tile(512, tp, seq_s)
    cid = _cond_index(tm, tp, seq_s)
    row = pl.BlockSpec((tm, d), lambda i: (i, 0))
    mod_spec = pl.BlockSpec((None, 1, d), lambda i: (cid(i), 0, 0))
    vec = pl.BlockSpec((1, d), lambda i: (0, 0))
    col = pl.BlockSpec((n_exp, tm), lambda i: (0, i))
    return pl.pallas_call(
        functools.partial(_norm_route_kernel, alpha=alpha),
        grid=(t // tm,),
        in_specs=[row, row, mod_spec, vec, vec, mod_spec, mod_spec,
                  pl.BlockSpec((n_exp, d), lambda i: (0, 0)),
                  pl.BlockSpec((n_exp, 1), lambda i: (0, 0))],
        out_specs=[row, pl.BlockSpec((tm, d // 2), lambda i: (i, 0)), col, col, col,
                   pl.BlockSpec((n_exp, LANES), lambda i: (0, 0))],
        out_shape=[
            jax.ShapeDtypeStruct((t, d), F32),
            jax.ShapeDtypeStruct((t, d // 2), jnp.uint32),
            jax.ShapeDtypeStruct((n_exp, t), F32),
            jax.ShapeDtypeStruct((n_exp, t), F32),
            jax.ShapeDtypeStruct((n_exp, t), F32),
            jax.ShapeDtypeStruct((n_exp, LANES), F32),
        ],
        scratch_shapes=[pltpu.VMEM((n_exp, LANES), F32)],
        compiler_params=_cparams(1),
        name="norm_route",
    )(x, out, gate, ln_g.reshape(1, d), ln_b.reshape(1, d), shift, scale, w_rt, b_r_col)


def _slots_kernel(sel_ref, wd_ref, pos_ref, ps_ref, d8_ref, w8_ref):
    sel = sel_ref[...]
    n_exp = sel.shape[0]
    r_i = lax.broadcasted_iota(jnp.int32, (n_exp, n_exp), 0)
    c_i = lax.broadcasted_iota(jnp.int32, (n_exp, n_exp), 1)
    lower = jnp.where(c_i < r_i, 1.0, 0.0).astype(BF16)
    slot = jnp.dot(lower, sel.astype(BF16), preferred_element_type=F32)
    chosen = sel > 0.5
    dest = pos_ref[...] + ps_ref[...]
    wd = wd_ref[...]
    d_rows, w_rows = [], []
    for k in range(TOP_K):
        pick = chosen & (slot == k)
        d_rows.append(jnp.sum(jnp.where(pick, dest, 0.0), axis=0, keepdims=True))
        w_rows.append(jnp.sum(jnp.where(pick, wd, 0.0), axis=0, keepdims=True))
    d8_ref[...] = jnp.concatenate(d_rows, axis=0).astype(jnp.int32)
    w8_ref[...] = jnp.concatenate(w_rows, axis=0)


def _slots(sel, wd, pos, pstart_col):
    n_exp, t = sel.shape
    tm = _row_tile(512, t)
    col = pl.BlockSpec((n_exp, tm), lambda i: (0, i))
    out = pl.BlockSpec((TOP_K, tm), lambda i: (0, i))
    return pl.pallas_call(
        _slots_kernel,
        grid=(t // tm,),
        in_specs=[col, col, col, pl.BlockSpec((n_exp, 1), lambda i: (0, 0))],
        out_specs=[out, out],
        out_shape=[jax.ShapeDtypeStruct((TOP_K, t), jnp.int32),
                   jax.ShapeDtypeStruct((TOP_K, t), F32)],
        compiler_params=_cparams(1),
        name="moe_slots",
    )(sel, wd, pos, pstart_col)


def _swiglu(x, wgu, wd, ff):
    h = jnp.dot(x, wgu, preferred_element_type=F32)
    act = (_silu(h[:, :ff]) * h[:, ff:]).astype(BF16)
    return jnp.dot(act, wd, preferred_element_type=F32)


def _expert_kernel(be_ref, nu_ref, src_cur, src_next, dst_prev, hm_hbm, wgu_ref, wd_ref,
                   y_hbm, xbuf0, xbuf1, ybuf0, ybuf1, wgu_bf, wd_bf, gsem, ssem, *, ff, n_real):
    b = pl.program_id(0)
    nu = nu_ref[0]
    xbufs, ybufs = (xbuf0, xbuf1), (ybuf0, ybuf1)
    rows = MOE_BLOCK

    def gather_row(ids, par, i):
        pltpu.make_async_copy(hm_hbm.at[pl.ds(ids[0, i], 1), :],
                              xbufs[par].at[pl.ds(i, 1), :], gsem.at[par]).start()

    def scatter_row(ids, par, i):
        pltpu.make_async_copy(ybufs[par].at[pl.ds(i, 1), :],
                              y_hbm.at[pl.ds(ids[0, i], 1), :], ssem.at[par]).start()

    def gather(ids, par):
        for i in range(rows):
            gather_row(ids, par, i)

    def scatter(ids, par):
        for i in range(rows):
            scatter_row(ids, par, i)

    def wait_gather(par):
        pltpu.make_async_copy(hm_hbm.at[pl.ds(0, rows), :], xbufs[par], gsem.at[par]).wait()

    def wait_scatter(par):
        pltpu.make_async_copy(ybufs[par], y_hbm.at[pl.ds(0, rows), :], ssem.at[par]).wait()

    def block_step(par, first):
        if first:
            for k in range(2):
                ybufs[k][...] = jnp.zeros_like(ybufs[k])
                spare = y_hbm.at[pl.ds(n_real + k * rows, rows), :]
                pltpu.make_async_copy(ybufs[k], spare, ssem.at[k]).start()
            for k in range(2):
                wait_scatter(k)
            gather(src_cur, par)
        wait_gather(par)

        @pl.when((b == 0) | (be_ref[b] != be_ref[jnp.maximum(b - 1, 0)]))
        def _():
            wgu_bf[...] = wgu_ref[...].astype(BF16)
            wd_bf[...] = wd_ref[...].astype(BF16)

        if first:
            gather(src_next, 1 - par)
        else:
            for i in range(rows):
                gather_row(src_next, 1 - par, i)
                scatter_row(dst_prev, 1 - par, i)
        y = _swiglu(_unpack_pairs_bf16(xbufs[par][...]), wgu_bf[...], wd_bf[...], ff)
        if not first:
            @pl.when(b >= 2)
            def _():
                wait_scatter(par)
        ybufs[par][...] = _pack_pairs(y)

    @pl.when(b == 0)
    def _():
        block_step(0, True)

    for par in range(2):
        @pl.when((b > 0) & (b < nu) & (b % 2 == par))
        def _(par=par):
            block_step(par, False)

        @pl.when((b == nu) & (b % 2 == par))
        def _(par=par):
            wait_gather(par)
            scatter(dst_prev, 1 - par)

            @pl.when(b >= 2)
            def _():
                wait_scatter(par)

            wait_scatter(1 - par)


def _experts(hm, src_ids, dst_ids, block_e, n_used, w_gu, w_down, layer, n_real):
    dh = hm.shape[1]
    d = 2 * dh
    ff = w_down.shape[2]
    n_blocks = src_ids.shape[0]

    def blk(b, be, nu):
        return jnp.minimum(b, nu[0] - 1)

    def ids_spec(index):
        return pl.BlockSpec((None, 1, MOE_BLOCK), lambda b, be, nu: (index(b), 0, 0),
                            memory_space=pltpu.SMEM)

    row_buf = pltpu.VMEM((MOE_BLOCK, dh), jnp.uint32)
    return pl.pallas_call(
        functools.partial(_expert_kernel, ff=ff, n_real=n_real),
        grid_spec=pltpu.PrefetchScalarGridSpec(
            num_scalar_prefetch=2,
            grid=(n_blocks,),
            in_specs=[
                ids_spec(lambda b: b),
                ids_spec(lambda b: jnp.minimum(b + 1, n_blocks - 1)),
                ids_spec(lambda b: jnp.maximum(b - 1, 0)),
                pl.BlockSpec(memory_space=pl.ANY),
                pl.BlockSpec((None, None, d, 2 * ff),
                             lambda b, be, nu: (layer, be[blk(b, be, nu)], 0, 0)),
                pl.BlockSpec((None, None, ff, d),
                             lambda b, be, nu: (layer, be[blk(b, be, nu)], 0, 0)),
            ],
            out_specs=pl.BlockSpec(memory_space=pl.ANY),
            scratch_shapes=[row_buf, row_buf, row_buf, row_buf,
                            pltpu.VMEM((d, 2 * ff), BF16), pltpu.VMEM((ff, d), BF16),
                            pltpu.SemaphoreType.DMA((2,)), pltpu.SemaphoreType.DMA((2,))],
        ),
        out_shape=jax.ShapeDtypeStruct((n_real + 2 * MOE_BLOCK, dh), jnp.uint32),
        compiler_params=_cparams(1),
        name="moe_experts",
    )(block_e, n_used, src_ids, src_ids, dst_ids, hm, w_gu, w_down)


def _combine_kernel(*refs, alpha, n_p, with_h, ff):
    yk_refs = refs[:TOP_K]
    w8_ref, hm_ref, wgu_ref, wd_ref, x_ref, gate_ref, g_ref, b_ref = refs[TOP_K:TOP_K + 8]
    rest = list(refs[TOP_K + 8:])
    wd_bf = rest.pop()
    wgu_bf = rest.pop()

    @pl.when(pl.program_id(0) == 0)
    def _():
        wgu_bf[...] = wgu_ref[...].astype(BF16)
        wd_bf[...] = wd_ref[...].astype(BF16)

    y_sh = _swiglu(_unpack_pairs_bf16(hm_ref[...]), wgu_bf[...], wd_bf[...], ff)
    w8 = w8_ref[...]
    half = yk_refs[0].shape[-1]
    acc_lo = y_sh[:, :half]
    acc_hi = y_sh[:, half:]
    for kk in range(TOP_K):
        rows = yk_refs[kk][...]
        wk = w8[:, kk:kk + 1]
        acc_lo = acc_lo + wk * _unpack_lo(rows)
        acc_hi = acc_hi + wk * _unpack_hi(rows)
    acc = jnp.concatenate([acc_lo, acc_hi], axis=-1)
    y = alpha * x_ref[...] + gate_ref[...] * acc
    xn = _layer_norm(y, g_ref[...], b_ref[...])
    if with_h:
        sh_ref, sc_ref, xo_ref, ho_ref = rest
        xo_ref[...] = xn
        ho_ref[...] = (xn * (1.0 + sc_ref[...]) + sh_ref[...]).astype(ho_ref.dtype)
    else:
        xp_ref, xs_ref = rest
        i = pl.program_id(0)

        @pl.when(i < n_p)
        def _():
            xp_ref[...] = xn

        @pl.when(i >= n_p)
        def _():
            xs_ref[...] = xn


def _combine(yk, w8, hm, ws_gu, ws_down, layer, x, gate, ln_g, ln_b, shift, scale, alpha, tp,
             seq_s):
    t, d = x.shape
    ff = ws_down.shape[1]
    tm = _row_tile(256, tp, seq_s)
    n_steps = t // tm
    n_p = tp // tm
    with_h = shift is not None
    cid = _cond_index(tm, tp, seq_s)
    row = pl.BlockSpec((tm, d), lambda i: (i, 0))
    packed_row = pl.BlockSpec((tm, d // 2), lambda i: (i, 0))
    mod_spec = pl.BlockSpec((None, 1, d), lambda i: (cid(i), 0, 0))
    vec = pl.BlockSpec((1, d), lambda i: (0, 0))
    resident = pl.Buffered(1)
    in_specs = [pl.BlockSpec((tm, d // 2), lambda i, kk=kk: (kk * n_steps + i, 0))
                for kk in range(TOP_K)]
    in_specs += [pl.BlockSpec((tm, TOP_K), lambda i: (i, 0)), packed_row,
                 pl.BlockSpec((None, d, 2 * ff), lambda i: (layer, 0, 0), pipeline_mode=resident),
                 pl.BlockSpec((None, ff, d), lambda i: (layer, 0, 0), pipeline_mode=resident),
                 row, mod_spec, vec, vec]
    args = [yk] * TOP_K + [w8, hm, ws_gu, ws_down, x, gate, ln_g.reshape(1, d),
                           ln_b.reshape(1, d)]
    if with_h:
        in_specs += [mod_spec, mod_spec]
        args += [shift, scale]
        out_specs = [row, row]
        out_shape = [jax.ShapeDtypeStruct((t, d), F32), jax.ShapeDtypeStruct((t, d), BF16)]
    else:
        out_specs = [pl.BlockSpec((tm, d), lambda i: (jnp.minimum(i, n_p - 1), 0)),
                     pl.BlockSpec((tm, d), lambda i: (jnp.maximum(i - n_p, 0), 0))]
        out_shape = [jax.ShapeDtypeStruct((tp, d), F32), jax.ShapeDtypeStruct((t - tp, d), F32)]
    return pl.pallas_call(
        functools.partial(_combine_kernel, alpha=alpha, n_p=n_p, with_h=with_h, ff=ff),
        grid=(n_steps,),
        in_specs=in_specs,
        out_specs=out_specs,
        out_shape=out_shape,
        scratch_shapes=[pltpu.VMEM((d, 2 * ff), BF16), pltpu.VMEM((ff, d), BF16)],
        compiler_params=_cparams(1),
        name="moe_combine",
    )(*args)


def _expert_row_order(n_exp):
    per = n_exp // N_GROUPS
    r = jnp.arange(n_exp)
    return (r % N_GROUPS) * per + r // N_GROUPS


def _count_le(edges, v):
    return jnp.sum((edges[None, :] <= v[:, None]).astype(jnp.int32), axis=1)


def _moe_layer(x, mixer_out, layer, pre, w_r, b_r, w_gu, w_down, ws_gu, ws_down, gate, ln_g, ln_b,
               shift, scale, alpha, tp, seq_s):
    t = x.shape[0]
    n_exp = w_r.shape[1]
    order = _expert_row_order(n_exp)
    x, hm, sel, wd, pos, cnt = _norm_route(x, mixer_out, *pre, w_r.T[order],
                                           b_r[order].reshape(n_exp, 1), alpha, tp, seq_s)

    assert (t * TOP_K) % MOE_BLOCK == 0
    n_blocks = t * TOP_K // MOE_BLOCK + n_exp
    cap = n_blocks * MOE_BLOCK
    counts = cnt[:, 0].astype(jnp.int32)
    padded = (counts + MOE_BLOCK - 1) // MOE_BLOCK * MOE_BLOCK
    pend = jnp.cumsum(padded)
    pstart = pend - padded
    n_used = (pend[-1:] // MOE_BLOCK).astype(jnp.int32)
    block_row = jnp.minimum(_count_le(pend, jnp.arange(n_blocks) * MOE_BLOCK), n_exp - 1)
    per = n_exp // N_GROUPS
    block_e = ((block_row % N_GROUPS) * per + block_row // N_GROUPS).astype(jnp.int32)
    dest8, w8 = _slots(sel, wd, pos, pstart.astype(F32).reshape(n_exp, 1))

    padlen = padded - counts
    cpad_start = jnp.cumsum(padlen) - padlen
    f = jnp.arange(cap - t * TOP_K)
    fill = f + jnp.sum(jnp.where(cpad_start[None, :] <= f[:, None], counts[None, :], 0), axis=1)
    position = jnp.concatenate([dest8.reshape(-1), fill.astype(jnp.int32)])
    assign = jnp.argsort(position).astype(jnp.int32)
    src_ids = (assign % t).reshape(n_blocks, 1, MOE_BLOCK)
    n_real = t * TOP_K
    spare = n_real + jnp.arange(cap, dtype=jnp.int32) % (2 * MOE_BLOCK)
    dst_ids = jnp.where(assign >= n_real, spare, assign).reshape(n_blocks, 1, MOE_BLOCK)

    yk = _experts(hm, src_ids, dst_ids, block_e, n_used, w_gu, w_down, layer, n_real)
    return _combine(yk, w8.T, hm, ws_gu, ws_down, layer, x, gate, ln_g, ln_b, shift, scale, alpha,
                    tp, seq_s)


def _gla_layer(h, state_gla, new_state, layer_j, w_in, w_a1, w_a2, b_a, norm_g, w_out, dims):
    bp, sp, bs, ss = dims
    tp = bp * sp
    d = h.shape[1]
    rank = w_a1.shape[3]
    proj = _matmul(h, w_in, layer_j, BF16)
    w_low = jnp.transpose(w_a1[layer_j], (1, 0, 2)).reshape(1, d, 2 * rank)
    low = _matmul(h, w_low, 0, F32)
    low = jnp.transpose(low.reshape(-1, 2, rank), (1, 0, 2))
    og_p, st = _gla_scan(proj, low, w_a2[layer_j], b_a[layer_j], norm_g[layer_j], None, layer_j,
                         batch=bp, seq_len=sp, row0=0, hp=GLA_HEADS,
                         new_state="first" if new_state is None else new_state,
                         n_state_layers=w_in.shape[0])
    (og_s,) = _gla_scan(proj, low, w_a2[layer_j], b_a[layer_j], norm_g[layer_j], state_gla,
                        layer_j, batch=bs, seq_len=ss, row0=tp, hp=GLA_HEADS // 2)
    out = _matmul((og_p, og_s), w_out, layer_j, BF16)
    return out, st


def _sgu_layer(h, layer_j, w_in, ln_g, ln_b, w_s, b_s, w_out):
    z = _matmul(h, w_in, layer_j, BF16, act="gelu")
    gated = _sgu_gate(z, ln_g[layer_j], ln_b[layer_j], w_s[layer_j], b_s[layer_j])
    return _matmul(gated, w_out, layer_j, BF16)


def _grid_pos_embed(rows, d):
    t = jnp.arange(rows * GRID_W)
    r = (t // GRID_W).astype(F32)
    col = (t % GRID_W).astype(F32)
    quarter = d // 4
    omega = 1.0 / (10000.0 ** (jnp.arange(quarter, dtype=F32) / quarter))

    def emb(p):
        a = p[:, None] * omega[None, :]
        return jnp.concatenate([jnp.sin(a), jnp.cos(a)], axis=-1)

    return jnp.concatenate([emb(r), emb(col)], axis=-1)


def kernel(x_prompt, x_sample, state_gla, c, c_ctx, w_ada, b_ada, ln_g, ln_b, gla_w_in, gla_w_a1, gla_w_a2, gla_b_a, gla_norm_g, gla_w_out, sgu_w_in, sgu_ln_g, sgu_ln_b, sgu_w_s, sgu_b_s, sgu_w_out, moe_w_router, moe_b_router, moe_w_gu, moe_w_down, moe_ws_gu, moe_ws_down):
    bp, sp, d = x_prompt.shape
    bs, ss, _ = x_sample.shape
    assert 1 + bs <= COND_ROWS
    tp, ts = bp * sp, bs * ss
    depth = w_ada.shape[0]
    alpha = (2 * depth) ** 0.25
    dims = (bp, sp, bs, ss)

    cond = jnp.zeros((COND_ROWS, d), F32).at[0].set(c_ctx).at[1:1 + bs].set(c)
    mods = _ada_mods(cond, w_ada, b_ada).reshape(depth, COND_ROWS, 6, 1, d)

    def mod(l, i):
        return mods[l, :, i]

    pos = _grid_pos_embed(ss // GRID_W, d).astype(x_sample.dtype)
    x, h = _prep(x_prompt.reshape(tp, d), x_sample.reshape(ts, d), pos, mod(0, 0), mod(0, 1), ss)

    new_state = None
    for l in range(depth):
        j = l // 2
        if l % 2 == 0:
            out, new_state = _gla_layer(h, state_gla, new_state, j, gla_w_in, gla_w_a1, gla_w_a2,
                                        gla_b_a, gla_norm_g, gla_w_out, dims)
        else:
            out = _sgu_layer(h, j, sgu_w_in, sgu_ln_g, sgu_ln_b, sgu_w_s, sgu_b_s, sgu_w_out)
        pre = (mod(l, 2), ln_g[l, 0], ln_b[l, 0], mod(l, 3), mod(l, 4))
        last = l == depth - 1
        res = _moe_layer(x, out, l, pre, moe_w_router[l], moe_b_router[l], moe_w_gu, moe_w_down,
                         moe_ws_gu, moe_ws_down, mod(l, 5), ln_g[l, 1], ln_b[l, 1],
                         None if last else mod(l + 1, 0), None if last else mod(l + 1, 1),
                         alpha, tp, ss)
        if last:
            y_prompt, y_sample = res
        else:
            x, h = res

    return y_prompt.reshape(bp, sp, d), y_sample.reshape(bs, ss, d), new_state
```

```python
import functools

import jax
import jax.numpy as jnp
from jax import lax
from jax.experimental import pallas as pl
from jax.experimental.pallas import tpu as pltpu

F32 = jnp.float32
BF16 = jnp.bfloat16

V7X_VMEM_LIMIT_BYTES = 56 * 1024 * 1024
LANES = 128
SUBLANES = 8

GRID_W = 64
GLA_HEADS = 4
GLA_CHUNK = 64
GLA_NORMALIZER = 16.0
SGU_GROUPS = 4
SGU_CHUNK = 128
TOP_K = 8
N_GROUPS = 8
TOPK_GROUPS = 4
ROUTE_SCALE = 2.5
LN_EPS = 1e-5
RMS_EPS = 1e-6
MOE_BLOCK = 512
COND_ROWS = 8


def _cparams(n_axes):
    return pltpu.CompilerParams(
        dimension_semantics=("arbitrary",) * n_axes,
        vmem_limit_bytes=V7X_VMEM_LIMIT_BYTES,
    )


def _silu(x):
    return x * jax.nn.sigmoid(x)


def _layer_norm(y, g, b):
    mu = jnp.mean(y, axis=-1, keepdims=True)
    d = y - mu
    var = jnp.mean(d * d, axis=-1, keepdims=True)
    return d * lax.rsqrt(var + LN_EPS) * g + b


_HI16 = 0xFFFF0000


def _pack_pairs(x):
    half = x.shape[-1] // 2
    lo = lax.bitcast_convert_type(x[:, :half].astype(BF16).astype(F32), jnp.uint32)
    hi = lax.bitcast_convert_type(x[:, half:].astype(BF16).astype(F32), jnp.uint32)
    return lax.shift_right_logical(lo, jnp.uint32(16)) | (hi & jnp.uint32(_HI16))


def _unpack_lo(w):
    return lax.bitcast_convert_type(lax.shift_left(w, jnp.uint32(16)), F32)


def _unpack_hi(w):
    return lax.bitcast_convert_type(w & jnp.uint32(_HI16), F32)


def _unpack_pairs_bf16(w):
    return jnp.concatenate([_unpack_lo(w).astype(BF16), _unpack_hi(w).astype(BF16)], axis=-1)


def _row_tile(limit, *sizes):
    tm = limit
    while any(s % tm for s in sizes):
        tm //= 2
    return tm


def _cond_index(tm, tp, seq_s):
    def index(i):
        return jnp.where(i * tm < tp, 0, 1 + (i * tm - tp) // seq_s)

    return index


def _ada_kernel(c_ref, w_ref, b_ref, o_ref):
    s = _silu(c_ref[...]).astype(BF16)
    o_ref[...] = jnp.dot(s, w_ref[...].astype(BF16), preferred_element_type=F32) + b_ref[...]


def _ada_mods(cond, w_ada, b_ada):
    depth, d, n = w_ada.shape
    tn = _row_tile(1024, n)
    return pl.pallas_call(
        _ada_kernel,
        grid=(depth, n // tn),
        in_specs=[
            pl.BlockSpec((COND_ROWS, d), lambda l, j: (0, 0)),
            pl.BlockSpec((None, d, tn), lambda l, j: (l, 0, j)),
            pl.BlockSpec((None, 1, tn), lambda l, j: (l, 0, j)),
        ],
        out_specs=pl.BlockSpec((None, COND_ROWS, tn), lambda l, j: (l, 0, j)),
        out_shape=jax.ShapeDtypeStruct((depth, COND_ROWS, n), F32),
        compiler_params=_cparams(2),
        name="ada_mods",
    )(cond, w_ada, b_ada.reshape(depth, 1, n))


def _prep_kernel(xp_ref, xs_ref, pos_ref, sh_ref, sc_ref, x_ref, h_ref, *, n_p):
    x = jnp.where(pl.program_id(0) < n_p, xp_ref[...], xs_ref[...] + pos_ref[...])
    x_ref[...] = x
    h_ref[...] = (x * (1.0 + sc_ref[...]) + sh_ref[...]).astype(h_ref.dtype)


def _prep(xp, xs, pos, shift, scale, seq_s):
    tp, d = xp.shape
    ts = xs.shape[0]
    tm = _row_tile(512, tp, seq_s)
    n_p = tp // tm
    n_pos = seq_s // tm
    cid = _cond_index(tm, tp, seq_s)
    mod_spec = pl.BlockSpec((None, 1, d), lambda i: (cid(i), 0, 0))
    return pl.pallas_call(
        functools.partial(_prep_kernel, n_p=n_p),
        grid=((tp + ts) // tm,),
        in_specs=[
            pl.BlockSpec((tm, d), lambda i: (jnp.minimum(i, n_p - 1), 0)),
            pl.BlockSpec((tm, d), lambda i: (jnp.maximum(i - n_p, 0), 0)),
            pl.BlockSpec((tm, d), lambda i: (jnp.maximum(i - n_p, 0) % n_pos, 0)),
            mod_spec,
            mod_spec,
        ],
        out_specs=[pl.BlockSpec((tm, d), lambda i: (i, 0))] * 2,
        out_shape=[
            jax.ShapeDtypeStruct((tp + ts, d), F32),
            jax.ShapeDtypeStruct((tp + ts, d), BF16),
        ],
        compiler_params=_cparams(1),
        name="prep",
    )(xp, xs, pos, shift, scale)


def _mm_kernel(*refs, act, n_first):
    *x_refs, w_ref, o_ref, wbf_ref = refs

    @pl.when(pl.program_id(1) == 0)
    def _():
        wbf_ref[...] = w_ref[...].astype(BF16)

    x = x_refs[0][...]
    if len(x_refs) == 2:
        x = jnp.where(pl.program_id(1) < n_first, x, x_refs[1][...])
    acc = jnp.dot(x.astype(BF16), wbf_ref[...], preferred_element_type=F32)
    if act == "gelu":
        acc = 0.5 * acc * (1.0 + lax.erf(acc * 0.7071067811865476))
    o_ref[...] = acc.astype(o_ref.dtype)


def _matmul(x, w, layer, out_dtype, act=None):
    xs = x if isinstance(x, tuple) else (x,)
    x = xs[0]
    m = sum(a.shape[0] for a in xs)
    k = x.shape[1]
    n = w.shape[2]
    tn_limit = 1024
    while k * tn_limit * 10 > V7X_VMEM_LIMIT_BYTES * 4 // 7 and tn_limit > LANES:
        tn_limit //= 2
    tn = _row_tile(tn_limit, n)
    out_bytes = jnp.dtype(out_dtype).itemsize
    tm_limit = 1024
    while (k * tn * 10
           + tm_limit * (2 * len(xs) * k * x.dtype.itemsize + tn * (2 * out_bytes + 4))
           > V7X_VMEM_LIMIT_BYTES * 7 // 8 and tm_limit > SUBLANES):
        tm_limit //= 2
    tm = _row_tile(tm_limit, *(a.shape[0] for a in xs))
    n_first = xs[0].shape[0] // tm
    x_specs = [pl.BlockSpec((tm, k), lambda j, i: (jnp.minimum(i, n_first - 1), 0))]
    if len(xs) == 2:
        x_specs.append(pl.BlockSpec((tm, k), lambda j, i: (jnp.maximum(i - n_first, 0), 0)))
    return pl.pallas_call(
        functools.partial(_mm_kernel, act=act, n_first=n_first),
        grid=(n // tn, m // tm),
        in_specs=x_specs + [pl.BlockSpec((None, k, tn), lambda j, i: (layer, 0, j))],
        out_specs=pl.BlockSpec((tm, tn), lambda j, i: (i, j)),
        out_shape=jax.ShapeDtypeStruct((m, n), out_dtype),
        scratch_shapes=[pltpu.VMEM((k, tn), BF16)],
        compiler_params=_cparams(2),
        name="matmul_" + (act or "lin"),
    )(*xs, w)


def _prefix_sum_mm(tri, x):
    hi = x.astype(BF16)
    r1 = x - hi.astype(F32)
    mid = r1.astype(BF16)
    lo = (r1 - mid.astype(F32)).astype(BF16)
    out = jnp.dot(tri, hi, preferred_element_type=F32)
    out = out + jnp.dot(tri, mid, preferred_element_type=F32)
    return out + jnp.dot(tri, lo, preferred_element_type=F32)


def _gla_kernel(*refs, seq_len, hp, has_s0, out_state, layer_j):
    q_ref, k_ref, v_ref, g_ref, low_ref, wa2_ref, ba_ref, ng_ref = refs[:8]
    rest = list(refs[8:])
    s0_ref = rest.pop(0) if has_s0 else None
    if out_state == "next":
        rest.pop(0)
    o_ref = rest.pop(0)
    st_ref = rest.pop(0) if out_state else None
    gk_scr, oacc, s_t = rest

    c = GLA_CHUNK
    n = seq_len // c
    dk = q_ref.shape[1] // hp
    dv = v_ref.shape[1] // hp
    scale = dk ** -0.5

    for z in range(2):
        pre = jnp.dot(low_ref[z].astype(BF16), wa2_ref[z].astype(BF16),
                      preferred_element_type=F32) + ba_ref[z]
        log_sig = jnp.minimum(pre, 0.0) - jnp.log(1.0 + jnp.exp(-jnp.abs(pre)))
        gk_scr[z] = log_sig * (1.0 / GLA_NORMALIZER)
        for hh in range(hp):
            s_t[z, hh] = s0_ref[z, hh].T if has_s0 else jnp.zeros((dv, dk), F32)

    row = lax.broadcasted_iota(jnp.int32, (c, c), 0)
    col = lax.broadcasted_iota(jnp.int32, (c, c), 1)
    masks = (col <= row, col >= row)
    tris = tuple(jnp.where(m, 1.0, 0.0).astype(BF16) for m in masks)
    nt_dims = (((1,), (1,)), ((), ()))
    tn_dims = (((0,), (0,)), ((), ()))

    n_stages = 5

    def chunk_stages(ci, z, hh):
        rows = pl.ds(pl.multiple_of(ci * c, c), c)
        ck = slice(hh * dk, (hh + 1) * dk)
        cv = slice(hh * dv, (hh + 1) * dv)
        gcum = _prefix_sum_mm(tris[z], gk_scr[z, rows, ck])
        yield
        qc = q_ref[rows, ck].astype(F32) * scale
        kc = k_ref[rows, ck].astype(F32)
        if z == 0:
            g_mid = gcum[c // 2 - 1:c // 2]
            g_last = gcum[c - 1:c]
        else:
            g_mid = gcum[c // 2:c // 2 + 1]
            g_last = gcum[0:1]
        q_in = (qc * jnp.exp(gcum - g_mid)).astype(BF16)
        k_in = (kc * jnp.exp(g_mid - gcum)).astype(BF16)
        a = lax.dot_general(q_in, k_in, nt_dims, preferred_element_type=F32)
        yield
        vc = v_ref[rows, cv].astype(BF16)
        a = jnp.where(masks[z], a, 0.0).astype(BF16)
        o = jnp.dot(a, vc, preferred_element_type=F32)
        yield
        s_prev = s_t[z, hh]
        q_dec = (qc * jnp.exp(gcum)).astype(BF16)
        o = o + lax.dot_general(q_dec, s_prev.astype(BF16), nt_dims, preferred_element_type=F32)
        oacc[z, rows, cv] = o
        yield
        k_dec = (kc * jnp.exp(g_last - gcum)).astype(BF16)
        u_t = lax.dot_general(vc, k_dec, tn_dims, preferred_element_type=F32)
        s_t[z, hh] = s_prev * jnp.exp(g_last) + u_t
        yield

    def body(i, carry):
        scans = [chunk_stages(i if z == 0 else n - 1 - i, z, hh)
                 for z in range(2) for hh in range(hp)]
        for _ in range(n_stages):
            for scan in scans:
                next(scan)
        return carry

    lax.fori_loop(0, n, body, 0)

    for hh in range(hp):
        cv = slice(hh * dv, (hh + 1) * dv)
        for z in range(2):
            if out_state == "next":
                st_ref[z, hh] = s_t[z, hh].T
            elif out_state == "first":
                for j in range(st_ref.shape[0]):
                    st_ref[j, z, hh] = s_t[z, hh].T if j == layer_j else jnp.zeros((dk, dv), F32)
        o = oacc[0, :, cv] + oacc[1, :, cv]
        on = o * lax.rsqrt(jnp.mean(o * o, axis=-1, keepdims=True) + RMS_EPS) * ng_ref[...]
        o_ref[:, cv] = (on * _silu(g_ref[:, cv].astype(F32))).astype(o_ref.dtype)


def _gla_scan(proj, low, w_a2, b_a, norm_g, state, layer_j, *, batch, seq_len, row0, hp,
              new_state=None, n_state_layers=1):
    h = GLA_HEADS
    hk = w_a2.shape[2]
    dk = hk // h
    hv = (proj.shape[1] - 2 * hk) // 2
    dv = hv // h
    rank = w_a2.shape[1]
    off = row0 // seq_len
    has_s0 = state is not None
    wk, wv = hp * dk, hp * dv
    in_specs = [
        pl.BlockSpec((seq_len, wk), lambda b, i: (b + off, i)),
        pl.BlockSpec((seq_len, wk), lambda b, i: (b + off, hk // wk + i)),
        pl.BlockSpec((seq_len, wv), lambda b, i: (b + off, 2 * hk // wv + i)),
        pl.BlockSpec((seq_len, wv), lambda b, i: (b + off, (2 * hk + hv) // wv + i)),
        pl.BlockSpec((2, seq_len, rank), lambda b, i: (0, b + off, 0)),
        pl.BlockSpec((2, rank, wk), lambda b, i: (0, 0, i)),
        pl.BlockSpec((2, 1, wk), lambda b, i: (0, 0, i)),
        pl.BlockSpec((1, dv), lambda b, i: (0, 0)),
    ]
    args = [proj, proj, proj, proj, low, w_a2, b_a.reshape(2, 1, hk), norm_g.reshape(1, dv)]
    if has_s0:
        in_specs.append(pl.BlockSpec((None, None, 2, hp, dk, dv),
                                     lambda b, i: (b, layer_j, 0, i, 0, 0)))
        args.append(state)
    out_specs = [pl.BlockSpec((seq_len, wv), lambda b, i: (b, i))]
    out_shape = [jax.ShapeDtypeStruct((batch * seq_len, hv), BF16)]
    aliases = {}
    out_state = None
    if isinstance(new_state, str):
        out_state = "first"
        out_specs.append(pl.BlockSpec((None, n_state_layers, 2, hp, dk, dv),
                                      lambda b, i: (b, 0, 0, i, 0, 0)))
        out_shape.append(jax.ShapeDtypeStruct((batch, n_state_layers, 2, h, dk, dv), F32))
    elif new_state is not None:
        out_state = "next"
        aliases = {len(args): 1}
        in_specs.append(pl.BlockSpec(memory_space=pl.ANY))
        args.append(new_state)
        out_specs.append(pl.BlockSpec((None, None, 2, hp, dk, dv),
                                      lambda b, i: (b, layer_j, 0, i, 0, 0)))
        out_shape.append(jax.ShapeDtypeStruct(new_state.shape, F32))
    return pl.pallas_call(
        functools.partial(_gla_kernel, seq_len=seq_len, hp=hp, has_s0=has_s0,
                          out_state=out_state, layer_j=layer_j),
        grid=(batch, h // hp),
        in_specs=in_specs,
        out_specs=out_specs,
        out_shape=out_shape,
        input_output_aliases=aliases,
        scratch_shapes=[
            pltpu.VMEM((2, seq_len, wk), F32),
            pltpu.VMEM((2, seq_len, wv), F32),
            pltpu.VMEM((2, hp, dv, dk), F32),
        ],
        compiler_params=_cparams(2),
        name="gla_scan",
    )(*args)


def _sgu_kernel(u_ref, v_ref, lg_ref, lb_ref, ws_ref, bs_ref, o_ref):
    v = v_ref[...].astype(F32)
    vn = _layer_norm(v, lg_ref[...], lb_ref[...]).astype(BF16)
    rows, half = vn.shape
    cg = half // SGU_GROUPS
    for r in range(rows // SGU_CHUNK):
        rs = slice(r * SGU_CHUNK, (r + 1) * SGU_CHUNK)
        for g in range(SGU_GROUPS):
            cs = slice(g * cg, (g + 1) * cg)
            s = jnp.dot(ws_ref[g].astype(BF16), vn[rs, cs], preferred_element_type=F32) + bs_ref[g]
            o_ref[rs, cs] = (u_ref[rs, cs].astype(F32) * s).astype(o_ref.dtype)


def _sgu_gate(z, ln_g, ln_b, w_s, b_s):
    t, ffn = z.shape
    half = ffn // 2
    rows = SGU_CHUNK
    return pl.pallas_call(
        _sgu_kernel,
        grid=(t // rows,),
        in_specs=[
            pl.BlockSpec((rows, half), lambda i: (i, 0)),
            pl.BlockSpec((rows, half), lambda i: (i, 1)),
            pl.BlockSpec((1, half), lambda i: (0, 0)),
            pl.BlockSpec((1, half), lambda i: (0, 0)),
            pl.BlockSpec((SGU_GROUPS, SGU_CHUNK, SGU_CHUNK), lambda i: (0, 0, 0)),
            pl.BlockSpec((SGU_GROUPS, SGU_CHUNK, 1), lambda i: (0, 0, 0)),
        ],
        out_specs=pl.BlockSpec((rows, half), lambda i: (i, 0)),
        out_shape=jax.ShapeDtypeStruct((t, half), BF16),
        compiler_params=_cparams(1),
        name="sgu_gate",
    )(z, z, ln_g.reshape(1, half), ln_b.reshape(1, half), w_s, b_s.reshape(SGU_GROUPS, SGU_CHUNK, 1))


def _norm_route_kernel(x_ref, o_ref, gate_ref, g_ref, b_ref, sh_ref, sc_ref, wrt_ref, br_ref,
                       xo_ref, ho_ref, sel_ref, wd_ref, pos_ref, cnt_ref, carry, *, alpha):
    i = pl.program_id(0)

    @pl.when(i == 0)
    def _():
        carry[...] = jnp.zeros_like(carry)

    y = alpha * x_ref[...] + gate_ref[...] * o_ref[...].astype(F32)
    xn = _layer_norm(y, g_ref[...], b_ref[...])
    xo_ref[...] = xn
    h = xn * (1.0 + sc_ref[...]) + sh_ref[...]
    ho_ref[...] = _pack_pairs(h)

    x = h.astype(BF16)
    logits = lax.dot_general(wrt_ref[...].astype(BF16), x, (((1,), (1,)), ((), ())),
                             preferred_element_type=F32)
    n_exp, tm = logits.shape
    ng = N_GROUPS
    per = n_exp // ng
    scores = jax.nn.sigmoid(logits)
    selv = scores + br_ref[...]
    a = [selv[j * ng:(j + 1) * ng] for j in range(per)]
    sc = [scores[j * ng:(j + 1) * ng] for j in range(per)]

    t1 = a[0]
    t2 = jnp.full_like(t1, -jnp.inf)
    for j in range(1, per):
        t2 = jnp.maximum(t2, jnp.minimum(t1, a[j]))
        t1 = jnp.maximum(t1, a[j])
    gscore = t1 + t2

    gidx = lax.broadcasted_iota(jnp.int32, (ng, tm), 0)
    grank = jnp.zeros((ng, tm), jnp.int32)
    for gp in range(ng):
        other = gscore[gp:gp + 1, :]
        beats = (other > gscore) | ((other == gscore) & (gp < gidx))
        grank = grank + beats.astype(jnp.int32)
    gmask = grank < TOPK_GROUPS

    val = [jnp.where(gmask, a[j], -jnp.inf) for j in range(per)]
    ranks = [jnp.zeros((ng, tm), jnp.int32) for _ in range(per)]
    for gp in range(ng):
        lower_group = gp < gidx
        lower_or_same_group = gp <= gidx
        for jp in range(per):
            other = val[jp][gp:gp + 1, :]
            for j in range(per):
                tie = lower_or_same_group if jp < j else lower_group
                beats = (other > val[j]) | ((other == val[j]) & tie)
                ranks[j] = ranks[j] + beats.astype(jnp.int32)
    selm = [(ranks[j] < TOP_K) & gmask for j in range(per)]

    ssum = jnp.zeros((ng, tm), F32)
    for j in range(per):
        ssum = ssum + jnp.where(selm[j], sc[j], 0.0)
    ssum = jnp.sum(ssum, axis=0, keepdims=True)
    wd = jnp.concatenate(
        [jnp.where(selm[j], sc[j] / ssum * ROUTE_SCALE, 0.0) for j in range(per)], axis=0)
    selmat = jnp.concatenate([jnp.where(selm[j], 1.0, 0.0) for j in range(per)], axis=0)

    s_i = lax.broadcasted_iota(jnp.int32, (tm, tm), 0)
    t_i = lax.broadcasted_iota(jnp.int32, (tm, tm), 1)
    upper = jnp.where(s_i < t_i, 1.0, 0.0).astype(BF16)
    base = carry[...][:, 0:1]
    pos = jnp.dot(selmat.astype(BF16), upper, preferred_element_type=F32) + base
    total = base + jnp.sum(selmat, axis=1, keepdims=True)
    carry[...] = jnp.broadcast_to(total, carry.shape)

    sel_ref[...] = selmat
    wd_ref[...] = wd
    pos_ref[...] = pos
    cnt_ref[...] = jnp.broadcast_to(total, cnt_ref.shape)


def _norm_route(x, out, gate, ln_g, ln_b, shift, scale, w_rt, b_r_col, alpha, tp, seq_s):
    t, d = x.shape
    n_exp = w_rt.shape[0]
    tm = _row_tile(512, tp, seq_s)
    cid = _cond_index(tm, tp, seq_s)
    row = pl.BlockSpec((tm, d), lambda i: (i, 0))
    mod_spec = pl.BlockSpec((None, 1, d), lambda i: (cid(i), 0, 0))
    vec = pl.BlockSpec((1, d), lambda i: (0, 0))
    col = pl.BlockSpec((n_exp, tm), lambda i: (0, i))
    return pl.pallas_call(
        functools.partial(_norm_route_kernel, alpha=alpha),
        grid=(t // tm,),
        in_specs=[row, row, mod_spec, vec, vec, mod_spec, mod_spec,
                  pl.BlockSpec((n_exp, d), lambda i: (0, 0)),
                  pl.BlockSpec((n_exp, 1), lambda i: (0, 0))],
        out_specs=[row, pl.BlockSpec((tm, d // 2), lambda i: (i, 0)), col, col, col,
                   pl.BlockSpec((n_exp, LANES), lambda i: (0, 0))],
        out_shape=[
            jax.ShapeDtypeStruct((t, d), F32),
            jax.ShapeDtypeStruct((t, d // 2), jnp.uint32),
            jax.ShapeDtypeStruct((n_exp, t), F32),
            jax.ShapeDtypeStruct((n_exp, t), F32),
            jax.ShapeDtypeStruct((n_exp, t), F32),
            jax.ShapeDtypeStruct((n_exp, LANES), F32),
        ],
        scratch_shapes=[pltpu.VMEM((n_exp, LANES), F32)],
        compiler_params=_cparams(1),
        name="norm_route",
    )(x, out, gate, ln_g.reshape(1, d), ln_b.reshape(1, d), shift, scale, w_rt, b_r_col)


def _slots_kernel(sel_ref, wd_ref, pos_ref, ps_ref, d8_ref, w8_ref):
    sel = sel_ref[...]
    n_exp = sel.shape[0]
    r_i = lax.broadcasted_iota(jnp.int32, (n_exp, n_exp), 0)
    c_i = lax.broadcasted_iota(jnp.int32, (n_exp, n_exp), 1)
    lower = jnp.where(c_i < r_i, 1.0, 0.0).astype(BF16)
    slot = jnp.dot(lower, sel.astype(BF16), preferred_element_type=F32)
    chosen = sel > 0.5
    dest = pos_ref[...] + ps_ref[...]
    wd = wd_ref[...]
    d_rows, w_rows = [], []
    for k in range(TOP_K):
        pick = chosen & (slot == k)
        d_rows.append(jnp.sum(jnp.where(pick, dest, 0.0), axis=0, keepdims=True))
        w_rows.append(jnp.sum(jnp.where(pick, wd, 0.0), axis=0, keepdims=True))
    d8_ref[...] = jnp.concatenate(d_rows, axis=0).astype(jnp.int32)
    w8_ref[...] = jnp.concatenate(w_rows, axis=0)


def _slots(sel, wd, pos, pstart_col):
    n_exp, t = sel.shape
    tm = _row_tile(512, t)
    col = pl.BlockSpec((n_exp, tm), lambda i: (0, i))
    out = pl.BlockSpec((TOP_K, tm), lambda i: (0, i))
    return pl.pallas_call(
        _slots_kernel,
        grid=(t // tm,),
        in_specs=[col, col, col, pl.BlockSpec((n_exp, 1), lambda i: (0, 0))],
        out_specs=[out, out],
        out_shape=[jax.ShapeDtypeStruct((TOP_K, t), jnp.int32),
                   jax.ShapeDtypeStruct((TOP_K, t), F32)],
        compiler_params=_cparams(1),
        name="moe_slots",
    )(sel, wd, pos, pstart_col)


def _swiglu(x, wgu, wd, ff):
    h = jnp.dot(x, wgu, preferred_element_type=F32)
    act = (_silu(h[:, :ff]) * h[:, ff:]).astype(BF16)
    return jnp.dot(act, wd, preferred_element_type=F32)


def _expert_kernel(be_ref, nu_ref, src_cur, src_next, dst_prev, hm_hbm, wgu_ref, wd_ref,
                   y_hbm, xbuf0, xbuf1, ybuf0, ybuf1, wgu_bf, wd_bf, gsem, ssem, *, ff, n_real):
    b = pl.program_id(0)
    nu = nu_ref[0]
    xbufs, ybufs = (xbuf0, xbuf1), (ybuf0, ybuf1)
    rows = MOE_BLOCK

    def gather(ids, par):
        for i in range(rows):
            pltpu.make_async_copy(hm_hbm.at[pl.ds(ids[0, i], 1), :],
                                  xbufs[par].at[pl.ds(i, 1), :], gsem.at[par]).start()

    def scatter(ids, par):
        for i in range(rows):
            pltpu.make_async_copy(ybufs[par].at[pl.ds(i, 1), :],
                                  y_hbm.at[pl.ds(ids[0, i], 1), :], ssem.at[par]).start()

    def wait_gather(par):
        pltpu.make_async_copy(hm_hbm.at[pl.ds(0, rows), :], xbufs[par], gsem.at[par]).wait()

    def wait_scatter(par):
        pltpu.make_async_copy(ybufs[par], y_hbm.at[pl.ds(0, rows), :], ssem.at[par]).wait()

    def block_step(par, first):
        if first:
            for k in range(2):
                ybufs[k][...] = jnp.zeros_like(ybufs[k])
                spare = y_hbm.at[pl.ds(n_real + k * rows, rows), :]
                pltpu.make_async_copy(ybufs[k], spare, ssem.at[k]).start()
            for k in range(2):
                wait_scatter(k)
            gather(src_cur, par)
        wait_gather(par)

        @pl.when((b == 0) | (be_ref[b] != be_ref[jnp.maximum(b - 1, 0)]))
        def _():
            wgu_bf[...] = wgu_ref[...].astype(BF16)
            wd_bf[...] = wd_ref[...].astype(BF16)

        gather(src_next, 1 - par)
        if not first:
            scatter(dst_prev, 1 - par)
        y = _swiglu(_unpack_pairs_bf16(xbufs[par][...]), wgu_bf[...], wd_bf[...], ff)
        if not first:
            @pl.when(b >= 2)
            def _():
                wait_scatter(par)
        ybufs[par][...] = _pack_pairs(y)

    @pl.when(b == 0)
    def _():
        block_step(0, True)

    for par in range(2):
        @pl.when((b > 0) & (b < nu) & (b % 2 == par))
        def _(par=par):
            block_step(par, False)

        @pl.when((b == nu) & (b % 2 == par))
        def _(par=par):
            wait_gather(par)
            scatter(dst_prev, 1 - par)

            @pl.when(b >= 2)
            def _():
                wait_scatter(par)

            wait_scatter(1 - par)


def _experts(hm, src_ids, dst_ids, block_e, n_used, w_gu, w_down, layer, n_real):
    dh = hm.shape[1]
    d = 2 * dh
    ff = w_down.shape[2]
    n_blocks = src_ids.shape[0]

    def blk(b, be, nu):
        return jnp.minimum(b, nu[0] - 1)

    def ids_spec(index):
        return pl.BlockSpec((None, 1, MOE_BLOCK), lambda b, be, nu: (index(b), 0, 0),
                            memory_space=pltpu.SMEM)

    row_buf = pltpu.VMEM((MOE_BLOCK, dh), jnp.uint32)
    return pl.pallas_call(
        functools.partial(_expert_kernel, ff=ff, n_real=n_real),
        grid_spec=pltpu.PrefetchScalarGridSpec(
            num_scalar_prefetch=2,
            grid=(n_blocks,),
            in_specs=[
                ids_spec(lambda b: b),
                ids_spec(lambda b: jnp.minimum(b + 1, n_blocks - 1)),
                ids_spec(lambda b: jnp.maximum(b - 1, 0)),
                pl.BlockSpec(memory_space=pl.ANY),
                pl.BlockSpec((None, None, d, 2 * ff),
                             lambda b, be, nu: (layer, be[blk(b, be, nu)], 0, 0)),
                pl.BlockSpec((None, None, ff, d),
                             lambda b, be, nu: (layer, be[blk(b, be, nu)], 0, 0)),
            ],
            out_specs=pl.BlockSpec(memory_space=pl.ANY),
            scratch_shapes=[row_buf, row_buf, row_buf, row_buf,
                            pltpu.VMEM((d, 2 * ff), BF16), pltpu.VMEM((ff, d), BF16),
                            pltpu.SemaphoreType.DMA((2,)), pltpu.SemaphoreType.DMA((2,))],
        ),
        out_shape=jax.ShapeDtypeStruct((n_real + 2 * MOE_BLOCK, dh), jnp.uint32),
        compiler_params=_cparams(1),
        name="moe_experts",
    )(block_e, n_used, src_ids, src_ids, dst_ids, hm, w_gu, w_down)


def _combine_kernel(*refs, alpha, n_p, with_h, ff):
    yk_refs = refs[:TOP_K]
    w8_ref, hm_ref, wgu_ref, wd_ref, x_ref, gate_ref, g_ref, b_ref = refs[TOP_K:TOP_K + 8]
    rest = list(refs[TOP_K + 8:])
    wd_bf = rest.pop()
    wgu_bf = rest.pop()

    @pl.when(pl.program_id(0) == 0)
    def _():
        wgu_bf[...] = wgu_ref[...].astype(BF16)
        wd_bf[...] = wd_ref[...].astype(BF16)

    y_sh = _swiglu(_unpack_pairs_bf16(hm_ref[...]), wgu_bf[...], wd_bf[...], ff)
    w8 = w8_ref[...]
    half = yk_refs[0].shape[-1]
    acc_lo = y_sh[:, :half]
    acc_hi = y_sh[:, half:]
    for kk in range(TOP_K):
        rows = yk_refs[kk][...]
        wk = w8[:, kk:kk + 1]
        acc_lo = acc_lo + wk * _unpack_lo(rows)
        acc_hi = acc_hi + wk * _unpack_hi(rows)
    acc = jnp.concatenate([acc_lo, acc_hi], axis=-1)
    y = alpha * x_ref[...] + gate_ref[...] * acc
    xn = _layer_norm(y, g_ref[...], b_ref[...])
    if with_h:
        sh_ref, sc_ref, xo_ref, ho_ref = rest
        xo_ref[...] = xn
        ho_ref[...] = (xn * (1.0 + sc_ref[...]) + sh_ref[...]).astype(ho_ref.dtype)
    else:
        xp_ref, xs_ref = rest
        i = pl.program_id(0)

        @pl.when(i < n_p)
        def _():
            xp_ref[...] = xn

        @pl.when(i >= n_p)
        def _():
            xs_ref[...] = xn


def _combine(yk, w8, hm, ws_gu, ws_down, layer, x, gate, ln_g, ln_b, shift, scale, alpha, tp,
             seq_s):
    t, d = x.shape
    ff = ws_down.shape[1]
    tm = _row_tile(256, tp, seq_s)
    n_steps = t // tm
    n_p = tp // tm
    with_h = shift is not None
    cid = _cond_index(tm, tp, seq_s)
    row = pl.BlockSpec((tm, d), lambda i: (i, 0))
    packed_row = pl.BlockSpec((tm, d // 2), lambda i: (i, 0))
    mod_spec = pl.BlockSpec((None, 1, d), lambda i: (cid(i), 0, 0))
    vec = pl.BlockSpec((1, d), lambda i: (0, 0))
    resident = pl.Buffered(1)
    in_specs = [pl.BlockSpec((tm, d // 2), lambda i, kk=kk: (kk * n_steps + i, 0))
                for kk in range(TOP_K)]
    in_specs += [pl.BlockSpec((tm, TOP_K), lambda i: (i, 0)), packed_row,
                 pl.BlockSpec((None, d, 2 * ff), lambda i: (layer, 0, 0), pipeline_mode=resident),
                 pl.BlockSpec((None, ff, d), lambda i: (layer, 0, 0), pipeline_mode=resident),
                 row, mod_spec, vec, vec]
    args = [yk] * TOP_K + [w8, hm, ws_gu, ws_down, x, gate, ln_g.reshape(1, d),
                           ln_b.reshape(1, d)]
    if with_h:
        in_specs += [mod_spec, mod_spec]
        args += [shift, scale]
        out_specs = [row, row]
        out_shape = [jax.ShapeDtypeStruct((t, d), F32), jax.ShapeDtypeStruct((t, d), BF16)]
    else:
        out_specs = [pl.BlockSpec((tm, d), lambda i: (jnp.minimum(i, n_p - 1), 0)),
                     pl.BlockSpec((tm, d), lambda i: (jnp.maximum(i - n_p, 0), 0))]
        out_shape = [jax.ShapeDtypeStruct((tp, d), F32), jax.ShapeDtypeStruct((t - tp, d), F32)]
    return pl.pallas_call(
        functools.partial(_combine_kernel, alpha=alpha, n_p=n_p, with_h=with_h, ff=ff),
        grid=(n_steps,),
        in_specs=in_specs,
        out_specs=out_specs,
        out_shape=out_shape,
        scratch_shapes=[pltpu.VMEM((d, 2 * ff), BF16), pltpu.VMEM((ff, d), BF16)],
        compiler_params=_cparams(1),
        name="moe_combine",
    )(*args)


def _expert_row_order(n_exp):
    per = n_exp // N_GROUPS
    r = jnp.arange(n_exp)
    return (r % N_GROUPS) * per + r // N_GROUPS


def _count_le(edges, v):
    return jnp.sum((edges[None, :] <= v[:, None]).astype(jnp.int32), axis=1)


def _moe_layer(x, mixer_out, layer, pre, w_r, b_r, w_gu, w_down, ws_gu, ws_down, gate, ln_g, ln_b,
               shift, scale, alpha, tp, seq_s):
    t = x.shape[0]
    n_exp = w_r.shape[1]
    order = _expert_row_order(n_exp)
    x, hm, sel, wd, pos, cnt = _norm_route(x, mixer_out, *pre, w_r.T[order],
                                           b_r[order].reshape(n_exp, 1), alpha, tp, seq_s)

    assert (t * TOP_K) % MOE_BLOCK == 0
    n_blocks = t * TOP_K // MOE_BLOCK + n_exp
    cap = n_blocks * MOE_BLOCK
    counts = cnt[:, 0].astype(jnp.int32)
    padded = (counts + MOE_BLOCK - 1) // MOE_BLOCK * MOE_BLOCK
    pend = jnp.cumsum(padded)
    pstart = pend - padded
    n_used = (pend[-1:] // MOE_BLOCK).astype(jnp.int32)
    block_row = jnp.minimum(_count_le(pend, jnp.arange(n_blocks) * MOE_BLOCK), n_exp - 1)
    per = n_exp // N_GROUPS
    block_e = ((block_row % N_GROUPS) * per + block_row // N_GROUPS).astype(jnp.int32)
    dest8, w8 = _slots(sel, wd, pos, pstart.astype(F32).reshape(n_exp, 1))

    padlen = padded - counts
    cpad_start = jnp.cumsum(padlen) - padlen
    f = jnp.arange(cap - t * TOP_K)
    fill = f + jnp.sum(jnp.where(cpad_start[None, :] <= f[:, None], counts[None, :], 0), axis=1)
    position = jnp.concatenate([dest8.reshape(-1), fill.astype(jnp.int32)])
    _, assign = lax.sort((position, jnp.arange(cap, dtype=jnp.int32)), num_keys=1, is_stable=False)
    src_ids = (assign % t).reshape(n_blocks, 1, MOE_BLOCK)
    n_real = t * TOP_K
    spare = n_real + jnp.arange(cap, dtype=jnp.int32) % (2 * MOE_BLOCK)
    dst_ids = jnp.where(assign >= n_real, spare, assign).reshape(n_blocks, 1, MOE_BLOCK)

    yk = _experts(hm, src_ids, dst_ids, block_e, n_used, w_gu, w_down, layer, n_real)
    return _combine(yk, w8.T, hm, ws_gu, ws_down, layer, x, gate, ln_g, ln_b, shift, scale, alpha,
                    tp, seq_s)


def _gla_layer(h, state_gla, new_state, layer_j, w_in, w_a1, w_a2, b_a, norm_g, w_out, dims):
    bp, sp, bs, ss = dims
    tp = bp * sp
    d = h.shape[1]
    rank = w_a1.shape[3]
    proj = _matmul(h, w_in, layer_j, BF16)
    w_low = jnp.transpose(w_a1[layer_j], (1, 0, 2)).reshape(1, d, 2 * rank)
    low = _matmul(h, w_low, 0, F32)
    low = jnp.transpose(low.reshape(-1, 2, rank), (1, 0, 2))
    og_p, st = _gla_scan(proj, low, w_a2[layer_j], b_a[layer_j], norm_g[layer_j], None, layer_j,
                         batch=bp, seq_len=sp, row0=0, hp=GLA_HEADS,
                         new_state="first" if new_state is None else new_state,
                         n_state_layers=w_in.shape[0])
    (og_s,) = _gla_scan(proj, low, w_a2[layer_j], b_a[layer_j], norm_g[layer_j], state_gla,
                        layer_j, batch=bs, seq_len=ss, row0=tp, hp=GLA_HEADS // 2)
    out = _matmul((og_p, og_s), w_out, layer_j, BF16)
    return out, st


def _sgu_layer(h, layer_j, w_in, ln_g, ln_b, w_s, b_s, w_out):
    z = _matmul(h, w_in, layer_j, BF16, act="gelu")
    gated = _sgu_gate(z, ln_g[layer_j], ln_b[layer_j], w_s[layer_j], b_s[layer_j])
    return _matmul(gated, w_out, layer_j, BF16)


def _grid_pos_embed(rows, d):
    t = jnp.arange(rows * GRID_W)
    r = (t // GRID_W).astype(F32)
    col = (t % GRID_W).astype(F32)
    quarter = d // 4
    omega = 1.0 / (10000.0 ** (jnp.arange(quarter, dtype=F32) / quarter))

    def emb(p):
        a = p[:, None] * omega[None, :]
        return jnp.concatenate([jnp.sin(a), jnp.cos(a)], axis=-1)

    return jnp.concatenate([emb(r), emb(col)], axis=-1)


def kernel(x_prompt, x_sample, state_gla, c, c_ctx, w_ada, b_ada, ln_g, ln_b, gla_w_in, gla_w_a1, gla_w_a2, gla_b_a, gla_norm_g, gla_w_out, sgu_w_in, sgu_ln_g, sgu_ln_b, sgu_w_s, sgu_b_s, sgu_w_out, moe_w_router, moe_b_router, moe_w_gu, moe_w_down, moe_ws_gu, moe_ws_down):
    bp, sp, d = x_prompt.shape
    bs, ss, _ = x_sample.shape
    assert 1 + bs <= COND_ROWS
    tp, ts = bp * sp, bs * ss
    depth = w_ada.shape[0]
    alpha = (2 * depth) ** 0.25
    dims = (bp, sp, bs, ss)

    cond = jnp.zeros((COND_ROWS, d), F32).at[0].set(c_ctx).at[1:1 + bs].set(c)
    mods = _ada_mods(cond, w_ada, b_ada).reshape(depth, COND_ROWS, 6, 1, d)

    def mod(l, i):
        return mods[l, :, i]

    pos = _grid_pos_embed(ss // GRID_W, d).astype(x_sample.dtype)
    x, h = _prep(x_prompt.reshape(tp, d), x_sample.reshape(ts, d), pos, mod(0, 0), mod(0, 1), ss)

    new_state = None
    for l in range(depth):
        j = l // 2
        if l % 2 == 0:
            out, new_state = _gla_layer(h, state_gla, new_state, j, gla_w_in, gla_w_a1, gla_w_a2,
                                        gla_b_a, gla_norm_g, gla_w_out, dims)
        else:
            out = _sgu_layer(h, j, sgu_w_in, sgu_ln_g, sgu_ln_b, sgu_w_s, sgu_b_s, sgu_w_out)
        pre = (mod(l, 2), ln_g[l, 0], ln_b[l, 0], mod(l, 3), mod(l, 4))
        last = l == depth - 1
        res = _moe_layer(x, out, l, pre, moe_w_router[l], moe_b_router[l], moe_w_gu, moe_w_down,
                         moe_ws_gu, moe_ws_down, mod(l, 5), ln_g[l, 1], ln_b[l, 1],
                         None if last else mod(l + 1, 0), None if last else mod(l + 1, 1),
                         alpha, tp, ss)
        if last:
            y_prompt, y_sample = res
        else:
            x, h = res

    return y_prompt.reshape(bp, sp, d), y_sample.reshape(bs, ss, d), new_state
```

```python
import functools

import jax
import jax.numpy as jnp
from jax import lax
from jax.experimental import pallas as pl
from jax.experimental.pallas import tpu as pltpu

F32 = jnp.float32
BF16 = jnp.bfloat16

V7X_VMEM_LIMIT_BYTES = 56 * 1024 * 1024
LANES = 128
SUBLANES = 8

GRID_W = 64
GLA_HEADS = 4
GLA_CHUNK = 64
GLA_NORMALIZER = 16.0
SGU_GROUPS = 4
SGU_CHUNK = 128
TOP_K = 8
N_GROUPS = 8
TOPK_GROUPS = 4
ROUTE_SCALE = 2.5
LN_EPS = 1e-5
RMS_EPS = 1e-6
MOE_BLOCK = 512
COND_ROWS = 8


def _cparams(n_axes):
    return pltpu.CompilerParams(
        dimension_semantics=("arbitrary",) * n_axes,
        vmem_limit_bytes=V7X_VMEM_LIMIT_BYTES,
    )


def _silu(x):
    return x * jax.nn.sigmoid(x)


def _layer_norm(y, g, b):
    mu = jnp.mean(y, axis=-1, keepdims=True)
    d = y - mu
    var = jnp.mean(d * d, axis=-1, keepdims=True)
    return d * lax.rsqrt(var + LN_EPS) * g + b


_HI16 = 0xFFFF0000


def _pack_pairs(x):
    half = x.shape[-1] // 2
    lo = lax.bitcast_convert_type(x[:, :half].astype(BF16).astype(F32), jnp.uint32)
    hi = lax.bitcast_convert_type(x[:, half:].astype(BF16).astype(F32), jnp.uint32)
    return lax.shift_right_logical(lo, jnp.uint32(16)) | (hi & jnp.uint32(_HI16))


def _unpack_lo(w):
    return lax.bitcast_convert_type(lax.shift_left(w, jnp.uint32(16)), F32)


def _unpack_hi(w):
    return lax.bitcast_convert_type(w & jnp.uint32(_HI16), F32)


def _unpack_pairs_bf16(w):
    return jnp.concatenate([_unpack_lo(w).astype(BF16), _unpack_hi(w).astype(BF16)], axis=-1)


def _row_tile(limit, *sizes):
    tm = limit
    while any(s % tm for s in sizes):
        tm //= 2
    return tm


def _cond_index(tm, tp, seq_s):
    def index(i):
        return jnp.where(i * tm < tp, 0, 1 + (i * tm - tp) // seq_s)

    return index


def _ada_kernel(c_ref, w_ref, b_ref, o_ref):
    s = _silu(c_ref[...]).astype(BF16)
    o_ref[...] = jnp.dot(s, w_ref[...].astype(BF16), preferred_element_type=F32) + b_ref[...]


def _ada_mods(cond, w_ada, b_ada):
    depth, d, n = w_ada.shape
    tn = _row_tile(1024, n)
    return pl.pallas_call(
        _ada_kernel,
        grid=(depth, n // tn),
        in_specs=[
            pl.BlockSpec((COND_ROWS, d), lambda l, j: (0, 0)),
            pl.BlockSpec((None, d, tn), lambda l, j: (l, 0, j)),
            pl.BlockSpec((None, 1, tn), lambda l, j: (l, 0, j)),
        ],
        out_specs=pl.BlockSpec((None, COND_ROWS, tn), lambda l, j: (l, 0, j)),
        out_shape=jax.ShapeDtypeStruct((depth, COND_ROWS, n), F32),
        compiler_params=_cparams(2),
        name="ada_mods",
    )(cond, w_ada, b_ada.reshape(depth, 1, n))


def _prep_kernel(xp_ref, xs_ref, pos_ref, sh_ref, sc_ref, x_ref, h_ref, *, n_p):
    x = jnp.where(pl.program_id(0) < n_p, xp_ref[...], xs_ref[...] + pos_ref[...])
    x_ref[...] = x
    h_ref[...] = (x * (1.0 + sc_ref[...]) + sh_ref[...]).astype(h_ref.dtype)


def _prep(xp, xs, pos, shift, scale, seq_s):
    tp, d = xp.shape
    ts = xs.shape[0]
    tm = _row_tile(512, tp, seq_s)
    n_p = tp // tm
    n_pos = seq_s // tm
    cid = _cond_index(tm, tp, seq_s)
    mod_spec = pl.BlockSpec((None, 1, d), lambda i: (cid(i), 0, 0))
    return pl.pallas_call(
        functools.partial(_prep_kernel, n_p=n_p),
        grid=((tp + ts) // tm,),
        in_specs=[
            pl.BlockSpec((tm, d), lambda i: (jnp.minimum(i, n_p - 1), 0)),
            pl.BlockSpec((tm, d), lambda i: (jnp.maximum(i - n_p, 0), 0)),
            pl.BlockSpec((tm, d), lambda i: (jnp.maximum(i - n_p, 0) % n_pos, 0)),
            mod_spec,
            mod_spec,
        ],
        out_specs=[pl.BlockSpec((tm, d), lambda i: (i, 0))] * 2,
        out_shape=[
            jax.ShapeDtypeStruct((tp + ts, d), F32),
            jax.ShapeDtypeStruct((tp + ts, d), BF16),
        ],
        compiler_params=_cparams(1),
        name="prep",
    )(xp, xs, pos, shift, scale)


def _mm_kernel(*refs, act, n_first):
    *x_refs, w_ref, o_ref, wbf_ref = refs

    @pl.when(pl.program_id(1) == 0)
    def _():
        wbf_ref[...] = w_ref[...].astype(BF16)

    x = x_refs[0][...]
    if len(x_refs) == 2:
        x = jnp.where(pl.program_id(1) < n_first, x, x_refs[1][...])
    acc = jnp.dot(x.astype(BF16), wbf_ref[...], preferred_element_type=F32)
    if act == "gelu":
        acc = 0.5 * acc * (1.0 + lax.erf(acc * 0.7071067811865476))
    o_ref[...] = acc.astype(o_ref.dtype)


def _matmul(x, w, layer, out_dtype, act=None):
    xs = x if isinstance(x, tuple) else (x,)
    x = xs[0]
    m = sum(a.shape[0] for a in xs)
    k = x.shape[1]
    n = w.shape[2]
    tn_limit = 1024
    while k * tn_limit * 10 > V7X_VMEM_LIMIT_BYTES * 4 // 7 and tn_limit > LANES:
        tn_limit //= 2
    tn = _row_tile(tn_limit, n)
    out_bytes = jnp.dtype(out_dtype).itemsize
    tm_limit = 1024
    while (k * tn * 10
           + tm_limit * (2 * len(xs) * k * x.dtype.itemsize + tn * (2 * out_bytes + 4))
           > V7X_VMEM_LIMIT_BYTES * 7 // 8 and tm_limit > SUBLANES):
        tm_limit //= 2
    tm = _row_tile(tm_limit, *(a.shape[0] for a in xs))
    n_first = xs[0].shape[0] // tm
    x_specs = [pl.BlockSpec((tm, k), lambda j, i: (jnp.minimum(i, n_first - 1), 0))]
    if len(xs) == 2:
        x_specs.append(pl.BlockSpec((tm, k), lambda j, i: (jnp.maximum(i - n_first, 0), 0)))
    return pl.pallas_call(
        functools.partial(_mm_kernel, act=act, n_first=n_first),
        grid=(n // tn, m // tm),
        in_specs=x_specs + [pl.BlockSpec((None, k, tn), lambda j, i: (layer, 0, j))],
        out_specs=pl.BlockSpec((tm, tn), lambda j, i: (i, j)),
        out_shape=jax.ShapeDtypeStruct((m, n), out_dtype),
        scratch_shapes=[pltpu.VMEM((k, tn), BF16)],
        compiler_params=_cparams(2),
        name="matmul_" + (act or "lin"),
    )(*xs, w)


def _prefix_sum_mm(tri, x):
    hi = x.astype(BF16)
    r1 = x - hi.astype(F32)
    mid = r1.astype(BF16)
    lo = (r1 - mid.astype(F32)).astype(BF16)
    out = jnp.dot(tri, hi, preferred_element_type=F32)
    out = out + jnp.dot(tri, mid, preferred_element_type=F32)
    return out + jnp.dot(tri, lo, preferred_element_type=F32)


def _gla_kernel(*refs, seq_len, hp, has_s0, out_state, layer_j):
    q_ref, k_ref, v_ref, g_ref, low_ref, wa2_ref, ba_ref, ng_ref = refs[:8]
    rest = list(refs[8:])
    s0_ref = rest.pop(0) if has_s0 else None
    if out_state == "next":
        rest.pop(0)
    o_ref = rest.pop(0)
    st_ref = rest.pop(0) if out_state else None
    gk_scr, oacc, s_t = rest

    c = GLA_CHUNK
    n = seq_len // c
    dk = q_ref.shape[1] // hp
    dv = v_ref.shape[1] // hp
    scale = dk ** -0.5

    for z in range(2):
        pre = jnp.dot(low_ref[z].astype(BF16), wa2_ref[z].astype(BF16),
                      preferred_element_type=F32) + ba_ref[z]
        log_sig = jnp.minimum(pre, 0.0) - jnp.log(1.0 + jnp.exp(-jnp.abs(pre)))
        gk_scr[z] = log_sig * (1.0 / GLA_NORMALIZER)
        for hh in range(hp):
            s_t[z, hh] = s0_ref[z, hh].T if has_s0 else jnp.zeros((dv, dk), F32)

    row = lax.broadcasted_iota(jnp.int32, (c, c), 0)
    col = lax.broadcasted_iota(jnp.int32, (c, c), 1)
    masks = (col <= row, col >= row)
    tris = tuple(jnp.where(m, 1.0, 0.0).astype(BF16) for m in masks)
    nt_dims = (((1,), (1,)), ((), ()))
    tn_dims = (((0,), (0,)), ((), ()))

    n_stages = 5

    def chunk_stages(ci, z, hh):
        rows = pl.ds(pl.multiple_of(ci * c, c), c)
        ck = slice(hh * dk, (hh + 1) * dk)
        cv = slice(hh * dv, (hh + 1) * dv)
        gcum = _prefix_sum_mm(tris[z], gk_scr[z, rows, ck])
        yield
        qc = q_ref[rows, ck].astype(F32) * scale
        kc = k_ref[rows, ck].astype(F32)
        if z == 0:
            g_mid = gcum[c // 2 - 1:c // 2]
            g_last = gcum[c - 1:c]
        else:
            g_mid = gcum[c // 2:c // 2 + 1]
            g_last = gcum[0:1]
        q_in = (qc * jnp.exp(gcum - g_mid)).astype(BF16)
        k_in = (kc * jnp.exp(g_mid - gcum)).astype(BF16)
        a = lax.dot_general(q_in, k_in, nt_dims, preferred_element_type=F32)
        yield
        vc = v_ref[rows, cv].astype(BF16)
        a = jnp.where(masks[z], a, 0.0).astype(BF16)
        o = jnp.dot(a, vc, preferred_element_type=F32)
        yield
        s_prev = s_t[z, hh]
        q_dec = (qc * jnp.exp(gcum)).astype(BF16)
        o = o + lax.dot_general(q_dec, s_prev.astype(BF16), nt_dims, preferred_element_type=F32)
        oacc[z, rows, cv] = o
        yield
        k_dec = (kc * jnp.exp(g_last - gcum)).astype(BF16)
        u_t = lax.dot_general(vc, k_dec, tn_dims, preferred_element_type=F32)
        s_t[z, hh] = s_prev * jnp.exp(g_last) + u_t
        yield

    def body(i, carry):
        scans = [chunk_stages(i if z == 0 else n - 1 - i, z, hh)
                 for z in range(2) for hh in range(hp)]
        for _ in range(n_stages):
            for scan in scans:
                next(scan)
        return carry

    lax.fori_loop(0, n, body, 0)

    for hh in range(hp):
        cv = slice(hh * dv, (hh + 1) * dv)
        for z in range(2):
            if out_state == "next":
                st_ref[z, hh] = s_t[z, hh].T
            elif out_state == "first":
                for j in range(st_ref.shape[0]):
                    st_ref[j, z, hh] = s_t[z, hh].T if j == layer_j else jnp.zeros((dk, dv), F32)
        o = oacc[0, :, cv] + oacc[1, :, cv]
        on = o * lax.rsqrt(jnp.mean(o * o, axis=-1, keepdims=True) + RMS_EPS) * ng_ref[...]
        o_ref[:, cv] = (on * _silu(g_ref[:, cv].astype(F32))).astype(o_ref.dtype)


def _gla_scan(proj, low, w_a2, b_a, norm_g, state, layer_j, *, batch, seq_len, row0, hp,
              new_state=None, n_state_layers=1):
    h = GLA_HEADS
    hk = w_a2.shape[2]
    dk = hk // h
    hv = (proj.shape[1] - 2 * hk) // 2
    dv = hv // h
    rank = w_a2.shape[1]
    off = row0 // seq_len
    has_s0 = state is not None
    wk, wv = hp * dk, hp * dv
    in_specs = [
        pl.BlockSpec((seq_len, wk), lambda b, i: (b + off, i)),
        pl.BlockSpec((seq_len, wk), lambda b, i: (b + off, hk // wk + i)),
        pl.BlockSpec((seq_len, wv), lambda b, i: (b + off, 2 * hk // wv + i)),
        pl.BlockSpec((seq_len, wv), lambda b, i: (b + off, (2 * hk + hv) // wv + i)),
        pl.BlockSpec((2, seq_len, rank), lambda b, i: (0, b + off, 0)),
        pl.BlockSpec((2, rank, wk), lambda b, i: (0, 0, i)),
        pl.BlockSpec((2, 1, wk), lambda b, i: (0, 0, i)),
        pl.BlockSpec((1, dv), lambda b, i: (0, 0)),
    ]
    args = [proj, proj, proj, proj, low, w_a2, b_a.reshape(2, 1, hk), norm_g.reshape(1, dv)]
    if has_s0:
        in_specs.append(pl.BlockSpec((None, None, 2, hp, dk, dv),
                                     lambda b, i: (b, layer_j, 0, i, 0, 0)))
        args.append(state)
    out_specs = [pl.BlockSpec((seq_len, wv), lambda b, i: (b, i))]
    out_shape = [jax.ShapeDtypeStruct((batch * seq_len, hv), BF16)]
    aliases = {}
    out_state = None
    if isinstance(new_state, str):
        out_state = "first"
        out_specs.append(pl.BlockSpec((None, n_state_layers, 2, hp, dk, dv),
                                      lambda b, i: (b, 0, 0, i, 0, 0)))
        out_shape.append(jax.ShapeDtypeStruct((batch, n_state_layers, 2, h, dk, dv), F32))
    elif new_state is not None:
        out_state = "next"
        aliases = {len(args): 1}
        in_specs.append(pl.BlockSpec(memory_space=pl.ANY))
        args.append(new_state)
        out_specs.append(pl.BlockSpec((None, None, 2, hp, dk, dv),
                                      lambda b, i: (b, layer_j, 0, i, 0, 0)))
        out_shape.append(jax.ShapeDtypeStruct(new_state.shape, F32))
    return pl.pallas_call(
        functools.partial(_gla_kernel, seq_len=seq_len, hp=hp, has_s0=has_s0,
                          out_state=out_state, layer_j=layer_j),
        grid=(batch, h // hp),
        in_specs=in_specs,
        out_specs=out_specs,
        out_shape=out_shape,
        input_output_aliases=aliases,
        scratch_shapes=[
            pltpu.VMEM((2, seq_len, wk), F32),
            pltpu.VMEM((2, seq_len, wv), F32),
            pltpu.VMEM((2, hp, dv, dk), F32),
        ],
        compiler_params=_cparams(2),
        name="gla_scan",
    )(*args)


def _sgu_kernel(u_ref, v_ref, lg_ref, lb_ref, ws_ref, bs_ref, o_ref):
    v = v_ref[...].astype(F32)
    vn = _layer_norm(v, lg_ref[...], lb_ref[...]).astype(BF16)
    rows, half = vn.shape
    cg = half // SGU_GROUPS
    for r in range(rows // SGU_CHUNK):
        rs = slice(r * SGU_CHUNK, (r + 1) * SGU_CHUNK)
        for g in range(SGU_GROUPS):
            cs = slice(g * cg, (g + 1) * cg)
            s = jnp.dot(ws_ref[g].astype(BF16), vn[rs, cs], preferred_element_type=F32) + bs_ref[g]
            o_ref[rs, cs] = (u_ref[rs, cs].astype(F32) * s).astype(o_ref.dtype)


def _sgu_gate(z, ln_g, ln_b, w_s, b_s):
    t, ffn = z.shape
    half = ffn // 2
    rows = SGU_CHUNK
    return pl.pallas_call(
        _sgu_kernel,
        grid=(t // rows,),
        in_specs=[
            pl.BlockSpec((rows, half), lambda i: (i, 0)),
            pl.BlockSpec((rows, half), lambda i: (i, 1)),
            pl.BlockSpec((1, half), lambda i: (0, 0)),
            pl.BlockSpec((1, half), lambda i: (0, 0)),
            pl.BlockSpec((SGU_GROUPS, SGU_CHUNK, SGU_CHUNK), lambda i: (0, 0, 0)),
            pl.BlockSpec((SGU_GROUPS, SGU_CHUNK, 1), lambda i: (0, 0, 0)),
        ],
        out_specs=pl.BlockSpec((rows, half), lambda i: (i, 0)),
        out_shape=jax.ShapeDtypeStruct((t, half), BF16),
        compiler_params=_cparams(1),
        name="sgu_gate",
    )(z, z, ln_g.reshape(1, half), ln_b.reshape(1, half), w_s, b_s.reshape(SGU_GROUPS, SGU_CHUNK, 1))


def _norm_route_kernel(x_ref, o_ref, gate_ref, g_ref, b_ref, sh_ref, sc_ref, wrt_ref, br_ref,
                       xo_ref, ho_ref, sel_ref, wd_ref, pos_ref, cnt_ref, carry, *, alpha):
    i = pl.program_id(0)

    @pl.when(i == 0)
    def _():
        carry[...] = jnp.zeros_like(carry)

    y = alpha * x_ref[...] + gate_ref[...] * o_ref[...].astype(F32)
    xn = _layer_norm(y, g_ref[...], b_ref[...])
    xo_ref[...] = xn
    h = xn * (1.0 + sc_ref[...]) + sh_ref[...]
    ho_ref[...] = _pack_pairs(h)

    x = h.astype(BF16)
    logits = lax.dot_general(wrt_ref[...].astype(BF16), x, (((1,), (1,)), ((), ())),
                             preferred_element_type=F32)
    n_exp, tm = logits.shape
    ng = N_GROUPS
    per = n_exp // ng
    scores = jax.nn.sigmoid(logits)
    selv = scores + br_ref[...]
    a = [selv[j * ng:(j + 1) * ng] for j in range(per)]
    sc = [scores[j * ng:(j + 1) * ng] for j in range(per)]

    t1 = a[0]
    t2 = jnp.full_like(t1, -jnp.inf)
    for j in range(1, per):
        t2 = jnp.maximum(t2, jnp.minimum(t1, a[j]))
        t1 = jnp.maximum(t1, a[j])
    gscore = t1 + t2

    gidx = lax.broadcasted_iota(jnp.int32, (ng, tm), 0)
    grank = jnp.zeros((ng, tm), jnp.int32)
    for gp in range(ng):
        other = gscore[gp:gp + 1, :]
        beats = (other > gscore) | ((other == gscore) & (gp < gidx))
        grank = grank + beats.astype(jnp.int32)
    gmask = grank < TOPK_GROUPS

    val = [jnp.where(gmask, a[j], -jnp.inf) for j in range(per)]
    ranks = [jnp.zeros((ng, tm), jnp.int32) for _ in range(per)]
    for gp in range(ng):
        lower_group = gp < gidx
        lower_or_same_group = gp <= gidx
        for jp in range(per):
            other = val[jp][gp:gp + 1, :]
            for j in range(per):
                tie = lower_or_same_group if jp < j else lower_group
                beats = (other > val[j]) | ((other == val[j]) & tie)
                ranks[j] = ranks[j] + beats.astype(jnp.int32)
    selm = [(ranks[j] < TOP_K) & gmask for j in range(per)]

    ssum = jnp.zeros((ng, tm), F32)
    for j in range(per):
        ssum = ssum + jnp.where(selm[j], sc[j], 0.0)
    ssum = jnp.sum(ssum, axis=0, keepdims=True)
    wd = jnp.concatenate(
        [jnp.where(selm[j], sc[j] / ssum * ROUTE_SCALE, 0.0) for j in range(per)], axis=0)
    selmat = jnp.concatenate([jnp.where(selm[j], 1.0, 0.0) for j in range(per)], axis=0)

    s_i = lax.broadcasted_iota(jnp.int32, (tm, tm), 0)
    t_i = lax.broadcasted_iota(jnp.int32, (tm, tm), 1)
    upper = jnp.where(s_i < t_i, 1.0, 0.0).astype(BF16)
    base = carry[...][:, 0:1]
    pos = jnp.dot(selmat.astype(BF16), upper, preferred_element_type=F32) + base
    total = base + jnp.sum(selmat, axis=1, keepdims=True)
    carry[...] = jnp.broadcast_to(total, carry.shape)

    sel_ref[...] = selmat
    wd_ref[...] = wd
    pos_ref[...] = pos
    cnt_ref[...] = jnp.broadcast_to(total, cnt_ref.shape)


def _norm_route(x, out, gate, ln_g, ln_b, shift, scale, w_rt, b_r_col, alpha, tp, seq_s):
    t, d = x.shape
    n_exp = w_rt.shape[0]
    tm = _row_tile(512, tp, seq_s)
    cid = _cond_index(tm, tp, seq_s)
    row = pl.BlockSpec((tm, d), lambda i: (i, 0))
    mod_spec = pl.BlockSpec((None, 1, d), lambda i: (cid(i), 0, 0))
    vec = pl.BlockSpec((1, d), lambda i: (0, 0))
    col = pl.BlockSpec((n_exp, tm), lambda i: (0, i))
    return pl.pallas_call(
        functools.partial(_norm_route_kernel, alpha=alpha),
        grid=(t // tm,),
        in_specs=[row, row, mod_spec, vec, vec, mod_spec, mod_spec,
                  pl.BlockSpec((n_exp, d), lambda i: (0, 0)),
                  pl.BlockSpec((n_exp, 1), lambda i: (0, 0))],
        out_specs=[row, pl.BlockSpec((tm, d // 2), lambda i: (i, 0)), col, col, col,
                   pl.BlockSpec((n_exp, LANES), lambda i: (0, 0))],
        out_shape=[
            jax.ShapeDtypeStruct((t, d), F32),
            jax.ShapeDtypeStruct((t, d // 2), jnp.uint32),
            jax.ShapeDtypeStruct((n_exp, t), F32),
            jax.ShapeDtypeStruct((n_exp, t), F32),
            jax.ShapeDtypeStruct((n_exp, t), F32),
            jax.ShapeDtypeStruct((n_exp, LANES), F32),
        ],
        scratch_shapes=[pltpu.VMEM((n_exp, LANES), F32)],
        compiler_params=_cparams(1),
        name="norm_route",
    )(x, out, gate, ln_g.reshape(1, d), ln_b.reshape(1, d), shift, scale, w_rt, b_r_col)


def _slots_kernel(sel_ref, wd_ref, pos_ref, ps_ref, d8_ref, w8_ref):
    sel = sel_ref[...]
    n_exp = sel.shape[0]
    r_i = lax.broadcasted_iota(jnp.int32, (n_exp, n_exp), 0)
    c_i = lax.broadcasted_iota(jnp.int32, (n_exp, n_exp), 1)
    lower = jnp.where(c_i < r_i, 1.0, 0.0).astype(BF16)
    slot = jnp.dot(lower, sel.astype(BF16), preferred_element_type=F32)
    chosen = sel > 0.5
    dest = pos_ref[...] + ps_ref[...]
    wd = wd_ref[...]
    d_rows, w_rows = [], []
    for k in range(TOP_K):
        pick = chosen & (slot == k)
        d_rows.append(jnp.sum(jnp.where(pick, dest, 0.0), axis=0, keepdims=True))
        w_rows.append(jnp.sum(jnp.where(pick, wd, 0.0), axis=0, keepdims=True))
    d8_ref[...] = jnp.concatenate(d_rows, axis=0).astype(jnp.int32)
    w8_ref[...] = jnp.concatenate(w_rows, axis=0)


def _slots(sel, wd, pos, pstart_col):
    n_exp, t = sel.shape
    tm = _row_tile(512, t)
    col = pl.BlockSpec((n_exp, tm), lambda i: (0, i))
    out = pl.BlockSpec((TOP_K, tm), lambda i: (0, i))
    return pl.pallas_call(
        _slots_kernel,
        grid=(t // tm,),
        in_specs=[col, col, col, pl.BlockSpec((n_exp, 1), lambda i: (0, 0))],
        out_specs=[out, out],
        out_shape=[jax.ShapeDtypeStruct((TOP_K, t), jnp.int32),
                   jax.ShapeDtypeStruct((TOP_K, t), F32)],
        compiler_params=_cparams(1),
        name="moe_slots",
    )(sel, wd, pos, pstart_col)


def _swiglu(x, wgu, wd, ff):
    h = jnp.dot(x, wgu, preferred_element_type=F32)
    act = (_silu(h[:, :ff]) * h[:, ff:]).astype(BF16)
    return jnp.dot(act, wd, preferred_element_type=F32)


def _expert_kernel(be_ref, nu_ref, eo_ref, nx_ref, src_cur, src_next, dst_prev, hm_hbm, wgu_hbm,
                   wd_hbm, y_hbm, xbuf0, xbuf1, ybuf0, ybuf1, wgu_f, wd_f, wgu_bf, wd_bf, gsem,
                   ssem, wsem, *, ff, n_real, layer):
    b = pl.program_id(0)
    nu = nu_ref[0]
    xbufs, ybufs = (xbuf0, xbuf1), (ybuf0, ybuf1)
    rows = MOE_BLOCK

    def gather(ids, par):
        for i in range(rows):
            pltpu.make_async_copy(hm_hbm.at[pl.ds(ids[0, i], 1), :],
                                  xbufs[par].at[pl.ds(i, 1), :], gsem.at[par]).start()

    def scatter(ids, par):
        for i in range(rows):
            pltpu.make_async_copy(ybufs[par].at[pl.ds(i, 1), :],
                                  y_hbm.at[pl.ds(ids[0, i], 1), :], ssem.at[par]).start()

    def weight_copies(e, slot, what):
        for hbm, buf in ((wgu_hbm, wgu_f), (wd_hbm, wd_f)):
            copy = pltpu.make_async_copy(hbm.at[layer, e], buf.at[slot], wsem.at[slot])
            if what == "start":
                copy.start(priority=1)
            else:
                copy.wait()

    def wait_gather(par):
        pltpu.make_async_copy(hm_hbm.at[pl.ds(0, rows), :], xbufs[par], gsem.at[par]).wait()

    def wait_scatter(par):
        pltpu.make_async_copy(ybufs[par], y_hbm.at[pl.ds(0, rows), :], ssem.at[par]).wait()

    def block_step(par, first):
        if first:
            for k in range(2):
                ybufs[k][...] = jnp.zeros_like(ybufs[k])
                spare = y_hbm.at[pl.ds(n_real + k * rows, rows), :]
                pltpu.make_async_copy(ybufs[k], spare, ssem.at[k]).start()
            for k in range(2):
                wait_scatter(k)
            gather(src_cur, par)
            weight_copies(be_ref[0], 0, "start")
        wait_gather(par)

        @pl.when((b == 0) | (be_ref[b] != be_ref[jnp.maximum(b - 1, 0)]))
        def _():
            slot = eo_ref[b] % 2
            weight_copies(be_ref[b], slot, "wait")
            wgu_bf[...] = wgu_f[slot].astype(BF16)
            wd_bf[...] = wd_f[slot].astype(BF16)

            @pl.when(nx_ref[b] >= 0)
            def _():
                weight_copies(nx_ref[b], 1 - slot, "start")

        gather(src_next, 1 - par)
        if not first:
            scatter(dst_prev, 1 - par)
        y = _swiglu(_unpack_pairs_bf16(xbufs[par][...]), wgu_bf[...], wd_bf[...], ff)
        if not first:
            @pl.when(b >= 2)
            def _():
                wait_scatter(par)
        ybufs[par][...] = _pack_pairs(y)

    @pl.when(b == 0)
    def _():
        block_step(0, True)

    for par in range(2):
        @pl.when((b > 0) & (b < nu) & (b % 2 == par))
        def _(par=par):
            block_step(par, False)

        @pl.when((b == nu) & (b % 2 == par))
        def _(par=par):
            wait_gather(par)
            scatter(dst_prev, 1 - par)

            @pl.when(b >= 2)
            def _():
                wait_scatter(par)

            wait_scatter(1 - par)


def _experts(hm, src_ids, dst_ids, block_e, n_used, e_ord, e_next, w_gu, w_down, layer, n_real):
    dh = hm.shape[1]
    d = 2 * dh
    ff = w_down.shape[2]
    n_blocks = src_ids.shape[0]

    def ids_spec(index):
        return pl.BlockSpec((None, 1, MOE_BLOCK), lambda b, be, nu, eo, nx: (index(b), 0, 0),
                            memory_space=pltpu.SMEM)

    row_buf = pltpu.VMEM((MOE_BLOCK, dh), jnp.uint32)
    hbm = pl.BlockSpec(memory_space=pl.ANY)
    return pl.pallas_call(
        functools.partial(_expert_kernel, ff=ff, n_real=n_real, layer=layer),
        grid_spec=pltpu.PrefetchScalarGridSpec(
            num_scalar_prefetch=4,
            grid=(n_blocks,),
            in_specs=[
                ids_spec(lambda b: b),
                ids_spec(lambda b: jnp.minimum(b + 1, n_blocks - 1)),
                ids_spec(lambda b: jnp.maximum(b - 1, 0)),
                hbm, hbm, hbm,
            ],
            out_specs=pl.BlockSpec(memory_space=pl.ANY),
            scratch_shapes=[row_buf, row_buf, row_buf, row_buf,
                            pltpu.VMEM((2, d, 2 * ff), F32), pltpu.VMEM((2, ff, d), F32),
                            pltpu.VMEM((d, 2 * ff), BF16), pltpu.VMEM((ff, d), BF16),
                            pltpu.SemaphoreType.DMA((2,)), pltpu.SemaphoreType.DMA((2,)),
                            pltpu.SemaphoreType.DMA((2,))],
        ),
        out_shape=jax.ShapeDtypeStruct((n_real + 2 * MOE_BLOCK, dh), jnp.uint32),
        compiler_params=_cparams(1),
        name="moe_experts",
    )(block_e, n_used, e_ord, e_next, src_ids, src_ids, dst_ids, hm, w_gu, w_down)


def _combine_kernel(*refs, alpha, n_p, with_h, ff):
    yk_refs = refs[:TOP_K]
    w8_ref, hm_ref, wgu_ref, wd_ref, x_ref, gate_ref, g_ref, b_ref = refs[TOP_K:TOP_K + 8]
    rest = list(refs[TOP_K + 8:])
    wd_bf = rest.pop()
    wgu_bf = rest.pop()

    @pl.when(pl.program_id(0) == 0)
    def _():
        wgu_bf[...] = wgu_ref[...].astype(BF16)
        wd_bf[...] = wd_ref[...].astype(BF16)

    y_sh = _swiglu(_unpack_pairs_bf16(hm_ref[...]), wgu_bf[...], wd_bf[...], ff)
    w8 = w8_ref[...]
    half = yk_refs[0].shape[-1]
    acc_lo = y_sh[:, :half]
    acc_hi = y_sh[:, half:]
    for kk in range(TOP_K):
        rows = yk_refs[kk][...]
        wk = w8[:, kk:kk + 1]
        acc_lo = acc_lo + wk * _unpack_lo(rows)
        acc_hi = acc_hi + wk * _unpack_hi(rows)
    acc = jnp.concatenate([acc_lo, acc_hi], axis=-1)
    y = alpha * x_ref[...] + gate_ref[...] * acc
    xn = _layer_norm(y, g_ref[...], b_ref[...])
    if with_h:
        sh_ref, sc_ref, xo_ref, ho_ref = rest
        xo_ref[...] = xn
        ho_ref[...] = (xn * (1.0 + sc_ref[...]) + sh_ref[...]).astype(ho_ref.dtype)
    else:
        xp_ref, xs_ref = rest
        i = pl.program_id(0)

        @pl.when(i < n_p)
        def _():
            xp_ref[...] = xn

        @pl.when(i >= n_p)
        def _():
            xs_ref[...] = xn


def _combine(yk, w8, hm, ws_gu, ws_down, layer, x, gate, ln_g, ln_b, shift, scale, alpha, tp,
             seq_s):
    t, d = x.shape
    ff = ws_down.shape[1]
    tm = _row_tile(256, tp, seq_s)
    n_steps = t // tm
    n_p = tp // tm
    with_h = shift is not None
    cid = _cond_index(tm, tp, seq_s)
    row = pl.BlockSpec((tm, d), lambda i: (i, 0))
    packed_row = pl.BlockSpec((tm, d // 2), lambda i: (i, 0))
    mod_spec = pl.BlockSpec((None, 1, d), lambda i: (cid(i), 0, 0))
    vec = pl.BlockSpec((1, d), lambda i: (0, 0))
    resident = pl.Buffered(1)
    in_specs = [pl.BlockSpec((tm, d // 2), lambda i, kk=kk: (kk * n_steps + i, 0))
                for kk in range(TOP_K)]
    in_specs += [pl.BlockSpec((tm, TOP_K), lambda i: (i, 0)), packed_row,
                 pl.BlockSpec((None, d, 2 * ff), lambda i: (layer, 0, 0), pipeline_mode=resident),
                 pl.BlockSpec((None, ff, d), lambda i: (layer, 0, 0), pipeline_mode=resident),
                 row, mod_spec, vec, vec]
    args = [yk] * TOP_K + [w8, hm, ws_gu, ws_down, x, gate, ln_g.reshape(1, d),
                           ln_b.reshape(1, d)]
    if with_h:
        in_specs += [mod_spec, mod_spec]
        args += [shift, scale]
        out_specs = [row, row]
        out_shape = [jax.ShapeDtypeStruct((t, d), F32), jax.ShapeDtypeStruct((t, d), BF16)]
    else:
        out_specs = [pl.BlockSpec((tm, d), lambda i: (jnp.minimum(i, n_p - 1), 0)),
                     pl.BlockSpec((tm, d), lambda i: (jnp.maximum(i - n_p, 0), 0))]
        out_shape = [jax.ShapeDtypeStruct((tp, d), F32), jax.ShapeDtypeStruct((t - tp, d), F32)]
    return pl.pallas_call(
        functools.partial(_combine_kernel, alpha=alpha, n_p=n_p, with_h=with_h, ff=ff),
        grid=(n_steps,),
        in_specs=in_specs,
        out_specs=out_specs,
        out_shape=out_shape,
        scratch_shapes=[pltpu.VMEM((d, 2 * ff), BF16), pltpu.VMEM((ff, d), BF16)],
        compiler_params=_cparams(1),
        name="moe_combine",
    )(*args)


def _expert_row_order(n_exp):
    per = n_exp // N_GROUPS
    r = jnp.arange(n_exp)
    return (r % N_GROUPS) * per + r // N_GROUPS


def _count_le(edges, v):
    return jnp.sum((edges[None, :] <= v[:, None]).astype(jnp.int32), axis=1)


def _moe_layer(x, mixer_out, layer, pre, w_r, b_r, w_gu, w_down, ws_gu, ws_down, gate, ln_g, ln_b,
               shift, scale, alpha, tp, seq_s):
    t = x.shape[0]
    n_exp = w_r.shape[1]
    order = _expert_row_order(n_exp)
    x, hm, sel, wd, pos, cnt = _norm_route(x, mixer_out, *pre, w_r.T[order],
                                           b_r[order].reshape(n_exp, 1), alpha, tp, seq_s)

    assert (t * TOP_K) % MOE_BLOCK == 0
    n_blocks = t * TOP_K // MOE_BLOCK + n_exp
    cap = n_blocks * MOE_BLOCK
    counts = cnt[:, 0].astype(jnp.int32)
    padded = (counts + MOE_BLOCK - 1) // MOE_BLOCK * MOE_BLOCK
    pend = jnp.cumsum(padded)
    pstart = pend - padded
    n_used = (pend[-1:] // MOE_BLOCK).astype(jnp.int32)
    block_row = jnp.minimum(_count_le(pend, jnp.arange(n_blocks) * MOE_BLOCK), n_exp - 1)
    per = n_exp // N_GROUPS
    block_e = ((block_row % N_GROUPS) * per + block_row // N_GROUPS).astype(jnp.int32)
    dest8, w8 = _slots(sel, wd, pos, pstart.astype(F32).reshape(n_exp, 1))

    padlen = padded - counts
    cpad_start = jnp.cumsum(padlen) - padlen
    f = jnp.arange(cap - t * TOP_K)
    fill = f + jnp.sum(jnp.where(cpad_start[None, :] <= f[:, None], counts[None, :], 0), axis=1)
    position = jnp.concatenate([dest8.reshape(-1), fill.astype(jnp.int32)])
    _, assign = lax.sort((position, jnp.arange(cap, dtype=jnp.int32)), num_keys=1, is_stable=False)
    src_ids = (assign % t).reshape(n_blocks, 1, MOE_BLOCK)
    n_real = t * TOP_K
    spare = n_real + jnp.arange(cap, dtype=jnp.int32) % (2 * MOE_BLOCK)
    dst_ids = jnp.where(assign >= n_real, spare, assign).reshape(n_blocks, 1, MOE_BLOCK)

    rows64 = jnp.arange(n_exp)
    has_tokens = counts > 0
    ordinal = jnp.cumsum(has_tokens.astype(jnp.int32)) - 1
    later = (rows64[None, :] > rows64[:, None]) & has_tokens[None, :]
    next_row = jnp.min(jnp.where(later, rows64[None, :], n_exp), axis=1)
    next_e = jnp.where(next_row < n_exp, (next_row % N_GROUPS) * per + next_row // N_GROUPS, -1)
    e_ord = ordinal[block_row].astype(jnp.int32)
    e_next = next_e[block_row].astype(jnp.int32)

    yk = _experts(hm, src_ids, dst_ids, block_e, n_used, e_ord, e_next, w_gu, w_down, layer,
                  n_real)
    return _combine(yk, w8.T, hm, ws_gu, ws_down, layer, x, gate, ln_g, ln_b, shift, scale, alpha,
                    tp, seq_s)


def _gla_layer(h, state_gla, new_state, layer_j, w_in, w_a1, w_a2, b_a, norm_g, w_out, dims):
    bp, sp, bs, ss = dims
    tp = bp * sp
    d = h.shape[1]
    rank = w_a1.shape[3]
    proj = _matmul(h, w_in, layer_j, BF16)
    w_low = jnp.transpose(w_a1[layer_j], (1, 0, 2)).reshape(1, d, 2 * rank)
    low = _matmul(h, w_low, 0, F32)
    low = jnp.transpose(low.reshape(-1, 2, rank), (1, 0, 2))
    og_p, st = _gla_scan(proj, low, w_a2[layer_j], b_a[layer_j], norm_g[layer_j], None, layer_j,
                         batch=bp, seq_len=sp, row0=0, hp=GLA_HEADS,
                         new_state="first" if new_state is None else new_state,
                         n_state_layers=w_in.shape[0])
    (og_s,) = _gla_scan(proj, low, w_a2[layer_j], b_a[layer_j], norm_g[layer_j], state_gla,
                        layer_j, batch=bs, seq_len=ss, row0=tp, hp=GLA_HEADS // 2)
    out = _matmul((og_p, og_s), w_out, layer_j, BF16)
    return out, st


def _sgu_layer(h, layer_j, w_in, ln_g, ln_b, w_s, b_s, w_out):
    z = _matmul(h, w_in, layer_j, BF16, act="gelu")
    gated = _sgu_gate(z, ln_g[layer_j], ln_b[layer_j], w_s[layer_j], b_s[layer_j])
    return _matmul(gated, w_out, layer_j, BF16)


def _grid_pos_embed(rows, d):
    t = jnp.arange(rows * GRID_W)
    r = (t // GRID_W).astype(F32)
    col = (t % GRID_W).astype(F32)
    quarter = d // 4
    omega = 1.0 / (10000.0 ** (jnp.arange(quarter, dtype=F32) / quarter))

    def emb(p):
        a = p[:, None] * omega[None, :]
        return jnp.concatenate([jnp.sin(a), jnp.cos(a)], axis=-1)

    return jnp.concatenate([emb(r), emb(col)], axis=-1)


def kernel(x_prompt, x_sample, state_gla, c, c_ctx, w_ada, b_ada, ln_g, ln_b, gla_w_in, gla_w_a1, gla_w_a2, gla_b_a, gla_norm_g, gla_w_out, sgu_w_in, sgu_ln_g, sgu_ln_b, sgu_w_s, sgu_b_s, sgu_w_out, moe_w_router, moe_b_router, moe_w_gu, moe_w_down, moe_ws_gu, moe_ws_down):
    bp, sp, d = x_prompt.shape
    bs, ss, _ = x_sample.shape
    assert 1 + bs <= COND_ROWS
    tp, ts = bp * sp, bs * ss
    depth = w_ada.shape[0]
    alpha = (2 * depth) ** 0.25
    dims = (bp, sp, bs, ss)

    cond = jnp.zeros((COND_ROWS, d), F32).at[0].set(c_ctx).at[1:1 + bs].set(c)
    mods = _ada_mods(cond, w_ada, b_ada).reshape(depth, COND_ROWS, 6, 1, d)

    def mod(l, i):
        return mods[l, :, i]

    pos = _grid_pos_embed(ss // GRID_W, d).astype(x_sample.dtype)
    x, h = _prep(x_prompt.reshape(tp, d), x_sample.reshape(ts, d), pos, mod(0, 0), mod(0, 1), ss)

    new_state = None
    for l in range(depth):
        j = l // 2
        if l % 2 == 0:
            out, new_state = _gla_layer(h, state_gla, new_state, j, gla_w_in, gla_w_a1, gla_w_a2,
                                        gla_b_a, gla_norm_g, gla_w_out, dims)
        else:
            out = _sgu_layer(h, j, sgu_w_in, sgu_ln_g, sgu_ln_b, sgu_w_s, sgu_b_s, sgu_w_out)
        pre = (mod(l, 2), ln_g[l, 0], ln_b[l, 0], mod(l, 3), mod(l, 4))
        last = l == depth - 1
        res = _moe_layer(x, out, l, pre, moe_w_router[l], moe_b_router[l], moe_w_gu, moe_w_down,
                         moe_ws_gu, moe_ws_down, mod(l, 5), ln_g[l, 1], ln_b[l, 1],
                         None if last else mod(l + 1, 0), None if last else mod(l + 1, 1),
                         alpha, tp, ss)
        if last:
            y_prompt, y_sample = res
        else:
            x, h = res

    return y_prompt.reshape(bp, sp, d), y_sample.reshape(bs, ss, d), new_state
```
